```python
import jax, jax.numpy as jnp
from jax import lax
import numpy as np

D_MODEL = 2048
BATCH = 4
SEQ = 4096
DEPTH = 1

HEAD_DIM = 128
N_ATTN_HEADS = 8
ATTN_WIDTH = N_ATTN_HEADS * HEAD_DIM
CONV_WIDTH = D_MODEL - ATTN_WIDTH
CONV_GROUPS = CONV_WIDTH // HEAD_DIM
CONV_KSIZE = 3
IN_PROJ_WIDTH = 3 * ATTN_WIDTH + 3 * CONV_WIDTH

MOBA_BLOCK = 256
MOBA_TOPK = 3
QUERY_CHUNK = 32
ROPE_THETA = 10000.0

N_MEM = 256
N_XATTN_HEADS = 4
XATTN_HEAD_DIM = 128
XATTN_WIDTH = N_XATTN_HEADS * XATTN_HEAD_DIM

D_FF = -(-(8 * D_MODEL) // (3 * 256)) * 256
EPS = 1e-6

kernel_name = "hymba_moba_shortconv_hybrid_layer"


def rms_norm(x, g):
    xf = x.astype(jnp.float32)
    y = xf * lax.rsqrt(jnp.mean(xf * xf, axis=-1, keepdims=True) + EPS)
    return (y * g.astype(jnp.float32)).astype(x.dtype)


def rope_tables(positions):
    inv_freq = ROPE_THETA ** (-jnp.arange(0, HEAD_DIM, 2, dtype=jnp.float32) / HEAD_DIM)
    ang = positions.astype(jnp.float32)[..., None] * inv_freq
    return jnp.cos(ang), jnp.sin(ang)


def apply_rope(t, cos, sin):
    t1, t2 = jnp.split(t.astype(jnp.float32), 2, axis=-1)
    c = cos[:, :, None, :]
    s = sin[:, :, None, :]
    return jnp.concatenate([t1 * c - t2 * s, t2 * c + t1 * s], axis=-1).astype(t.dtype)


def moba_attention(q, k, v):
    bsz, n_h, s_len, hd = q.shape
    n_blk = -(-s_len // MOBA_BLOCK)
    s_pad = n_blk * MOBA_BLOCK
    top_k = min(MOBA_TOPK, n_blk)
    pad = ((0, 0), (0, 0), (0, s_pad - s_len), (0, 0))
    kp = jnp.pad(k, pad)
    vp = jnp.pad(v, pad)
    kb = kp.reshape(bsz, n_h, n_blk, MOBA_BLOCK, hd)
    vb = vp.reshape(bsz, n_h, n_blk, MOBA_BLOCK, hd)
    k_mean = jnp.mean(kb.astype(jnp.float32), axis=3)
    scale = hd ** -0.5
    b_idx = jnp.arange(bsz)[:, None, None, None]
    h_idx = jnp.arange(n_h)[None, :, None, None]
    blk_ids = jnp.arange(n_blk)
    key_off = jnp.arange(MOBA_BLOCK)

    def one_chunk(c):
        q0 = c * QUERY_CHUNK
        qc = lax.dynamic_slice_in_dim(q, q0, QUERY_CHUNK, axis=2)
        blk = q0 // MOBA_BLOCK
        qpos = q0 + jnp.arange(QUERY_CHUNK)
        gate = jnp.einsum('bhqd,bhnd->bhqn', qc.astype(jnp.float32), k_mean)
        gate = jnp.where(blk_ids < blk, gate, -jnp.inf)
        top_val, top_idx = lax.top_k(gate, top_k)
        sel_valid = jnp.isfinite(top_val)
        k_sel = kb[b_idx, h_idx, top_idx]
        v_sel = vb[b_idx, h_idx, top_idx]
        s_sel = jnp.einsum('bhqd,bhqrkd->bhqrk', qc, k_sel).astype(jnp.float32) * scale
        s_sel = jnp.where(sel_valid[..., None], s_sel, -jnp.inf)
        k_own = lax.dynamic_slice_in_dim(kp, blk * MOBA_BLOCK, MOBA_BLOCK, axis=2)
        v_own = lax.dynamic_slice_in_dim(vp, blk * MOBA_BLOCK, MOBA_BLOCK, axis=2)
        s_own = jnp.einsum('bhqd,bhkd->bhqk', qc, k_own).astype(jnp.float32) * scale
        kpos = blk * MOBA_BLOCK + key_off
        s_own = jnp.where(kpos[None, :] <= qpos[:, None], s_own, -jnp.inf)
        s_all = jnp.concatenate(
            [s_sel.reshape(bsz, n_h, QUERY_CHUNK, top_k * MOBA_BLOCK), s_own], axis=-1)
        p = jax.nn.softmax(s_all, axis=-1)
        p_sel = p[..., :top_k * MOBA_BLOCK].reshape(
            bsz, n_h, QUERY_CHUNK, top_k, MOBA_BLOCK).astype(v.dtype)
        p_own = p[..., top_k * MOBA_BLOCK:].astype(v.dtype)
        return (jnp.einsum('bhqrk,bhqrkd->bhqd', p_sel, v_sel)
                + jnp.einsum('bhqk,bhkd->bhqd', p_own, v_own))

    out = lax.map(one_chunk, jnp.arange(s_len // QUERY_CHUNK))
    return out.transpose(1, 2, 0, 3, 4).reshape(bsz, n_h, s_len, hd)


def short_conv(u, w_conv, b_conv):
    kern = w_conv[:, None, :].astype(u.dtype)
    y = lax.conv_general_dilated(
        u, kern, window_strides=(1,), padding=[(CONV_KSIZE - 1, 0)],
        dimension_numbers=('NWC', 'WIO', 'NWC'), feature_group_count=u.shape[-1])
    return y + b_conv


def hybrid_mixer(h, cos, sin, w_in, w_conv, b_conv, g_attn_out, g_conv_out, w_out):
    bsz, s_len, _ = h.shape
    proj = h @ w_in
    q, k, v, c_gate, b_gate, x_in = jnp.split(
        proj, np.cumsum([ATTN_WIDTH] * 3 + [CONV_WIDTH] * 2).tolist(), axis=-1)
    q = apply_rope(q.reshape(bsz, s_len, N_ATTN_HEADS, HEAD_DIM), cos, sin)
    k = apply_rope(k.reshape(bsz, s_len, N_ATTN_HEADS, HEAD_DIM), cos, sin)
    v = v.reshape(bsz, s_len, N_ATTN_HEADS, HEAD_DIM)
    o_attn = moba_attention(q.transpose(0, 2, 1, 3), k.transpose(0, 2, 1, 3),
                            v.transpose(0, 2, 1, 3)).transpose(0, 2, 1, 3)
    o_attn = rms_norm(o_attn, g_attn_out.reshape(N_ATTN_HEADS, HEAD_DIM))
    y_conv = b_gate * short_conv(c_gate * x_in, w_conv, b_conv)
    y_conv = rms_norm(y_conv.reshape(bsz, s_len, CONV_GROUPS, HEAD_DIM),
                      g_conv_out.reshape(CONV_GROUPS, HEAD_DIM))
    merged = jnp.concatenate([o_attn.reshape(bsz, s_len, ATTN_WIDTH),
                              y_conv.reshape(bsz, s_len, CONV_WIDTH)], axis=-1)
    return merged @ w_out


def memory_cross_attention(h, m, w_xq, w_xkv, w_xo):
    bsz, s_len, _ = h.shape
    q = (h @ w_xq).reshape(bsz, s_len, N_XATTN_HEADS, XATTN_HEAD_DIM)
    kv = (m @ w_xkv).reshape(bsz, m.shape[1], 2, N_XATTN_HEADS, XATTN_HEAD_DIM)
    k, v = kv[:, :, 0], kv[:, :, 1]
    s = jnp.einsum('bqhd,bmhd->bhqm', q, k).astype(jnp.float32) * (XATTN_HEAD_DIM ** -0.5)
    p = jax.nn.softmax(s, axis=-1).astype(v.dtype)
    o = jnp.einsum('bhqm,bmhd->bqhd', p, v).reshape(bsz, s_len, XATTN_WIDTH)
    return o @ w_xo


def swiglu(h, w_gate, w_up, w_down):
    return (jax.nn.silu(h @ w_gate) * (h @ w_up)) @ w_down


def setup_inputs(seed: int = 0) -> dict:
    key = jax.random.key(seed)
    ks = jax.random.split(key, 24)
    f32 = jnp.float32

    def nrm(k, shape, fan_in):
        return jax.random.normal(k, shape, f32) * (fan_in ** -0.5)

    def gain(k, shape):
        return 1.0 + 0.02 * jax.random.normal(k, shape, f32)

    x = jax.random.normal(ks[0], (BATCH, SEQ, D_MODEL), f32)
    mem = jax.random.normal(ks[1], (BATCH, N_MEM, D_MODEL), f32)
    offset = jax.random.randint(ks[2], (BATCH, 1), 0, SEQ, dtype=jnp.int32)
    positions = (offset + jnp.arange(SEQ, dtype=jnp.int32)[None, :]).astype(jnp.int32)
    return {
        "x": x,
        "mem": mem,
        "positions": positions,
        "g_mix": gain(ks[3], (DEPTH, D_MODEL)),
        "w_in": nrm(ks[4], (DEPTH, D_MODEL, IN_PROJ_WIDTH), D_MODEL),
        "w_conv": nrm(ks[5], (DEPTH, CONV_KSIZE, CONV_WIDTH), CONV_KSIZE),
        "b_conv": 0.02 * jax.random.normal(ks[6], (DEPTH, CONV_WIDTH), f32),
        "g_attn_out": gain(ks[7], (DEPTH, ATTN_WIDTH)),
        "g_conv_out": gain(ks[8], (DEPTH, CONV_WIDTH)),
        "w_out": nrm(ks[9], (DEPTH, D_MODEL, D_MODEL), D_MODEL),
        "g_xattn": gain(ks[10], (DEPTH, D_MODEL)),
        "g_mem": gain(ks[11], (DEPTH, D_MODEL)),
        "w_xq": nrm(ks[12], (DEPTH, D_MODEL, XATTN_WIDTH), D_MODEL),
        "w_xkv": nrm(ks[13], (DEPTH, D_MODEL, 2 * XATTN_WIDTH), D_MODEL),
        "w_xo": nrm(ks[14], (DEPTH, XATTN_WIDTH, D_MODEL), XATTN_WIDTH),
        "g_ffn": gain(ks[15], (DEPTH, D_MODEL)),
        "w_gate": nrm(ks[16], (DEPTH, D_MODEL, D_FF), D_MODEL),
        "w_up": nrm(ks[17], (DEPTH, D_MODEL, D_FF), D_MODEL),
        "w_down": nrm(ks[18], (DEPTH, D_FF, D_MODEL), D_FF),
        "g_final": gain(ks[19], (D_MODEL,)),
    }


def reference(x, mem, positions, g_mix, w_in, w_conv, b_conv, g_attn_out, g_conv_out, w_out,
              g_xattn, g_mem, w_xq, w_xkv, w_xo, g_ffn, w_gate, w_up, w_down, g_final):
    cos, sin = rope_tables(positions)
    for l in range(DEPTH):
        h = rms_norm(x, g_mix[l])
        x = x + hybrid_mixer(h, cos, sin, w_in[l], w_conv[l], b_conv[l],
                             g_attn_out[l], g_conv_out[l], w_out[l])
        h = rms_norm(x, g_xattn[l])
        m = rms_norm(mem, g_mem[l])
        x = x + memory_cross_attention(h, m, w_xq[l], w_xkv[l], w_xo[l])
        h = rms_norm(x, g_ffn[l])
        x = x + swiglu(h, w_gate[l], w_up[l], w_down[l])
    return rms_norm(x, g_final)
```

```python
import functools

import jax
import jax.numpy as jnp
from jax import lax
from jax.experimental import pallas as pl
from jax.experimental.pallas import tpu as pltpu

HEAD_DIM = 128
N_ATTN_HEADS = 8
ATTN_WIDTH = N_ATTN_HEADS * HEAD_DIM
CONV_KSIZE = 3
MOBA_BLOCK = 256
MOBA_TOPK = 3
ROPE_THETA = 10000.0
N_XATTN_HEADS = 4
XATTN_HEAD_DIM = 128
XATTN_WIDTH = N_XATTN_HEADS * XATTN_HEAD_DIM
EPS = 1e-6

MASK_VALUE = -1e30
SUBLANES = 8
VMEM_LIMIT = 56 * 1024 * 1024

ROW_TILE = 512
FF_TILE = 512

BF16 = jnp.bfloat16
F32 = jnp.float32


def _rms_norm(xf, g):
    y = xf * lax.rsqrt(jnp.mean(xf * xf, axis=-1, keepdims=True) + EPS)
    return y * g


def _dot(a, b):
    return jnp.dot(a, b, preferred_element_type=F32)


def _dot_nt(a, b):
    return lax.dot_general(a, b, (((1,), (1,)), ((), ())), preferred_element_type=F32)


def _params(*semantics):
    return pltpu.CompilerParams(dimension_semantics=semantics, vmem_limit_bytes=VMEM_LIMIT)


def _rope_table_kernel(pos_ref, inv_ref, sign_ref, cos_ref, sin_ref):
    ang = pos_ref[...] * inv_ref[...]
    cos_ref[...] = jnp.cos(ang)
    sin_ref[...] = jnp.sin(ang) * sign_ref[...]


def _rope_tables(positions):
    n_tok = positions.size
    half = HEAD_DIM // 2
    inv_freq = ROPE_THETA ** (-jnp.arange(0, HEAD_DIM, 2, dtype=F32) / HEAD_DIM)
    inv_full = jnp.concatenate([inv_freq, inv_freq]).reshape(1, HEAD_DIM)
    sign = jnp.concatenate([-jnp.ones((half,), F32), jnp.ones((half,), F32)]).reshape(1, HEAD_DIM)
    pos = positions.astype(F32).reshape(n_tok, 1)
    tm = 2048
    row = pl.BlockSpec((tm, HEAD_DIM), lambda i: (i, 0))
    const = pl.BlockSpec((1, HEAD_DIM), lambda i: (0, 0))
    return pl.pallas_call(
        _rope_table_kernel,
        grid=(n_tok // tm,),
        in_specs=[pl.BlockSpec((tm, 1), lambda i: (i, 0)), const, const],
        out_specs=[row, row],
        out_shape=[jax.ShapeDtypeStruct((n_tok, HEAD_DIM), F32)] * 2,
        compiler_params=_params("parallel"),
        name="rope_tables",
    )(pos, inv_full, sign)


def _qkv_kernel(x_ref, g_ref, w_ref, cos_ref, sin_ref, o_ref):
    h = _rms_norm(x_ref[...], g_ref[...]).astype(BF16)
    cos = cos_ref[...]
    sin = sin_ref[...]
    scale = HEAD_DIM ** -0.5
    for part in range(3):
        c0 = part * ATTN_WIDTH
        r = _dot(h, w_ref[:, c0:c0 + ATTN_WIDTH])
        for hh in range(N_ATTN_HEADS):
            lo = hh * HEAD_DIM
            t = r[:, lo:lo + HEAD_DIM]
            if part < 2:
                t = t * cos + pltpu.roll(t, HEAD_DIM // 2, axis=1) * sin
            if part == 0:
                t = t * scale
            o_ref[:, c0 + lo:c0 + lo + HEAD_DIM] = t.astype(BF16)


def _qkv_proj(x2d, g, w_in_bf, cos_t, sin_t):
    n_tok, d_model = x2d.shape
    tm = ROW_TILE
    width = 3 * ATTN_WIDTH
    return pl.pallas_call(
        _qkv_kernel,
        grid=(n_tok // tm,),
        in_specs=[
            pl.BlockSpec((tm, d_model), lambda i: (i, 0)),
            pl.BlockSpec((1, d_model), lambda i: (0, 0)),
            pl.BlockSpec((d_model, width), lambda i: (0, 0)),
            pl.BlockSpec((tm, HEAD_DIM), lambda i: (i, 0)),
            pl.BlockSpec((tm, HEAD_DIM), lambda i: (i, 0)),
        ],
        out_specs=pl.BlockSpec((tm, width), lambda i: (i, 0)),
        out_shape=jax.ShapeDtypeStruct((n_tok, width), BF16),
        compiler_params=_params("parallel"),
        name="qkv_proj",
    )(x2d, g, w_in_bf, cos_t, sin_t)


def _conv_kernel(tiles_per_seq, x_ref, g_ref, w_ref, wc_ref, bc_ref, gc_ref, o_ref, uext_ref):
    tm = x_ref.shape[0]
    cw = o_ref.shape[1]

    @pl.when(pl.program_id(0) % tiles_per_seq == 0)
    def _():
        uext_ref[0:SUBLANES, :] = jnp.zeros((SUBLANES, cw), F32)

    h = _rms_norm(x_ref[...], g_ref[...]).astype(BF16)
    c_gate = _dot(h, w_ref[:, 0:cw])
    b_gate = _dot(h, w_ref[:, cw:2 * cw])
    x_in = _dot(h, w_ref[:, 2 * cw:3 * cw])
    u = c_gate * x_in
    uext_ref[SUBLANES:, :] = u
    u1 = uext_ref[SUBLANES - 1:SUBLANES - 1 + tm, :]
    u2 = uext_ref[SUBLANES - 2:SUBLANES - 2 + tm, :]
    conv = wc_ref[0:1, :] * u2 + wc_ref[1:2, :] * u1 + wc_ref[2:3, :] * u
    y = b_gate * (conv + bc_ref[...])
    uext_ref[0:SUBLANES, :] = u[tm - SUBLANES:, :]
    for grp in range(cw // HEAD_DIM):
        lo = grp * HEAD_DIM
        yg = _rms_norm(y[:, lo:lo + HEAD_DIM], gc_ref[:, lo:lo + HEAD_DIM])
        o_ref[:, lo:lo + HEAD_DIM] = yg.astype(BF16)


def _conv_branch(x2d, g, w_in_bf, w_conv, b_conv, g_conv, seq_len):
    n_tok, d_model = x2d.shape
    cw = w_conv.shape[1]
    tm = ROW_TILE
    const = lambda i: (0, 0)
    return pl.pallas_call(
        functools.partial(_conv_kernel, seq_len // tm),
        grid=(n_tok // tm,),
        in_specs=[
            pl.BlockSpec((tm, d_model), lambda i: (i, 0)),
            pl.BlockSpec((1, d_model), const),
            pl.BlockSpec((d_model, 3 * cw), lambda i: (0, 1)),
            pl.BlockSpec((CONV_KSIZE, cw), const),
            pl.BlockSpec((1, cw), const),
            pl.BlockSpec((1, cw), const),
        ],
        out_specs=pl.BlockSpec((tm, cw), lambda i: (i, 0)),
        out_shape=jax.ShapeDtypeStruct((n_tok, cw), BF16),
        scratch_shapes=[pltpu.VMEM((tm + SUBLANES, cw), F32)],
        compiler_params=_params("arbitrary"),
        name="conv_branch",
    )(x2d, g, w_in_bf, w_conv, b_conv, g_conv)


def _moba_kernel(q_ref, k_ref, v_ref, g_ref, o_ref, kmean_ref, bias_ref):
    qi = pl.program_id(2)
    n_blk = k_ref.shape[0] // MOBA_BLOCK
    blk = MOBA_BLOCK

    @pl.when(qi == 0)
    def _():
        kf = k_ref[...].astype(F32).reshape(n_blk, blk, HEAD_DIM)
        kmean_ref[...] = jnp.sum(kf, axis=1) * (1.0 / blk)

    q = q_ref[...]

    gate = lax.dot_general(q.astype(F32), kmean_ref[...], (((1,), (1,)), ((), ())),
                           precision=lax.Precision.HIGHEST, preferred_element_type=F32)
    col = lax.broadcasted_iota(jnp.int32, (blk, n_blk), 1)
    rank = jnp.zeros((blk, n_blk), jnp.int32)
    for j in range(n_blk):
        gj = gate[:, j:j + 1]
        beats = (gj > gate) | ((gj == gate) & (j < col))
        rank = rank + jnp.where(beats & (j < qi), 1, 0)
    sel = (col < qi) & (rank < MOBA_TOPK)
    bias = jnp.where(sel, 0.0, MASK_VALUE).astype(F32)
    for j in range(n_blk):
        bias_ref[j] = jnp.broadcast_to(bias[:, j:j + 1], (blk, HEAD_DIM))

    own = pl.multiple_of(qi * blk, blk)
    s = _dot_nt(q, k_ref[pl.ds(own, blk), :])
    r_id = lax.broadcasted_iota(jnp.int32, (blk, blk), 0)
    c_id = lax.broadcasted_iota(jnp.int32, (blk, blk), 1)
    s = jnp.where(c_id <= r_id, s, MASK_VALUE)
    m0 = jnp.max(s, axis=-1, keepdims=True)
    p = jnp.exp(s - m0)
    l0 = jnp.sum(p, axis=-1, keepdims=True)
    acc0 = _dot(p.astype(BF16), v_ref[pl.ds(own, blk), :])

    def past_block(j, carry):
        m, l, acc = carry
        start = pl.multiple_of(j * blk, blk)
        b = bias_ref[j]
        s = _dot_nt(q, k_ref[pl.ds(start, blk), :]) + jnp.concatenate([b] * (blk // HEAD_DIM), axis=1)
        m_new = jnp.maximum(m, jnp.max(s, axis=-1, keepdims=True))
        alpha = jnp.exp(m - m_new)
        p = jnp.exp(s - m_new)
        l = alpha * l + jnp.sum(p, axis=-1, keepdims=True)
        acc = alpha * acc + _dot(p.astype(BF16), v_ref[pl.ds(start, blk), :])
        return m_new, l, acc

    _, l, acc = lax.fori_loop(0, qi, past_block, (m0, l0, acc0))
    o = acc / l
    o_ref[...] = _rms_norm(o, g_ref[...]).astype(BF16)


def _moba_attention(qkv, g_attn, bsz, seq_len):
    n_tok = qkv.shape[0]
    blk = MOBA_BLOCK
    n_q = seq_len // blk
    nh = N_ATTN_HEADS
    return pl.pallas_call(
        _moba_kernel,
        grid=(bsz, nh, n_q),
        in_specs=[
            pl.BlockSpec((blk, HEAD_DIM), lambda b, h, i: (b * n_q + i, h)),
            pl.BlockSpec((seq_len, HEAD_DIM), lambda b, h, i: (b, nh + h)),
            pl.BlockSpec((seq_len, HEAD_DIM), lambda b, h, i: (b, 2 * nh + h)),
            pl.BlockSpec((1, HEAD_DIM), lambda b, h, i: (0, h)),
        ],
        out_specs=pl.BlockSpec((blk, HEAD_DIM), lambda b, h, i: (b * n_q + i, h)),
        out_shape=jax.ShapeDtypeStruct((n_tok, nh * HEAD_DIM), BF16),
        scratch_shapes=[
            pltpu.VMEM((n_q, HEAD_DIM), F32),
            pltpu.VMEM((n_q, blk, HEAD_DIM), F32),
        ],
        compiler_params=_params("parallel", "parallel", "arbitrary"),
        name="moba_attention",
    )(qkv, qkv, qkv, g_attn)


def _out_proj_kernel(a_ref, c_ref, w_ref, x_ref, o_ref):
    aw = a_ref.shape[1]
    y = _dot(a_ref[...], w_ref[0:aw, :]) + _dot(c_ref[...], w_ref[aw:, :])
    o_ref[...] = x_ref[...] + y


def _out_proj(o_attn, y_conv, w_out_bf, x2d):
    n_tok, d_model = x2d.shape
    tm = ROW_TILE
    return pl.pallas_call(
        _out_proj_kernel,
        grid=(n_tok // tm,),
        in_specs=[
            pl.BlockSpec((tm, o_attn.shape[1]), lambda i: (i, 0)),
            pl.BlockSpec((tm, y_conv.shape[1]), lambda i: (i, 0)),
            pl.BlockSpec(w_out_bf.shape, lambda i: (0, 0)),
            pl.BlockSpec((tm, d_model), lambda i: (i, 0)),
        ],
        out_specs=pl.BlockSpec((tm, d_model), lambda i: (i, 0)),
        out_shape=jax.ShapeDtypeStruct((n_tok, d_model), F32),
        compiler_params=_params("parallel"),
        name="out_proj",
    )(o_attn, y_conv, w_out_bf, x2d)


def _mem_kv_kernel(m_ref, g_ref, w_ref, o_ref):
    h = _rms_norm(m_ref[0], g_ref[...]).astype(BF16)
    o_ref[0] = _dot(h, w_ref[...]).astype(BF16)


def _mem_kv(mem, g, w_xkv_bf):
    bsz, n_mem, d_model = mem.shape
    width = w_xkv_bf.shape[1]
    return pl.pallas_call(
        _mem_kv_kernel,
        grid=(bsz,),
        in_specs=[
            pl.BlockSpec((1, n_mem, d_model), lambda b: (b, 0, 0)),
            pl.BlockSpec((1, d_model), lambda b: (0, 0)),
            pl.BlockSpec(w_xkv_bf.shape, lambda b: (0, 0)),
        ],
        out_specs=pl.BlockSpec((1, n_mem, width), lambda b: (b, 0, 0)),
        out_shape=jax.ShapeDtypeStruct((bsz, n_mem, width), BF16),
        compiler_params=_params("parallel"),
        name="mem_kv",
    )(mem, g, w_xkv_bf)


def _xattn_kernel(x_ref, g_ref, wq_ref, kv_ref, wo_ref, o_ref):
    x = x_ref[...]
    h = _rms_norm(x, g_ref[...]).astype(BF16)
    q = (_dot(h, wq_ref[...]) * (XATTN_HEAD_DIM ** -0.5)).astype(BF16)
    y = x
    for hh in range(N_XATTN_HEADS):
        lo = hh * XATTN_HEAD_DIM
        k = kv_ref[0, :, lo:lo + XATTN_HEAD_DIM]
        v = kv_ref[0, :, XATTN_WIDTH + lo:XATTN_WIDTH + lo + XATTN_HEAD_DIM]
        s = _dot_nt(q[:, lo:lo + XATTN_HEAD_DIM], k)
        p = jnp.exp(s - jnp.max(s, axis=-1, keepdims=True))
        o = _dot(p.astype(BF16), v) / jnp.sum(p, axis=-1, keepdims=True)
        y = y + _dot(o.astype(BF16), wo_ref[lo:lo + XATTN_HEAD_DIM, :])
    o_ref[...] = y


def _cross_attention(x2d, g, w_xq_bf, kv, w_xo_bf, seq_len):
    n_tok, d_model = x2d.shape
    tm = ROW_TILE
    tiles_per_seq = seq_len // tm
    return pl.pallas_call(
        _xattn_kernel,
        grid=(n_tok // tm,),
        in_specs=[
            pl.BlockSpec((tm, d_model), lambda i: (i, 0)),
            pl.BlockSpec((1, d_model), lambda i: (0, 0)),
            pl.BlockSpec(w_xq_bf.shape, lambda i: (0, 0)),
            pl.BlockSpec((1,) + kv.shape[1:], lambda i: (i // tiles_per_seq, 0, 0)),
            pl.BlockSpec(w_xo_bf.shape, lambda i: (0, 0)),
        ],
        out_specs=pl.BlockSpec((tm, d_model), lambda i: (i, 0)),
        out_shape=jax.ShapeDtypeStruct((n_tok, d_model), F32),
        compiler_params=_params("parallel"),
        name="cross_attention",
    )(x2d, g, w_xq_bf, kv, w_xo_bf)


def _ffn_kernel(final_norm, x_ref, g_ref, wg_ref, wu_ref, wd_ref, gf_ref, o_ref, h_ref, acc_ref):
    j = pl.program_id(1)

    @pl.when(j == 0)
    def _():
        h_ref[...] = _rms_norm(x_ref[...], g_ref[...]).astype(BF16)
        acc_ref[...] = jnp.zeros_like(acc_ref)

    h = h_ref[...]
    gate = _dot(h, wg_ref[...])
    up = _dot(h, wu_ref[...])
    act = (gate * jax.nn.sigmoid(gate) * up).astype(BF16)
    acc_ref[...] += _dot(act, wd_ref[...])

    @pl.when(j == pl.num_programs(1) - 1)
    def _():
        y = x_ref[...] + acc_ref[...]
        o_ref[...] = _rms_norm(y, gf_ref[...]) if final_norm else y


def _ffn(x2d, g, w_gate_bf, w_up_bf, w_down_bf, g_final, final_norm):
    n_tok, d_model = x2d.shape
    d_ff = w_gate_bf.shape[1]
    tm, tf = ROW_TILE, FF_TILE
    return pl.pallas_call(
        functools.partial(_ffn_kernel, final_norm),
        grid=(n_tok // tm, d_ff // tf),
        in_specs=[
            pl.BlockSpec((tm, d_model), lambda i, j: (i, 0)),
            pl.BlockSpec((1, d_model), lambda i, j: (0, 0)),
            pl.BlockSpec((d_model, tf), lambda i, j: (0, j)),
            pl.BlockSpec((d_model, tf), lambda i, j: (0, j)),
            pl.BlockSpec((tf, d_model), lambda i, j: (j, 0)),
            pl.BlockSpec((1, d_model), lambda i, j: (0, 0)),
        ],
        out_specs=pl.BlockSpec((tm, d_model), lambda i, j: (i, 0)),
        out_shape=jax.ShapeDtypeStruct((n_tok, d_model), F32),
        scratch_shapes=[pltpu.VMEM((tm, d_model), BF16), pltpu.VMEM((tm, d_model), F32)],
        compiler_params=_params("parallel", "arbitrary"),
        name="ffn",
    )(x2d, g, w_gate_bf, w_up_bf, w_down_bf, g_final)


def kernel(x, mem, positions, g_mix, w_in, w_conv, b_conv, g_attn_out, g_conv_out, w_out, g_xattn, g_mem,
           w_xq, w_xkv, w_xo, g_ffn, w_gate, w_up, w_down, g_final):
    bsz, seq_len, d_model = x.shape
    depth = g_mix.shape[0]
    row = lambda a: a.reshape(1, -1)
    cos_t, sin_t = _rope_tables(positions)
    x2d = x.reshape(bsz * seq_len, d_model)
    for l in range(depth):
        w_in_bf = w_in[l].astype(BF16)
        qkv = _qkv_proj(x2d, row(g_mix[l]), w_in_bf, cos_t, sin_t)
        o_attn = _moba_attention(qkv, row(g_attn_out[l]), bsz, seq_len)
        y_conv = _conv_branch(x2d, row(g_mix[l]), w_in_bf, w_conv[l], row(b_conv[l]), row(g_conv_out[l]), seq_len)
        x2d = _out_proj(o_attn, y_conv, w_out[l].astype(BF16), x2d)
        kv = _mem_kv(mem, row(g_mem[l]), w_xkv[l].astype(BF16))
        x2d = _cross_attention(x2d, row(g_xattn[l]), w_xq[l].astype(BF16), kv, w_xo[l].astype(BF16), seq_len)
        x2d = _ffn(x2d, row(g_ffn[l]), w_gate[l].astype(BF16), w_up[l].astype(BF16), w_down[l].astype(BF16),
                   row(g_final), final_norm=l == depth - 1)
    return x2d.reshape(bsz, seq_len, d_model)
```

```python
import functools

import jax
import jax.numpy as jnp
from jax import lax
from jax.experimental import pallas as pl
from jax.experimental.pallas import tpu as pltpu

HEAD_DIM = 128
N_ATTN_HEADS = 8
ATTN_WIDTH = N_ATTN_HEADS * HEAD_DIM
CONV_KSIZE = 3
MOBA_BLOCK = 256
MOBA_TOPK = 3
ROPE_THETA = 10000.0
N_XATTN_HEADS = 4
XATTN_HEAD_DIM = 128
XATTN_WIDTH = N_XATTN_HEADS * XATTN_HEAD_DIM
EPS = 1e-6

LOG2_E = 1.4426950408889634
MASK_VALUE = -1e30
SUBLANES = 8
VMEM_LIMIT = 56 * 1024 * 1024

ROW_TILE = 512
FF_TILE = 512
MOBA_QUERY_PIECE = 1024

BF16 = jnp.bfloat16
F32 = jnp.float32


def _rms_norm(xf, g):
    y = xf * lax.rsqrt(jnp.mean(xf * xf, axis=-1, keepdims=True) + EPS)
    return y * g


def _dot(a, b):
    return jnp.dot(a, b, preferred_element_type=F32)


def _dot_nt(a, b):
    return lax.dot_general(a, b, (((1,), (1,)), ((), ())), preferred_element_type=F32)


def _dot_tn(a, b):
    return lax.dot_general(a, b, (((0,), (0,)), ((), ())), preferred_element_type=F32)


def _params(*semantics):
    return pltpu.CompilerParams(dimension_semantics=semantics, vmem_limit_bytes=VMEM_LIMIT)


def _rope_table_kernel(pos_ref, inv_ref, sign_ref, cos_ref, sin_ref):
    ang = pos_ref[...] * inv_ref[...]
    cos_ref[...] = jnp.cos(ang)
    sin_ref[...] = jnp.sin(ang) * sign_ref[...]


def _rope_tables(positions):
    n_tok = positions.size
    half = HEAD_DIM // 2
    inv_freq = ROPE_THETA ** (-jnp.arange(0, HEAD_DIM, 2, dtype=F32) / HEAD_DIM)
    inv_full = jnp.concatenate([inv_freq, inv_freq]).reshape(1, HEAD_DIM)
    sign = jnp.concatenate([-jnp.ones((half,), F32), jnp.ones((half,), F32)]).reshape(1, HEAD_DIM)
    pos = positions.astype(F32).reshape(n_tok, 1)
    tm = 2048
    row = pl.BlockSpec((tm, HEAD_DIM), lambda i: (i, 0))
    const = pl.BlockSpec((1, HEAD_DIM), lambda i: (0, 0))
    return pl.pallas_call(
        _rope_table_kernel,
        grid=(n_tok // tm,),
        in_specs=[pl.BlockSpec((tm, 1), lambda i: (i, 0)), const, const],
        out_specs=[row, row],
        out_shape=[jax.ShapeDtypeStruct((n_tok, HEAD_DIM), F32)] * 2,
        compiler_params=_params("parallel"),
        name="rope_tables",
    )(pos, inv_full, sign)


def _qkv_kernel(x_ref, g_ref, w_ref, cos_ref, sin_ref, o_ref):
    h = _rms_norm(x_ref[...], g_ref[...]).astype(BF16)
    cos = cos_ref[...]
    sin = sin_ref[...]
    scale = HEAD_DIM ** -0.5 * LOG2_E
    for part in range(3):
        c0 = part * ATTN_WIDTH
        r = _dot(h, w_ref[:, c0:c0 + ATTN_WIDTH])
        for hh in range(N_ATTN_HEADS):
            lo = hh * HEAD_DIM
            t = r[:, lo:lo + HEAD_DIM]
            if part < 2:
                t = t * cos + pltpu.roll(t, HEAD_DIM // 2, axis=1) * sin
            if part == 0:
                t = t * scale
            o_ref[:, c0 + lo:c0 + lo + HEAD_DIM] = t.astype(BF16)


def _qkv_proj(x2d, g, w_in_bf, cos_t, sin_t):
    n_tok, d_model = x2d.shape
    tm = ROW_TILE
    width = 3 * ATTN_WIDTH
    return pl.pallas_call(
        _qkv_kernel,
        grid=(n_tok // tm,),
        in_specs=[
            pl.BlockSpec((tm, d_model), lambda i: (i, 0)),
            pl.BlockSpec((1, d_model), lambda i: (0, 0)),
            pl.BlockSpec((d_model, width), lambda i: (0, 0)),
            pl.BlockSpec((tm, HEAD_DIM), lambda i: (i, 0)),
            pl.BlockSpec((tm, HEAD_DIM), lambda i: (i, 0)),
        ],
        out_specs=pl.BlockSpec((tm, width), lambda i: (i, 0)),
        out_shape=jax.ShapeDtypeStruct((n_tok, width), BF16),
        compiler_params=_params("parallel"),
        name="qkv_proj",
    )(x2d, g, w_in_bf, cos_t, sin_t)


def _conv_kernel(tiles_per_seq, x_ref, g_ref, w_ref, wc_ref, bc_ref, gc_ref, o_ref, uext_ref):
    tm = x_ref.shape[0]
    cw = o_ref.shape[1]

    @pl.when(pl.program_id(0) % tiles_per_seq == 0)
    def _():
        uext_ref[0:SUBLANES, :] = jnp.zeros((SUBLANES, cw), F32)

    h = _rms_norm(x_ref[...], g_ref[...]).astype(BF16)
    c_gate = _dot(h, w_ref[:, 0:cw])
    b_gate = _dot(h, w_ref[:, cw:2 * cw])
    x_in = _dot(h, w_ref[:, 2 * cw:3 * cw])
    u = c_gate * x_in
    uext_ref[SUBLANES:, :] = u
    u1 = uext_ref[SUBLANES - 1:SUBLANES - 1 + tm, :]
    u2 = uext_ref[SUBLANES - 2:SUBLANES - 2 + tm, :]
    conv = wc_ref[0:1, :] * u2 + wc_ref[1:2, :] * u1 + wc_ref[2:3, :] * u
    y = b_gate * (conv + bc_ref[...])
    uext_ref[0:SUBLANES, :] = u[tm - SUBLANES:, :]
    for grp in range(cw // HEAD_DIM):
        lo = grp * HEAD_DIM
        yg = _rms_norm(y[:, lo:lo + HEAD_DIM], gc_ref[:, lo:lo + HEAD_DIM])
        o_ref[:, lo:lo + HEAD_DIM] = yg.astype(BF16)


def _conv_branch(x2d, g, w_in_bf, w_conv, b_conv, g_conv, seq_len):
    n_tok, d_model = x2d.shape
    cw = w_conv.shape[1]
    tm = ROW_TILE
    const = lambda i: (0, 0)
    return pl.pallas_call(
        functools.partial(_conv_kernel, seq_len // tm),
        grid=(n_tok // tm,),
        in_specs=[
            pl.BlockSpec((tm, d_model), lambda i: (i, 0)),
            pl.BlockSpec((1, d_model), const),
            pl.BlockSpec((d_model, 3 * cw), lambda i: (0, 1)),
            pl.BlockSpec((CONV_KSIZE, cw), const),
            pl.BlockSpec((1, cw), const),
            pl.BlockSpec((1, cw), const),
        ],
        out_specs=pl.BlockSpec((tm, cw), lambda i: (i, 0)),
        out_shape=jax.ShapeDtypeStruct((n_tok, cw), BF16),
        scratch_shapes=[pltpu.VMEM((tm + SUBLANES, cw), F32)],
        compiler_params=_params("arbitrary"),
        name="conv_branch",
    )(x2d, g, w_in_bf, w_conv, b_conv, g_conv)


def _moba_kernel(q_ref, k_ref, v_ref, g_ref, o_ref, bias_ref, m_ref, l_ref, acc_ref):
    blk = MOBA_BLOCK
    seq = k_ref.shape[0]
    n_blk = seq // blk

    kf = k_ref[...].astype(F32).reshape(n_blk, blk, HEAD_DIM)
    kmean = jnp.sum(kf, axis=1) * (1.0 / blk)
    gate = lax.dot_general(kmean, q_ref[...].astype(F32), (((1,), (1,)), ((), ())),
                           precision=lax.Precision.HIGHEST, preferred_element_type=F32)
    blk_id = lax.broadcasted_iota(jnp.int32, (n_blk, seq), 0)
    q_tile = lax.broadcasted_iota(jnp.int32, (n_blk, seq), 1) // blk
    rank = jnp.zeros((n_blk, seq), jnp.int32)
    for j in range(n_blk):
        gj = gate[j:j + 1, :]
        beats = (gj > gate) | ((gj == gate) & (j < blk_id))
        rank = rank + jnp.where(beats & (j < q_tile), 1, 0)
    visible = ((blk_id < q_tile) & (rank < MOBA_TOPK)) | (blk_id == q_tile)
    bias_ref[...] = jnp.where(visible, 0.0, MASK_VALUE).astype(F32)

    key_id = lax.broadcasted_iota(jnp.int32, (blk, blk), 0)
    qry_id = lax.broadcasted_iota(jnp.int32, (blk, blk), 1)
    causal = key_id <= qry_id

    for j in range(n_blk):
        c0 = j * blk
        kj = k_ref[c0:c0 + blk, :]
        vj = v_ref[c0:c0 + blk, :]
        for p0 in range(c0, seq, MOBA_QUERY_PIECE):
            p1 = min(p0 + MOBA_QUERY_PIECE, seq)
            s = _dot_nt(kj, q_ref[p0:p1, :]) + bias_ref[j:j + 1, p0:p1]
            if p0 == c0:
                own = jnp.where(causal, s[:, :blk], MASK_VALUE)
                s = own if p1 - p0 == blk else jnp.concatenate([own, s[:, blk:]], axis=1)
            s_max = jnp.max(s, axis=0, keepdims=True)
            if j == 0:
                m_new = s_max
                p = jnp.exp2(s - m_new)
                l_ref[:, p0:p1] = jnp.sum(p, axis=0, keepdims=True)
                acc_ref[:, p0:p1] = _dot_tn(vj, p.astype(BF16))
            else:
                m_old = m_ref[:, p0:p1]
                m_new = jnp.maximum(m_old, s_max)
                alpha = jnp.exp2(m_old - m_new)
                p = jnp.exp2(s - m_new)
                l_ref[:, p0:p1] = alpha * l_ref[:, p0:p1] + jnp.sum(p, axis=0, keepdims=True)
                acc_ref[:, p0:p1] = alpha * acc_ref[:, p0:p1] + _dot_tn(vj, p.astype(BF16))
            m_ref[:, p0:p1] = m_new
        o = (acc_ref[:, c0:c0 + blk] / l_ref[:, c0:c0 + blk]).T
        o_ref[c0:c0 + blk, :] = _rms_norm(o, g_ref[...]).astype(BF16)


def _moba_attention(qkv, g_attn, bsz, seq_len):
    n_tok = qkv.shape[0]
    nh = N_ATTN_HEADS
    n_blk = seq_len // MOBA_BLOCK
    return pl.pallas_call(
        _moba_kernel,
        grid=(bsz, nh),
        in_specs=[
            pl.BlockSpec((seq_len, HEAD_DIM), lambda b, h: (b, h)),
            pl.BlockSpec((seq_len, HEAD_DIM), lambda b, h: (b, nh + h)),
            pl.BlockSpec((seq_len, HEAD_DIM), lambda b, h: (b, 2 * nh + h)),
            pl.BlockSpec((1, HEAD_DIM), lambda b, h: (0, h)),
        ],
        out_specs=pl.BlockSpec((seq_len, HEAD_DIM), lambda b, h: (b, h)),
        out_shape=jax.ShapeDtypeStruct((n_tok, nh * HEAD_DIM), BF16),
        scratch_shapes=[
            pltpu.VMEM((n_blk, seq_len), F32),
            pltpu.VMEM((1, seq_len), F32),
            pltpu.VMEM((1, seq_len), F32),
            pltpu.VMEM((HEAD_DIM, seq_len), F32),
        ],
        compiler_params=_params("parallel", "parallel"),
        name="moba_attention",
    )(qkv, qkv, qkv, g_attn)


def _out_proj_kernel(a_ref, c_ref, w_ref, x_ref, o_ref):
    aw = a_ref.shape[1]
    y = _dot(a_ref[...], w_ref[0:aw, :]) + _dot(c_ref[...], w_ref[aw:, :])
    o_ref[...] = x_ref[...] + y


def _out_proj(o_attn, y_conv, w_out_bf, x2d):
    n_tok, d_model = x2d.shape
    tm = ROW_TILE
    return pl.pallas_call(
        _out_proj_kernel,
        grid=(n_tok // tm,),
        in_specs=[
            pl.BlockSpec((tm, o_attn.shape[1]), lambda i: (i, 0)),
            pl.BlockSpec((tm, y_conv.shape[1]), lambda i: (i, 0)),
            pl.BlockSpec(w_out_bf.shape, lambda i: (0, 0)),
            pl.BlockSpec((tm, d_model), lambda i: (i, 0)),
        ],
        out_specs=pl.BlockSpec((tm, d_model), lambda i: (i, 0)),
        out_shape=jax.ShapeDtypeStruct((n_tok, d_model), F32),
        compiler_params=_params("parallel"),
        name="out_proj",
    )(o_attn, y_conv, w_out_bf, x2d)


def _mem_kv_kernel(m_ref, g_ref, w_ref, o_ref):
    h = _rms_norm(m_ref[0], g_ref[...]).astype(BF16)
    o_ref[0] = _dot(h, w_ref[...]).astype(BF16)


def _mem_kv(mem, g, w_xkv_bf):
    bsz, n_mem, d_model = mem.shape
    width = w_xkv_bf.shape[1]
    return pl.pallas_call(
        _mem_kv_kernel,
        grid=(bsz,),
        in_specs=[
            pl.BlockSpec((1, n_mem, d_model), lambda b: (b, 0, 0)),
            pl.BlockSpec((1, d_model), lambda b: (0, 0)),
            pl.BlockSpec(w_xkv_bf.shape, lambda b: (0, 0)),
        ],
        out_specs=pl.BlockSpec((1, n_mem, width), lambda b: (b, 0, 0)),
        out_shape=jax.ShapeDtypeStruct((bsz, n_mem, width), BF16),
        compiler_params=_params("parallel"),
        name="mem_kv",
    )(mem, g, w_xkv_bf)


def _xattn_kernel(x_ref, g_ref, wq_ref, kv_ref, wo_ref, o_ref):
    x = x_ref[...]
    h = _rms_norm(x, g_ref[...]).astype(BF16)
    q = (_dot(h, wq_ref[...]) * (XATTN_HEAD_DIM ** -0.5)).astype(BF16)
    y = x
    for hh in range(N_XATTN_HEADS):
        lo = hh * XATTN_HEAD_DIM
        k = kv_ref[0, :, lo:lo + XATTN_HEAD_DIM]
        v = kv_ref[0, :, XATTN_WIDTH + lo:XATTN_WIDTH + lo + XATTN_HEAD_DIM]
        s = _dot_nt(q[:, lo:lo + XATTN_HEAD_DIM], k)
        p = jnp.exp(s - jnp.max(s, axis=-1, keepdims=True))
        o = _dot(p.astype(BF16), v) / jnp.sum(p, axis=-1, keepdims=True)
        y = y + _dot(o.astype(BF16), wo_ref[lo:lo + XATTN_HEAD_DIM, :])
    o_ref[...] = y


def _cross_attention(x2d, g, w_xq_bf, kv, w_xo_bf, seq_len):
    n_tok, d_model = x2d.shape
    tm = ROW_TILE
    tiles_per_seq = seq_len // tm
    return pl.pallas_call(
        _xattn_kernel,
        grid=(n_tok // tm,),
        in_specs=[
            pl.BlockSpec((tm, d_model), lambda i: (i, 0)),
            pl.BlockSpec((1, d_model), lambda i: (0, 0)),
            pl.BlockSpec(w_xq_bf.shape, lambda i: (0, 0)),
            pl.BlockSpec((1,) + kv.shape[1:], lambda i: (i // tiles_per_seq, 0, 0)),
            pl.BlockSpec(w_xo_bf.shape, lambda i: (0, 0)),
        ],
        out_specs=pl.BlockSpec((tm, d_model), lambda i: (i, 0)),
        out_shape=jax.ShapeDtypeStruct((n_tok, d_model), F32),
        compiler_params=_params("parallel"),
        name="cross_attention",
    )(x2d, g, w_xq_bf, kv, w_xo_bf)


def _ffn_kernel(final_norm, x_ref, g_ref, wg_ref, wu_ref, wd_ref, gf_ref, o_ref, h_ref, acc_ref):
    j = pl.program_id(1)

    @pl.when(j == 0)
    def _():
        h_ref[...] = _rms_norm(x_ref[...], g_ref[...]).astype(BF16)
        acc_ref[...] = jnp.zeros_like(acc_ref)

    h = h_ref[...]
    gate = _dot(h, wg_ref[...])
    up = _dot(h, wu_ref[...])
    act = (gate * jax.nn.sigmoid(gate) * up).astype(BF16)
    acc_ref[...] += _dot(act, wd_ref[...])

    @pl.when(j == pl.num_programs(1) - 1)
    def _():
        y = x_ref[...] + acc_ref[...]
        o_ref[...] = _rms_norm(y, gf_ref[...]) if final_norm else y


def _ffn(x2d, g, w_gate_bf, w_up_bf, w_down_bf, g_final, final_norm):
    n_tok, d_model = x2d.shape
    d_ff = w_gate_bf.shape[1]
    tm, tf = ROW_TILE, FF_TILE
    return pl.pallas_call(
        functools.partial(_ffn_kernel, final_norm),
        grid=(n_tok // tm, d_ff // tf),
        in_specs=[
            pl.BlockSpec((tm, d_model), lambda i, j: (i, 0)),
            pl.BlockSpec((1, d_model), lambda i, j: (0, 0)),
            pl.BlockSpec((d_model, tf), lambda i, j: (0, j)),
            pl.BlockSpec((d_model, tf), lambda i, j: (0, j)),
            pl.BlockSpec((tf, d_model), lambda i, j: (j, 0)),
            pl.BlockSpec((1, d_model), lambda i, j: (0, 0)),
        ],
        out_specs=pl.BlockSpec((tm, d_model), lambda i, j: (i, 0)),
        out_shape=jax.ShapeDtypeStruct((n_tok, d_model), F32),
        scratch_shapes=[pltpu.VMEM((tm, d_model), BF16), pltpu.VMEM((tm, d_model), F32)],
        compiler_params=_params("parallel", "arbitrary"),
        name="ffn",
    )(x2d, g, w_gate_bf, w_up_bf, w_down_bf, g_final)


def kernel(x, mem, positions, g_mix, w_in, w_conv, b_conv, g_attn_out, g_conv_out, w_out, g_xattn, g_mem,
           w_xq, w_xkv, w_xo, g_ffn, w_gate, w_up, w_down, g_final):
    bsz, seq_len, d_model = x.shape
    depth = g_mix.shape[0]
    row = lambda a: a.reshape(1, -1)
    cos_t, sin_t = _rope_tables(positions)
    x2d = x.reshape(bsz * seq_len, d_model)
    for l in range(depth):
        w_in_bf = w_in[l].astype(BF16)
        qkv = _qkv_proj(x2d, row(g_mix[l]), w_in_bf, cos_t, sin_t)
        o_attn = _moba_attention(qkv, row(g_attn_out[l]), bsz, seq_len)
        y_conv = _conv_branch(x2d, row(g_mix[l]), w_in_bf, w_conv[l], row(b_conv[l]), row(g_conv_out[l]), seq_len)
        x2d = _out_proj(o_attn, y_conv, w_out[l].astype(BF16), x2d)
        kv = _mem_kv(mem, row(g_mem[l]), w_xkv[l].astype(BF16))
        x2d = _cross_attention(x2d, row(g_xattn[l]), w_xq[l].astype(BF16), kv, w_xo[l].astype(BF16), seq_len)
        x2d = _ffn(x2d, row(g_ffn[l]), w_gate[l].astype(BF16), w_up[l].astype(BF16), w_down[l].astype(BF16),
                   row(g_final), final_norm=l == depth - 1)
    return x2d.reshape(bsz, seq_len, d_model)
```

```python
import functools

import jax
import jax.numpy as jnp
from jax import lax
from jax.experimental import pallas as pl
from jax.experimental.pallas import tpu as pltpu

HEAD_DIM = 128
N_ATTN_HEADS = 8
ATTN_WIDTH = N_ATTN_HEADS * HEAD_DIM
CONV_KSIZE = 3
MOBA_BLOCK = 256
MOBA_TOPK = 3
ROPE_THETA = 10000.0
N_XATTN_HEADS = 4
XATTN_HEAD_DIM = 128
XATTN_WIDTH = N_XATTN_HEADS * XATTN_HEAD_DIM
EPS = 1e-6

LOG2_E = 1.4426950408889634
MASK_VALUE = -1e30
SUBLANES = 8
VMEM_LIMIT = 56 * 1024 * 1024

ROW_TILE = 512
FF_TILE = 512
MOBA_QUERY_PIECE = 4096

BF16 = jnp.bfloat16
F32 = jnp.float32


def _rms_norm(xf, g):
    y = xf * lax.rsqrt(jnp.mean(xf * xf, axis=-1, keepdims=True) + EPS)
    return y * g


def _dot(a, b):
    return jnp.dot(a, b, preferred_element_type=F32)


def _dot_nt(a, b):
    return lax.dot_general(a, b, (((1,), (1,)), ((), ())), preferred_element_type=F32)


def _dot_tn(a, b):
    return lax.dot_general(a, b, (((0,), (0,)), ((), ())), preferred_element_type=F32)


def _params(*semantics):
    return pltpu.CompilerParams(dimension_semantics=semantics, vmem_limit_bytes=VMEM_LIMIT)


def _rope_table_kernel(pos_ref, inv_ref, sign_ref, cos_ref, sin_ref):
    ang = pos_ref[...] * inv_ref[...]
    cos_ref[...] = jnp.cos(ang)
    sin_ref[...] = jnp.sin(ang) * sign_ref[...]


def _rope_tables(positions):
    n_tok = positions.size
    half = HEAD_DIM // 2
    inv_freq = ROPE_THETA ** (-jnp.arange(0, HEAD_DIM, 2, dtype=F32) / HEAD_DIM)
    inv_full = jnp.concatenate([inv_freq, inv_freq]).reshape(1, HEAD_DIM)
    sign = jnp.concatenate([-jnp.ones((half,), F32), jnp.ones((half,), F32)]).reshape(1, HEAD_DIM)
    pos = positions.astype(F32).reshape(n_tok, 1)
    tm = 2048
    row = pl.BlockSpec((tm, HEAD_DIM), lambda i: (i, 0))
    const = pl.BlockSpec((1, HEAD_DIM), lambda i: (0, 0))
    return pl.pallas_call(
        _rope_table_kernel,
        grid=(n_tok // tm,),
        in_specs=[pl.BlockSpec((tm, 1), lambda i: (i, 0)), const, const],
        out_specs=[row, row],
        out_shape=[jax.ShapeDtypeStruct((n_tok, HEAD_DIM), F32)] * 2,
        compiler_params=_params("parallel"),
        name="rope_tables",
    )(pos, inv_full, sign)


def _qkv_kernel(x_ref, g_ref, w_ref, cos_ref, sin_ref, o_ref):
    h = _rms_norm(x_ref[...], g_ref[...]).astype(BF16)
    cos = cos_ref[...]
    sin = sin_ref[...]
    scale = HEAD_DIM ** -0.5 * LOG2_E
    for part in range(3):
        c0 = part * ATTN_WIDTH
        r = _dot(h, w_ref[:, c0:c0 + ATTN_WIDTH])
        for hh in range(N_ATTN_HEADS):
            lo = hh * HEAD_DIM
            t = r[:, lo:lo + HEAD_DIM]
            if part < 2:
                t = t * cos + pltpu.roll(t, HEAD_DIM // 2, axis=1) * sin
            if part == 0:
                t = t * scale
            o_ref[:, c0 + lo:c0 + lo + HEAD_DIM] = t.astype(BF16)


def _qkv_proj(x2d, g, w_in_bf, cos_t, sin_t):
    n_tok, d_model = x2d.shape
    tm = ROW_TILE
    width = 3 * ATTN_WIDTH
    return pl.pallas_call(
        _qkv_kernel,
        grid=(n_tok // tm,),
        in_specs=[
            pl.BlockSpec((tm, d_model), lambda i: (i, 0)),
            pl.BlockSpec((1, d_model), lambda i: (0, 0)),
            pl.BlockSpec((d_model, width), lambda i: (0, 0)),
            pl.BlockSpec((tm, HEAD_DIM), lambda i: (i, 0)),
            pl.BlockSpec((tm, HEAD_DIM), lambda i: (i, 0)),
        ],
        out_specs=pl.BlockSpec((tm, width), lambda i: (i, 0)),
        out_shape=jax.ShapeDtypeStruct((n_tok, width), BF16),
        compiler_params=_params("parallel"),
        name="qkv_proj",
    )(x2d, g, w_in_bf, cos_t, sin_t)


def _conv_kernel(tiles_per_seq, x_ref, g_ref, w_ref, wc_ref, bc_ref, gc_ref, o_ref, uext_ref):
    tm = x_ref.shape[0]
    cw = o_ref.shape[1]

    @pl.when(pl.program_id(0) % tiles_per_seq == 0)
    def _():
        uext_ref[0:SUBLANES, :] = jnp.zeros((SUBLANES, cw), F32)

    h = _rms_norm(x_ref[...], g_ref[...]).astype(BF16)
    c_gate = _dot(h, w_ref[:, 0:cw])
    b_gate = _dot(h, w_ref[:, cw:2 * cw])
    x_in = _dot(h, w_ref[:, 2 * cw:3 * cw])
    u = c_gate * x_in
    uext_ref[SUBLANES:, :] = u
    u1 = uext_ref[SUBLANES - 1:SUBLANES - 1 + tm, :]
    u2 = uext_ref[SUBLANES - 2:SUBLANES - 2 + tm, :]
    conv = wc_ref[0:1, :] * u2 + wc_ref[1:2, :] * u1 + wc_ref[2:3, :] * u
    y = b_gate * (conv + bc_ref[...])
    uext_ref[0:SUBLANES, :] = u[tm - SUBLANES:, :]
    for grp in range(cw // HEAD_DIM):
        lo = grp * HEAD_DIM
        yg = _rms_norm(y[:, lo:lo + HEAD_DIM], gc_ref[:, lo:lo + HEAD_DIM])
        o_ref[:, lo:lo + HEAD_DIM] = yg.astype(BF16)


def _conv_branch(x2d, g, w_in_bf, w_conv, b_conv, g_conv, seq_len):
    n_tok, d_model = x2d.shape
    cw = w_conv.shape[1]
    tm = ROW_TILE
    const = lambda i: (0, 0)
    return pl.pallas_call(
        functools.partial(_conv_kernel, seq_len // tm),
        grid=(n_tok // tm,),
        in_specs=[
            pl.BlockSpec((tm, d_model), lambda i: (i, 0)),
            pl.BlockSpec((1, d_model), const),
            pl.BlockSpec((d_model, 3 * cw), lambda i: (0, 1)),
            pl.BlockSpec((CONV_KSIZE, cw), const),
            pl.BlockSpec((1, cw), const),
            pl.BlockSpec((1, cw), const),
        ],
        out_specs=pl.BlockSpec((tm, cw), lambda i: (i, 0)),
        out_shape=jax.ShapeDtypeStruct((n_tok, cw), BF16),
        scratch_shapes=[pltpu.VMEM((tm + SUBLANES, cw), F32)],
        compiler_params=_params("arbitrary"),
        name="conv_branch",
    )(x2d, g, w_in_bf, w_conv, b_conv, g_conv)


def _moba_kernel(q_ref, k_ref, v_ref, g_ref, o_ref, bias_ref, m_ref, l_ref, acc_ref):
    blk = MOBA_BLOCK
    seq = k_ref.shape[0]
    n_blk = seq // blk

    kf = k_ref[...].astype(F32).reshape(n_blk, blk, HEAD_DIM)
    kmean = jnp.sum(kf, axis=1) * (1.0 / blk)
    gate = lax.dot_general(kmean, q_ref[...].astype(F32), (((1,), (1,)), ((), ())),
                           precision=lax.Precision.HIGHEST, preferred_element_type=F32)
    blk_id = lax.broadcasted_iota(jnp.int32, (n_blk, seq), 0)
    q_tile = lax.broadcasted_iota(jnp.int32, (n_blk, seq), 1) // blk
    rank = jnp.zeros((n_blk, seq), jnp.int32)
    for j in range(n_blk):
        gj = gate[j:j + 1, :]
        beats = (gj > gate) | ((gj == gate) & (j < blk_id))
        rank = rank + jnp.where(beats & (j < q_tile), 1, 0)
    visible = ((blk_id < q_tile) & (rank < MOBA_TOPK)) | (blk_id == q_tile)
    bias_ref[...] = jnp.where(visible, 0.0, MASK_VALUE).astype(F32)

    key_id = lax.broadcasted_iota(jnp.int32, (blk, blk), 0)
    qry_id = lax.broadcasted_iota(jnp.int32, (blk, blk), 1)
    causal = key_id <= qry_id

    for j in range(n_blk):
        c0 = j * blk
        kj = k_ref[c0:c0 + blk, :]
        vj = v_ref[c0:c0 + blk, :]
        for p0 in range(c0, seq, MOBA_QUERY_PIECE):
            p1 = min(p0 + MOBA_QUERY_PIECE, seq)
            s = _dot_nt(kj, q_ref[p0:p1, :]) + bias_ref[j:j + 1, p0:p1]
            if p0 == c0:
                own = jnp.where(causal, s[:, :blk], MASK_VALUE)
                s = own if p1 - p0 == blk else jnp.concatenate([own, s[:, blk:]], axis=1)
            s_max = jnp.max(s, axis=0, keepdims=True)
            if j == 0:
                m_new = s_max
                p = jnp.exp2(s - m_new)
                l_ref[:, p0:p1] = jnp.sum(p, axis=0, keepdims=True)
                acc_ref[:, p0:p1] = _dot_tn(vj, p.astype(BF16))
            else:
                m_old = m_ref[:, p0:p1]
                m_new = jnp.maximum(m_old, s_max)
                alpha = jnp.exp2(m_old - m_new)
                p = jnp.exp2(s - m_new)
                l_ref[:, p0:p1] = alpha * l_ref[:, p0:p1] + jnp.sum(p, axis=0, keepdims=True)
                acc_ref[:, p0:p1] = alpha * acc_ref[:, p0:p1] + _dot_tn(vj, p.astype(BF16))
            m_ref[:, p0:p1] = m_new
        o = (acc_ref[:, c0:c0 + blk] / l_ref[:, c0:c0 + blk]).T
        o_ref[c0:c0 + blk, :] = _rms_norm(o, g_ref[...]).astype(BF16)


def _moba_attention(qkv, g_attn, bsz, seq_len):
    n_tok = qkv.shape[0]
    nh = N_ATTN_HEADS
    n_blk = seq_len // MOBA_BLOCK
    return pl.pallas_call(
        _moba_kernel,
        grid=(bsz, nh),
        in_specs=[
            pl.BlockSpec((seq_len, HEAD_DIM), lambda b, h: (b, h)),
            pl.BlockSpec((seq_len, HEAD_DIM), lambda b, h: (b, nh + h)),
            pl.BlockSpec((seq_len, HEAD_DIM), lambda b, h: (b, 2 * nh + h)),
            pl.BlockSpec((1, HEAD_DIM), lambda b, h: (0, h)),
        ],
        out_specs=pl.BlockSpec((seq_len, HEAD_DIM), lambda b, h: (b, h)),
        out_shape=jax.ShapeDtypeStruct((n_tok, nh * HEAD_DIM), BF16),
        scratch_shapes=[
            pltpu.VMEM((n_blk, seq_len), F32),
            pltpu.VMEM((1, seq_len), F32),
            pltpu.VMEM((1, seq_len), F32),
            pltpu.VMEM((HEAD_DIM, seq_len), F32),
        ],
        compiler_params=_params("parallel", "parallel"),
        name="moba_attention",
    )(qkv, qkv, qkv, g_attn)


def _out_proj_kernel(a_ref, c_ref, w_ref, x_ref, o_ref):
    aw = a_ref.shape[1]
    y = _dot(a_ref[...], w_ref[0:aw, :]) + _dot(c_ref[...], w_ref[aw:, :])
    o_ref[...] = x_ref[...] + y


def _out_proj(o_attn, y_conv, w_out_bf, x2d):
    n_tok, d_model = x2d.shape
    tm = ROW_TILE
    return pl.pallas_call(
        _out_proj_kernel,
        grid=(n_tok // tm,),
        in_specs=[
            pl.BlockSpec((tm, o_attn.shape[1]), lambda i: (i, 0)),
            pl.BlockSpec((tm, y_conv.shape[1]), lambda i: (i, 0)),
            pl.BlockSpec(w_out_bf.shape, lambda i: (0, 0)),
            pl.BlockSpec((tm, d_model), lambda i: (i, 0)),
        ],
        out_specs=pl.BlockSpec((tm, d_model), lambda i: (i, 0)),
        out_shape=jax.ShapeDtypeStruct((n_tok, d_model), F32),
        compiler_params=_params("parallel"),
        name="out_proj",
    )(o_attn, y_conv, w_out_bf, x2d)


def _mem_kv_kernel(m_ref, g_ref, w_ref, o_ref):
    h = _rms_norm(m_ref[0], g_ref[...]).astype(BF16)
    o_ref[0] = _dot(h, w_ref[...]).astype(BF16)


def _mem_kv(mem, g, w_xkv_bf):
    bsz, n_mem, d_model = mem.shape
    width = w_xkv_bf.shape[1]
    return pl.pallas_call(
        _mem_kv_kernel,
        grid=(bsz,),
        in_specs=[
            pl.BlockSpec((1, n_mem, d_model), lambda b: (b, 0, 0)),
            pl.BlockSpec((1, d_model), lambda b: (0, 0)),
            pl.BlockSpec(w_xkv_bf.shape, lambda b: (0, 0)),
        ],
        out_specs=pl.BlockSpec((1, n_mem, width), lambda b: (b, 0, 0)),
        out_shape=jax.ShapeDtypeStruct((bsz, n_mem, width), BF16),
        compiler_params=_params("parallel"),
        name="mem_kv",
    )(mem, g, w_xkv_bf)


def _xattn_kernel(x_ref, g_ref, wq_ref, kv_ref, wo_ref, o_ref):
    x = x_ref[...]
    h = _rms_norm(x, g_ref[...]).astype(BF16)
    q = (_dot(h, wq_ref[...]) * (XATTN_HEAD_DIM ** -0.5)).astype(BF16)
    y = x
    for hh in range(N_XATTN_HEADS):
        lo = hh * XATTN_HEAD_DIM
        k = kv_ref[0, :, lo:lo + XATTN_HEAD_DIM]
        v = kv_ref[0, :, XATTN_WIDTH + lo:XATTN_WIDTH + lo + XATTN_HEAD_DIM]
        s = _dot_nt(q[:, lo:lo + XATTN_HEAD_DIM], k)
        p = jnp.exp(s - jnp.max(s, axis=-1, keepdims=True))
        o = _dot(p.astype(BF16), v) / jnp.sum(p, axis=-1, keepdims=True)
        y = y + _dot(o.astype(BF16), wo_ref[lo:lo + XATTN_HEAD_DIM, :])
    o_ref[...] = y


def _cross_attention(x2d, g, w_xq_bf, kv, w_xo_bf, seq_len):
    n_tok, d_model = x2d.shape
    tm = ROW_TILE
    tiles_per_seq = seq_len // tm
    return pl.pallas_call(
        _xattn_kernel,
        grid=(n_tok // tm,),
        in_specs=[
            pl.BlockSpec((tm, d_model), lambda i: (i, 0)),
            pl.BlockSpec((1, d_model), lambda i: (0, 0)),
            pl.BlockSpec(w_xq_bf.shape, lambda i: (0, 0)),
            pl.BlockSpec((1,) + kv.shape[1:], lambda i: (i // tiles_per_seq, 0, 0)),
            pl.BlockSpec(w_xo_bf.shape, lambda i: (0, 0)),
        ],
        out_specs=pl.BlockSpec((tm, d_model), lambda i: (i, 0)),
        out_shape=jax.ShapeDtypeStruct((n_tok, d_model), F32),
        compiler_params=_params("parallel"),
        name="cross_attention",
    )(x2d, g, w_xq_bf, kv, w_xo_bf)


def _ffn_kernel(final_norm, x_ref, g_ref, wg_ref, wu_ref, wd_ref, gf_ref, o_ref, h_ref, acc_ref):
    j = pl.program_id(1)

    @pl.when(j == 0)
    def _():
        h_ref[...] = _rms_norm(x_ref[...], g_ref[...]).astype(BF16)
        acc_ref[...] = jnp.zeros_like(acc_ref)

    h = h_ref[...]
    gate = _dot(h, wg_ref[...])
    up = _dot(h, wu_ref[...])
    act = (gate * jax.nn.sigmoid(gate) * up).astype(BF16)
    acc_ref[...] += _dot(act, wd_ref[...])

    @pl.when(j == pl.num_programs(1) - 1)
    def _():
        y = x_ref[...] + acc_ref[...]
        o_ref[...] = _rms_norm(y, gf_ref[...]) if final_norm else y


def _ffn(x2d, g, w_gate_bf, w_up_bf, w_down_bf, g_final, final_norm):
    n_tok, d_model = x2d.shape
    d_ff = w_gate_bf.shape[1]
    tm, tf = ROW_TILE, FF_TILE
    return pl.pallas_call(
        functools.partial(_ffn_kernel, final_norm),
        grid=(n_tok // tm, d_ff // tf),
        in_specs=[
            pl.BlockSpec((tm, d_model), lambda i, j: (i, 0)),
            pl.BlockSpec((1, d_model), lambda i, j: (0, 0)),
            pl.BlockSpec((d_model, tf), lambda i, j: (0, j)),
            pl.BlockSpec((d_model, tf), lambda i, j: (0, j)),
            pl.BlockSpec((tf, d_model), lambda i, j: (j, 0)),
            pl.BlockSpec((1, d_model), lambda i, j: (0, 0)),
        ],
        out_specs=pl.BlockSpec((tm, d_model), lambda i, j: (i, 0)),
        out_shape=jax.ShapeDtypeStruct((n_tok, d_model), F32),
        scratch_shapes=[pltpu.VMEM((tm, d_model), BF16), pltpu.VMEM((tm, d_model), F32)],
        compiler_params=_params("parallel", "arbitrary"),
        name="ffn",
    )(x2d, g, w_gate_bf, w_up_bf, w_down_bf, g_final)


def kernel(x, mem, positions, g_mix, w_in, w_conv, b_conv, g_attn_out, g_conv_out, w_out, g_xattn, g_mem,
           w_xq, w_xkv, w_xo, g_ffn, w_gate, w_up, w_down, g_final):
    bsz, seq_len, d_model = x.shape
    depth = g_mix.shape[0]
    row = lambda a: a.reshape(1, -1)
    cos_t, sin_t = _rope_tables(positions)
    x2d = x.reshape(bsz * seq_len, d_model)
    for l in range(depth):
        w_in_bf = w_in[l].astype(BF16)
        qkv = _qkv_proj(x2d, row(g_mix[l]), w_in_bf, cos_t, sin_t)
        o_attn = _moba_attention(qkv, row(g_attn_out[l]), bsz, seq_len)
        y_conv = _conv_branch(x2d, row(g_mix[l]), w_in_bf, w_conv[l], row(b_conv[l]), row(g_conv_out[l]), seq_len)
        x2d = _out_proj(o_attn, y_conv, w_out[l].astype(BF16), x2d)
        kv = _mem_kv(mem, row(g_mem[l]), w_xkv[l].astype(BF16))
        x2d = _cross_attention(x2d, row(g_xattn[l]), w_xq[l].astype(BF16), kv, w_xo[l].astype(BF16), seq_len)
        x2d = _ffn(x2d, row(g_ffn[l]), w_gate[l].astype(BF16), w_up[l].astype(BF16), w_down[l].astype(BF16),
                   row(g_final), final_norm=l == depth - 1)
    return x2d.reshape(bsz, seq_len, d_model)
```

```python
import functools

import jax
import jax.numpy as jnp
from jax import lax
from jax.experimental import pallas as pl
from jax.experimental.pallas import tpu as pltpu

HEAD_DIM = 128
N_ATTN_HEADS = 8
ATTN_WIDTH = N_ATTN_HEADS * HEAD_DIM
CONV_KSIZE = 3
MOBA_BLOCK = 256
MOBA_TOPK = 3
ROPE_THETA = 10000.0
N_XATTN_HEADS = 4
XATTN_HEAD_DIM = 128
XATTN_WIDTH = N_XATTN_HEADS * XATTN_HEAD_DIM
EPS = 1e-6

LOG2_E = 1.4426950408889634
MASK_VALUE = -1e30
SUBLANES = 8
VMEM_LIMIT = 56 * 1024 * 1024

ROW_TILE = 512
FF_TILE = 512
BF16_PACK = 16
MXU_DEPTH = 256

BF16 = jnp.bfloat16
F32 = jnp.float32


def _rms_norm(xf, g):
    y = xf * lax.rsqrt(jnp.mean(xf * xf, axis=-1, keepdims=True) + EPS)
    return y * g


def _dot(a, b):
    return jnp.dot(a, b, preferred_element_type=F32)


def _dot_nt(a, b):
    return lax.dot_general(a, b, (((1,), (1,)), ((), ())), preferred_element_type=F32)


def _dot_tn(a, b):
    return lax.dot_general(a, b, (((0,), (0,)), ((), ())), preferred_element_type=F32)


def _params(*semantics):
    return pltpu.CompilerParams(dimension_semantics=semantics, vmem_limit_bytes=VMEM_LIMIT)


def _rope_table_kernel(pos_ref, inv_ref, sign_ref, cos_ref, sin_ref):
    ang = pos_ref[...] * inv_ref[...]
    cos_ref[...] = jnp.cos(ang)
    sin_ref[...] = jnp.sin(ang) * sign_ref[...]


def _rope_tables(positions):
    n_tok = positions.size
    half = HEAD_DIM // 2
    inv_freq = ROPE_THETA ** (-jnp.arange(0, HEAD_DIM, 2, dtype=F32) / HEAD_DIM)
    inv_full = jnp.concatenate([inv_freq, inv_freq]).reshape(1, HEAD_DIM)
    sign = jnp.concatenate([-jnp.ones((half,), F32), jnp.ones((half,), F32)]).reshape(1, HEAD_DIM)
    pos = positions.astype(F32).reshape(n_tok, 1)
    tm = 2048
    row = pl.BlockSpec((tm, HEAD_DIM), lambda i: (i, 0))
    const = pl.BlockSpec((1, HEAD_DIM), lambda i: (0, 0))
    return pl.pallas_call(
        _rope_table_kernel,
        grid=(n_tok // tm,),
        in_specs=[pl.BlockSpec((tm, 1), lambda i: (i, 0)), const, const],
        out_specs=[row, row],
        out_shape=[jax.ShapeDtypeStruct((n_tok, HEAD_DIM), F32)] * 2,
        compiler_params=_params("parallel"),
        name="rope_tables",
    )(pos, inv_full, sign)


def _qkv_kernel(x_ref, g_ref, w_ref, cos_ref, sin_ref, o_ref):
    h = _rms_norm(x_ref[...], g_ref[...]).astype(BF16)
    cos = cos_ref[...]
    sin = sin_ref[...]
    scale = HEAD_DIM ** -0.5 * LOG2_E
    for part in range(3):
        c0 = part * ATTN_WIDTH
        r = _dot(h, w_ref[:, c0:c0 + ATTN_WIDTH])
        for hh in range(N_ATTN_HEADS):
            lo = hh * HEAD_DIM
            t = r[:, lo:lo + HEAD_DIM]
            if part < 2:
                t = t * cos + pltpu.roll(t, HEAD_DIM // 2, axis=1) * sin
            if part == 0:
                t = t * scale
            o_ref[:, c0 + lo:c0 + lo + HEAD_DIM] = t.astype(BF16)


def _qkv_proj(x2d, g, w_in_bf, cos_t, sin_t):
    n_tok, d_model = x2d.shape
    tm = ROW_TILE
    width = 3 * ATTN_WIDTH
    return pl.pallas_call(
        _qkv_kernel,
        grid=(n_tok // tm,),
        in_specs=[
            pl.BlockSpec((tm, d_model), lambda i: (i, 0)),
            pl.BlockSpec((1, d_model), lambda i: (0, 0)),
            pl.BlockSpec((d_model, width), lambda i: (0, 0)),
            pl.BlockSpec((tm, HEAD_DIM), lambda i: (i, 0)),
            pl.BlockSpec((tm, HEAD_DIM), lambda i: (i, 0)),
        ],
        out_specs=pl.BlockSpec((tm, width), lambda i: (i, 0)),
        out_shape=jax.ShapeDtypeStruct((n_tok, width), BF16),
        compiler_params=_params("parallel"),
        name="qkv_proj",
    )(x2d, g, w_in_bf, cos_t, sin_t)


def _conv_kernel(tiles_per_seq, x_ref, g_ref, w_ref, wc_ref, bc_ref, gc_ref, o_ref, uext_ref):
    tm = x_ref.shape[0]
    cw = o_ref.shape[1]

    @pl.when(pl.program_id(0) % tiles_per_seq == 0)
    def _():
        uext_ref[0:SUBLANES, :] = jnp.zeros((SUBLANES, cw), F32)

    h = _rms_norm(x_ref[...], g_ref[...]).astype(BF16)
    c_gate = _dot(h, w_ref[:, 0:cw])
    b_gate = _dot(h, w_ref[:, cw:2 * cw])
    x_in = _dot(h, w_ref[:, 2 * cw:3 * cw])
    u = c_gate * x_in
    uext_ref[SUBLANES:, :] = u
    u1 = uext_ref[SUBLANES - 1:SUBLANES - 1 + tm, :]
    u2 = uext_ref[SUBLANES - 2:SUBLANES - 2 + tm, :]
    conv = wc_ref[0:1, :] * u2 + wc_ref[1:2, :] * u1 + wc_ref[2:3, :] * u
    y = b_gate * (conv + bc_ref[...])
    uext_ref[0:SUBLANES, :] = u[tm - SUBLANES:, :]
    for grp in range(cw // HEAD_DIM):
        lo = grp * HEAD_DIM
        yg = _rms_norm(y[:, lo:lo + HEAD_DIM], gc_ref[:, lo:lo + HEAD_DIM])
        o_ref[:, lo:lo + HEAD_DIM] = yg.astype(BF16)


def _conv_branch(x2d, g, w_in_bf, w_conv, b_conv, g_conv, seq_len):
    n_tok, d_model = x2d.shape
    cw = w_conv.shape[1]
    tm = ROW_TILE
    const = lambda i: (0, 0)
    return pl.pallas_call(
        functools.partial(_conv_kernel, seq_len // tm),
        grid=(n_tok // tm,),
        in_specs=[
            pl.BlockSpec((tm, d_model), lambda i: (i, 0)),
            pl.BlockSpec((1, d_model), const),
            pl.BlockSpec((d_model, 3 * cw), lambda i: (0, 1)),
            pl.BlockSpec((CONV_KSIZE, cw), const),
            pl.BlockSpec((1, cw), const),
            pl.BlockSpec((1, cw), const),
        ],
        out_specs=pl.BlockSpec((tm, cw), lambda i: (i, 0)),
        out_shape=jax.ShapeDtypeStruct((n_tok, cw), BF16),
        scratch_shapes=[pltpu.VMEM((tm + SUBLANES, cw), F32)],
        compiler_params=_params("arbitrary"),
        name="conv_branch",
    )(x2d, g, w_in_bf, w_conv, b_conv, g_conv)


def _split_bf16(a, parts):
    out = []
    for _ in range(parts):
        piece = a.astype(BF16)
        out.append(piece)
        a = a - piece.astype(F32)
    return out


def _moba_kernel(q_ref, k_ref, v_ref, g_ref, o_ref, qt_ref, vt_ref, m_ref, acc_ref):
    blk = MOBA_BLOCK
    hd = HEAD_DIM
    seq = k_ref.shape[0]
    n_blk = seq // blk

    for c in range(n_blk):
        rows = slice(c * blk, (c + 1) * blk)
        qt_ref[0:hd, rows] = q_ref[rows, :].astype(F32).T.astype(BF16)
        vt_ref[0:hd, rows] = v_ref[rows, :].astype(F32).T.astype(BF16)
    qt_ref[hd + n_blk:, :] = jnp.zeros((qt_ref.shape[0] - hd - n_blk, seq), BF16)
    vt_ref[hd:, :] = jnp.ones((vt_ref.shape[0] - hd, seq), BF16)

    kf = k_ref[...].astype(F32).reshape(n_blk, blk, hd)
    kmean = jnp.sum(kf, axis=1) * (1.0 / blk)
    kmean_parts = _split_bf16(kmean, 3)

    key_id = lax.broadcasted_iota(jnp.int32, (blk, blk), 0)
    qry_id = lax.broadcasted_iota(jnp.int32, (blk, blk), 1)
    causal = key_id <= qry_id
    lane_id = lax.broadcasted_iota(jnp.int32, (blk, hd), 1)

    def mask_own_tile(s):
        own = jnp.where(causal, s[:, :blk], MASK_VALUE)
        return own if s.shape[1] == blk else jnp.concatenate([own, s[:, blk:]], axis=1)

    def finish_tile(c0):
        o = (acc_ref[0:hd, c0:c0 + blk] / acc_ref[hd:hd + 1, c0:c0 + blk]).T
        o_ref[c0:c0 + blk, :] = _rms_norm(o, g_ref[...]).astype(BF16)

    r = _dot(jnp.concatenate([k_ref[0:blk, :]] + kmean_parts, axis=0), qt_ref[0:hd, :])
    gate = (r[blk:blk + n_blk] + r[blk + n_blk:blk + 2 * n_blk]) + r[blk + 2 * n_blk:blk + 3 * n_blk]
    blk_id = lax.broadcasted_iota(jnp.int32, (n_blk, seq), 0)
    q_tile = lax.broadcasted_iota(jnp.int32, (n_blk, seq), 1) // blk
    rank = jnp.zeros((n_blk, seq), jnp.int32)
    for j in range(n_blk):
        gj = gate[j:j + 1, :]
        beats = (gj > gate) | ((gj == gate) & (j < blk_id))
        rank = rank + jnp.where(beats & (j < q_tile), 1, 0)
    visible = ((blk_id < q_tile) & (rank < MOBA_TOPK)) | (blk_id == q_tile)
    bias = jnp.where(visible, 0.0, MASK_VALUE).astype(F32)
    qt_ref[hd:hd + n_blk, :] = bias.astype(BF16)

    s = mask_own_tile(r[0:blk] + bias[0:1, :])
    m_new = jnp.max(s, axis=0, keepdims=True)
    acc_ref[...] = _dot(vt_ref[:, 0:blk], jnp.exp2(s - m_new).astype(BF16))
    m_ref[...] = m_new
    finish_tile(0)

    def scores(j):
        c0 = j * blk
        k_aug = jnp.concatenate([k_ref[c0:c0 + blk, :], jnp.where(lane_id == j, 1.0, 0.0).astype(BF16)], axis=1)
        return mask_own_tile(_dot(k_aug, qt_ref[:, c0:]))

    s_next = scores(1)
    for j in range(1, n_blk):
        c0 = j * blk
        s = s_next
        if j + 1 < n_blk:
            s_next = scores(j + 1)
        m_old = m_ref[:, c0:]
        m_new = jnp.maximum(m_old, jnp.max(s, axis=0, keepdims=True))
        alpha = jnp.exp2(m_old - m_new)
        p = jnp.exp2(s - m_new).astype(BF16)
        acc_ref[:, c0:] = alpha * acc_ref[:, c0:] + _dot(vt_ref[:, c0:c0 + blk], p)
        m_ref[:, c0:] = m_new
        finish_tile(c0)


def _moba_attention(qkv, g_attn, bsz, seq_len):
    n_tok = qkv.shape[0]
    nh = N_ATTN_HEADS
    n_blk = seq_len // MOBA_BLOCK
    return pl.pallas_call(
        _moba_kernel,
        grid=(bsz, nh),
        in_specs=[
            pl.BlockSpec((seq_len, HEAD_DIM), lambda b, h: (b, h)),
            pl.BlockSpec((seq_len, HEAD_DIM), lambda b, h: (b, nh + h)),
            pl.BlockSpec((seq_len, HEAD_DIM), lambda b, h: (b, 2 * nh + h)),
            pl.BlockSpec((1, HEAD_DIM), lambda b, h: (0, h)),
        ],
        out_specs=pl.BlockSpec((seq_len, HEAD_DIM), lambda b, h: (b, h)),
        out_shape=jax.ShapeDtypeStruct((n_tok, nh * HEAD_DIM), BF16),
        scratch_shapes=[
            pltpu.VMEM((MXU_DEPTH, seq_len), BF16),
            pltpu.VMEM((HEAD_DIM + BF16_PACK, seq_len), BF16),
            pltpu.VMEM((1, seq_len), F32),
            pltpu.VMEM((HEAD_DIM + BF16_PACK, seq_len), F32),
        ],
        compiler_params=_params("parallel", "parallel"),
        name="moba_attention",
    )(qkv, qkv, qkv, g_attn)


def _out_proj_kernel(a_ref, c_ref, w_ref, x_ref, o_ref):
    aw = a_ref.shape[1]
    y = _dot(a_ref[...], w_ref[0:aw, :]) + _dot(c_ref[...], w_ref[aw:, :])
    o_ref[...] = x_ref[...] + y


def _out_proj(o_attn, y_conv, w_out_bf, x2d):
    n_tok, d_model = x2d.shape
    tm = ROW_TILE
    return pl.pallas_call(
        _out_proj_kernel,
        grid=(n_tok // tm,),
        in_specs=[
            pl.BlockSpec((tm, o_attn.shape[1]), lambda i: (i, 0)),
            pl.BlockSpec((tm, y_conv.shape[1]), lambda i: (i, 0)),
            pl.BlockSpec(w_out_bf.shape, lambda i: (0, 0)),
            pl.BlockSpec((tm, d_model), lambda i: (i, 0)),
        ],
        out_specs=pl.BlockSpec((tm, d_model), lambda i: (i, 0)),
        out_shape=jax.ShapeDtypeStruct((n_tok, d_model), F32),
        compiler_params=_params("parallel"),
        name="out_proj",
    )(o_attn, y_conv, w_out_bf, x2d)


def _mem_kv_kernel(m_ref, g_ref, w_ref, o_ref):
    h = _rms_norm(m_ref[0], g_ref[...]).astype(BF16)
    o_ref[0] = _dot(h, w_ref[...]).astype(BF16)


def _mem_kv(mem, g, w_xkv_bf):
    bsz, n_mem, d_model = mem.shape
    width = w_xkv_bf.shape[1]
    return pl.pallas_call(
        _mem_kv_kernel,
        grid=(bsz,),
        in_specs=[
            pl.BlockSpec((1, n_mem, d_model), lambda b: (b, 0, 0)),
            pl.BlockSpec((1, d_model), lambda b: (0, 0)),
            pl.BlockSpec(w_xkv_bf.shape, lambda b: (0, 0)),
        ],
        out_specs=pl.BlockSpec((1, n_mem, width), lambda b: (b, 0, 0)),
        out_shape=jax.ShapeDtypeStruct((bsz, n_mem, width), BF16),
        compiler_params=_params("parallel"),
        name="mem_kv",
    )(mem, g, w_xkv_bf)


def _xattn_kernel(x_ref, g_ref, wq_ref, kv_ref, wo_ref, o_ref):
    x = x_ref[...]
    h = _rms_norm(x, g_ref[...]).astype(BF16)
    q = (_dot(h, wq_ref[...]) * (XATTN_HEAD_DIM ** -0.5)).astype(BF16)
    y = x
    for hh in range(N_XATTN_HEADS):
        lo = hh * XATTN_HEAD_DIM
        k = kv_ref[0, :, lo:lo + XATTN_HEAD_DIM]
        v = kv_ref[0, :, XATTN_WIDTH + lo:XATTN_WIDTH + lo + XATTN_HEAD_DIM]
        s = _dot_nt(q[:, lo:lo + XATTN_HEAD_DIM], k)
        p = jnp.exp(s - jnp.max(s, axis=-1, keepdims=True))
        o = _dot(p.astype(BF16), v) / jnp.sum(p, axis=-1, keepdims=True)
        y = y + _dot(o.astype(BF16), wo_ref[lo:lo + XATTN_HEAD_DIM, :])
    o_ref[...] = y


def _cross_attention(x2d, g, w_xq_bf, kv, w_xo_bf, seq_len):
    n_tok, d_model = x2d.shape
    tm = ROW_TILE
    tiles_per_seq = seq_len // tm
    return pl.pallas_call(
        _xattn_kernel,
        grid=(n_tok // tm,),
        in_specs=[
            pl.BlockSpec((tm, d_model), lambda i: (i, 0)),
            pl.BlockSpec((1, d_model), lambda i: (0, 0)),
            pl.BlockSpec(w_xq_bf.shape, lambda i: (0, 0)),
            pl.BlockSpec((1,) + kv.shape[1:], lambda i: (i // tiles_per_seq, 0, 0)),
            pl.BlockSpec(w_xo_bf.shape, lambda i: (0, 0)),
        ],
        out_specs=pl.BlockSpec((tm, d_model), lambda i: (i, 0)),
        out_shape=jax.ShapeDtypeStruct((n_tok, d_model), F32),
        compiler_params=_params("parallel"),
        name="cross_attention",
    )(x2d, g, w_xq_bf, kv, w_xo_bf)


def _ffn_kernel(final_norm, x_ref, g_ref, wg_ref, wu_ref, wd_ref, gf_ref, o_ref, h_ref, acc_ref):
    j = pl.program_id(1)

    @pl.when(j == 0)
    def _():
        h_ref[...] = _rms_norm(x_ref[...], g_ref[...]).astype(BF16)
        acc_ref[...] = jnp.zeros_like(acc_ref)

    h = h_ref[...]
    gate = _dot(h, wg_ref[...])
    up = _dot(h, wu_ref[...])
    act = (gate * jax.nn.sigmoid(gate) * up).astype(BF16)
    acc_ref[...] += _dot(act, wd_ref[...])

    @pl.when(j == pl.num_programs(1) - 1)
    def _():
        y = x_ref[...] + acc_ref[...]
        o_ref[...] = _rms_norm(y, gf_ref[...]) if final_norm else y


def _ffn(x2d, g, w_gate_bf, w_up_bf, w_down_bf, g_final, final_norm):
    n_tok, d_model = x2d.shape
    d_ff = w_gate_bf.shape[1]
    tm, tf = ROW_TILE, FF_TILE
    return pl.pallas_call(
        functools.partial(_ffn_kernel, final_norm),
        grid=(n_tok // tm, d_ff // tf),
        in_specs=[
            pl.BlockSpec((tm, d_model), lambda i, j: (i, 0)),
            pl.BlockSpec((1, d_model), lambda i, j: (0, 0)),
            pl.BlockSpec((d_model, tf), lambda i, j: (0, j)),
            pl.BlockSpec((d_model, tf), lambda i, j: (0, j)),
            pl.BlockSpec((tf, d_model), lambda i, j: (j, 0)),
            pl.BlockSpec((1, d_model), lambda i, j: (0, 0)),
        ],
        out_specs=pl.BlockSpec((tm, d_model), lambda i, j: (i, 0)),
        out_shape=jax.ShapeDtypeStruct((n_tok, d_model), F32),
        scratch_shapes=[pltpu.VMEM((tm, d_model), BF16), pltpu.VMEM((tm, d_model), F32)],
        compiler_params=_params("parallel", "arbitrary"),
        name="ffn",
    )(x2d, g, w_gate_bf, w_up_bf, w_down_bf, g_final)


def kernel(x, mem, positions, g_mix, w_in, w_conv, b_conv, g_attn_out, g_conv_out, w_out, g_xattn, g_mem,
           w_xq, w_xkv, w_xo, g_ffn, w_gate, w_up, w_down, g_final):
    bsz, seq_len, d_model = x.shape
    depth = g_mix.shape[0]
    row = lambda a: a.reshape(1, -1)
    cos_t, sin_t = _rope_tables(positions)
    x2d = x.reshape(bsz * seq_len, d_model)
    for l in range(depth):
        w_in_bf = w_in[l].astype(BF16)
        qkv = _qkv_proj(x2d, row(g_mix[l]), w_in_bf, cos_t, sin_t)
        o_attn = _moba_attention(qkv, row(g_attn_out[l]), bsz, seq_len)
        y_conv = _conv_branch(x2d, row(g_mix[l]), w_in_bf, w_conv[l], row(b_conv[l]), row(g_conv_out[l]), seq_len)
        x2d = _out_proj(o_attn, y_conv, w_out[l].astype(BF16), x2d)
        kv = _mem_kv(mem, row(g_mem[l]), w_xkv[l].astype(BF16))
        x2d = _cross_attention(x2d, row(g_xattn[l]), w_xq[l].astype(BF16), kv, w_xo[l].astype(BF16), seq_len)
        x2d = _ffn(x2d, row(g_ffn[l]), w_gate[l].astype(BF16), w_up[l].astype(BF16), w_down[l].astype(BF16),
                   row(g_final), final_norm=l == depth - 1)
    return x2d.reshape(bsz, seq_len, d_model)
```

```python
import functools

import jax
import jax.numpy as jnp
from jax import lax
from jax.experimental import pallas as pl
from jax.experimental.pallas import tpu as pltpu

HEAD_DIM = 128
N_ATTN_HEADS = 8
ATTN_WIDTH = N_ATTN_HEADS * HEAD_DIM
CONV_KSIZE = 3
MOBA_BLOCK = 256
MOBA_TOPK = 3
ROPE_THETA = 10000.0
N_XATTN_HEADS = 4
XATTN_HEAD_DIM = 128
XATTN_WIDTH = N_XATTN_HEADS * XATTN_HEAD_DIM
EPS = 1e-6

LOG2_E = 1.4426950408889634
MASK_VALUE = -1e30
SUBLANES = 8
VMEM_LIMIT = 56 * 1024 * 1024

ROW_TILE = 512
FF_TILE = 512
FFN_ROW_TILE = 1024
FFN_VMEM_LIMIT = 60 * 1024 * 1024
BF16_PACK = 16
MXU_DEPTH = 256

BF16 = jnp.bfloat16
F32 = jnp.float32


def _rms_norm(xf, g):
    y = xf * lax.rsqrt(jnp.mean(xf * xf, axis=-1, keepdims=True) + EPS)
    return y * g


def _dot(a, b):
    return jnp.dot(a, b, preferred_element_type=F32)


def _dot_nt(a, b):
    return lax.dot_general(a, b, (((1,), (1,)), ((), ())), preferred_element_type=F32)


def _dot_tn(a, b):
    return lax.dot_general(a, b, (((0,), (0,)), ((), ())), preferred_element_type=F32)


def _params(*semantics, vmem_limit=VMEM_LIMIT):
    return pltpu.CompilerParams(dimension_semantics=semantics, vmem_limit_bytes=vmem_limit)


def _rope_table_kernel(pos_ref, inv_ref, sign_ref, cos_ref, sin_ref):
    ang = pos_ref[...] * inv_ref[...]
    cos_ref[...] = jnp.cos(ang)
    sin_ref[...] = jnp.sin(ang) * sign_ref[...]


def _rope_tables(positions):
    n_tok = positions.size
    half = HEAD_DIM // 2
    inv_freq = ROPE_THETA ** (-jnp.arange(0, HEAD_DIM, 2, dtype=F32) / HEAD_DIM)
    inv_full = jnp.concatenate([inv_freq, inv_freq]).reshape(1, HEAD_DIM)
    sign = jnp.concatenate([-jnp.ones((half,), F32), jnp.ones((half,), F32)]).reshape(1, HEAD_DIM)
    pos = positions.astype(F32).reshape(n_tok, 1)
    tm = 2048
    row = pl.BlockSpec((tm, HEAD_DIM), lambda i: (i, 0))
    const = pl.BlockSpec((1, HEAD_DIM), lambda i: (0, 0))
    return pl.pallas_call(
        _rope_table_kernel,
        grid=(n_tok // tm,),
        in_specs=[pl.BlockSpec((tm, 1), lambda i: (i, 0)), const, const],
        out_specs=[row, row],
        out_shape=[jax.ShapeDtypeStruct((n_tok, HEAD_DIM), F32)] * 2,
        compiler_params=_params("parallel"),
        name="rope_tables",
    )(pos, inv_full, sign)


def _qkv_kernel(x_ref, g_ref, w_ref, cos_ref, sin_ref, o_ref):
    h = _rms_norm(x_ref[...], g_ref[...]).astype(BF16)
    cos = cos_ref[...]
    sin = sin_ref[...]
    scale = HEAD_DIM ** -0.5 * LOG2_E
    for part in range(3):
        c0 = part * ATTN_WIDTH
        r = _dot(h, w_ref[:, c0:c0 + ATTN_WIDTH])
        for hh in range(N_ATTN_HEADS):
            lo = hh * HEAD_DIM
            t = r[:, lo:lo + HEAD_DIM]
            if part < 2:
                t = t * cos + pltpu.roll(t, HEAD_DIM // 2, axis=1) * sin
            if part == 0:
                t = t * scale
            o_ref[:, c0 + lo:c0 + lo + HEAD_DIM] = t.astype(BF16)


def _qkv_proj(x2d, g, w_in_bf, cos_t, sin_t):
    n_tok, d_model = x2d.shape
    tm = ROW_TILE
    width = 3 * ATTN_WIDTH
    return pl.pallas_call(
        _qkv_kernel,
        grid=(n_tok // tm,),
        in_specs=[
            pl.BlockSpec((tm, d_model), lambda i: (i, 0)),
            pl.BlockSpec((1, d_model), lambda i: (0, 0)),
            pl.BlockSpec((d_model, width), lambda i: (0, 0)),
            pl.BlockSpec((tm, HEAD_DIM), lambda i: (i, 0)),
            pl.BlockSpec((tm, HEAD_DIM), lambda i: (i, 0)),
        ],
        out_specs=pl.BlockSpec((tm, width), lambda i: (i, 0)),
        out_shape=jax.ShapeDtypeStruct((n_tok, width), BF16),
        compiler_params=_params("parallel"),
        name="qkv_proj",
    )(x2d, g, w_in_bf, cos_t, sin_t)


def _conv_kernel(tiles_per_seq, x_ref, g_ref, w_ref, wc_ref, bc_ref, gc_ref, o_ref, uext_ref):
    tm = x_ref.shape[0]
    cw = o_ref.shape[1]

    @pl.when(pl.program_id(0) % tiles_per_seq == 0)
    def _():
        uext_ref[0:SUBLANES, :] = jnp.zeros((SUBLANES, cw), F32)

    h = _rms_norm(x_ref[...], g_ref[...]).astype(BF16)
    c_gate = _dot(h, w_ref[:, 0:cw])
    b_gate = _dot(h, w_ref[:, cw:2 * cw])
    x_in = _dot(h, w_ref[:, 2 * cw:3 * cw])
    u = c_gate * x_in
    uext_ref[SUBLANES:, :] = u
    u1 = uext_ref[SUBLANES - 1:SUBLANES - 1 + tm, :]
    u2 = uext_ref[SUBLANES - 2:SUBLANES - 2 + tm, :]
    conv = wc_ref[0:1, :] * u2 + wc_ref[1:2, :] * u1 + wc_ref[2:3, :] * u
    y = b_gate * (conv + bc_ref[...])
    uext_ref[0:SUBLANES, :] = u[tm - SUBLANES:, :]
    for grp in range(cw // HEAD_DIM):
        lo = grp * HEAD_DIM
        yg = _rms_norm(y[:, lo:lo + HEAD_DIM], gc_ref[:, lo:lo + HEAD_DIM])
        o_ref[:, lo:lo + HEAD_DIM] = yg.astype(BF16)


def _conv_branch(x2d, g, w_in_bf, w_conv, b_conv, g_conv, seq_len):
    n_tok, d_model = x2d.shape
    cw = w_conv.shape[1]
    tm = ROW_TILE
    const = lambda i: (0, 0)
    return pl.pallas_call(
        functools.partial(_conv_kernel, seq_len // tm),
        grid=(n_tok // tm,),
        in_specs=[
            pl.BlockSpec((tm, d_model), lambda i: (i, 0)),
            pl.BlockSpec((1, d_model), const),
            pl.BlockSpec((d_model, 3 * cw), lambda i: (0, 1)),
            pl.BlockSpec((CONV_KSIZE, cw), const),
            pl.BlockSpec((1, cw), const),
            pl.BlockSpec((1, cw), const),
        ],
        out_specs=pl.BlockSpec((tm, cw), lambda i: (i, 0)),
        out_shape=jax.ShapeDtypeStruct((n_tok, cw), BF16),
        scratch_shapes=[pltpu.VMEM((tm + SUBLANES, cw), F32)],
        compiler_params=_params("arbitrary"),
        name="conv_branch",
    )(x2d, g, w_in_bf, w_conv, b_conv, g_conv)


def _split_bf16(a, parts):
    out = []
    for _ in range(parts):
        piece = a.astype(BF16)
        out.append(piece)
        a = a - piece.astype(F32)
    return out


def _moba_kernel(q_ref, k_ref, v_ref, g_ref, o_ref, qt_ref, vt_ref, m_ref, acc_ref):
    blk = MOBA_BLOCK
    hd = HEAD_DIM
    seq = k_ref.shape[0]
    n_blk = seq // blk

    for c in range(n_blk):
        rows = slice(c * blk, (c + 1) * blk)
        qt_ref[0:hd, rows] = q_ref[rows, :].astype(F32).T.astype(BF16)
        vt_ref[0:hd, rows] = v_ref[rows, :].astype(F32).T.astype(BF16)
    qt_ref[hd + n_blk:, :] = jnp.zeros((qt_ref.shape[0] - hd - n_blk, seq), BF16)
    vt_ref[hd:, :] = jnp.ones((vt_ref.shape[0] - hd, seq), BF16)

    kf = k_ref[...].astype(F32).reshape(n_blk, blk, hd)
    kmean = jnp.sum(kf, axis=1) * (1.0 / blk)
    kmean_parts = _split_bf16(kmean, 3)

    key_id = lax.broadcasted_iota(jnp.int32, (blk, blk), 0)
    qry_id = lax.broadcasted_iota(jnp.int32, (blk, blk), 1)
    causal = key_id <= qry_id
    lane_id = lax.broadcasted_iota(jnp.int32, (blk, hd), 1)

    def mask_own_tile(s):
        own = jnp.where(causal, s[:, :blk], MASK_VALUE)
        return own if s.shape[1] == blk else jnp.concatenate([own, s[:, blk:]], axis=1)

    def finish_tile(c0):
        o = (acc_ref[0:hd, c0:c0 + blk] / acc_ref[hd:hd + 1, c0:c0 + blk]).T
        o_ref[c0:c0 + blk, :] = _rms_norm(o, g_ref[...]).astype(BF16)

    r = _dot(jnp.concatenate([k_ref[0:blk, :]] + kmean_parts, axis=0), qt_ref[0:hd, :])
    gate = (r[blk:blk + n_blk] + r[blk + n_blk:blk + 2 * n_blk]) + r[blk + 2 * n_blk:blk + 3 * n_blk]
    blk_id = lax.broadcasted_iota(jnp.int32, (n_blk, seq), 0)
    q_tile = lax.broadcasted_iota(jnp.int32, (n_blk, seq), 1) // blk
    rank = jnp.zeros((n_blk, seq), jnp.int32)
    for j in range(n_blk):
        gj = gate[j:j + 1, :]
        beats = (gj > gate) | ((gj == gate) & (j < blk_id))
        rank = rank + jnp.where(beats & (j < q_tile), 1, 0)
    visible = ((blk_id < q_tile) & (rank < MOBA_TOPK)) | (blk_id == q_tile)
    bias = jnp.where(visible, 0.0, MASK_VALUE).astype(F32)
    qt_ref[hd:hd + n_blk, :] = bias.astype(BF16)

    s = mask_own_tile(r[0:blk] + bias[0:1, :])
    m_new = jnp.max(s, axis=0, keepdims=True)
    acc_ref[...] = _dot(vt_ref[:, 0:blk], jnp.exp2(s - m_new).astype(BF16))
    m_ref[...] = m_new
    finish_tile(0)

    def scores(j):
        c0 = j * blk
        k_aug = jnp.concatenate([k_ref[c0:c0 + blk, :], jnp.where(lane_id == j, 1.0, 0.0).astype(BF16)], axis=1)
        return mask_own_tile(_dot(k_aug, qt_ref[:, c0:]))

    s_next = scores(1)
    for j in range(1, n_blk):
        c0 = j * blk
        s = s_next
        if j + 1 < n_blk:
            s_next = scores(j + 1)
        m_old = m_ref[:, c0:]
        m_new = jnp.maximum(m_old, jnp.max(s, axis=0, keepdims=True))
        alpha = jnp.exp2(m_old - m_new)
        p = jnp.exp2(s - m_new).astype(BF16)
        acc_ref[:, c0:] = alpha * acc_ref[:, c0:] + _dot(vt_ref[:, c0:c0 + blk], p)
        m_ref[:, c0:] = m_new
        finish_tile(c0)


def _moba_attention(qkv, g_attn, bsz, seq_len):
    n_tok = qkv.shape[0]
    nh = N_ATTN_HEADS
    n_blk = seq_len // MOBA_BLOCK
    return pl.pallas_call(
        _moba_kernel,
        grid=(bsz, nh),
        in_specs=[
            pl.BlockSpec((seq_len, HEAD_DIM), lambda b, h: (b, h)),
            pl.BlockSpec((seq_len, HEAD_DIM), lambda b, h: (b, nh + h)),
            pl.BlockSpec((seq_len, HEAD_DIM), lambda b, h: (b, 2 * nh + h)),
            pl.BlockSpec((1, HEAD_DIM), lambda b, h: (0, h)),
        ],
        out_specs=pl.BlockSpec((seq_len, HEAD_DIM), lambda b, h: (b, h)),
        out_shape=jax.ShapeDtypeStruct((n_tok, nh * HEAD_DIM), BF16),
        scratch_shapes=[
            pltpu.VMEM((MXU_DEPTH, seq_len), BF16),
            pltpu.VMEM((HEAD_DIM + BF16_PACK, seq_len), BF16),
            pltpu.VMEM((1, seq_len), F32),
            pltpu.VMEM((HEAD_DIM + BF16_PACK, seq_len), F32),
        ],
        compiler_params=_params("parallel", "parallel"),
        name="moba_attention",
    )(qkv, qkv, qkv, g_attn)


def _out_proj_kernel(a_ref, c_ref, w_ref, x_ref, o_ref):
    aw = a_ref.shape[1]
    y = _dot(a_ref[...], w_ref[0:aw, :]) + _dot(c_ref[...], w_ref[aw:, :])
    o_ref[...] = x_ref[...] + y


def _out_proj(o_attn, y_conv, w_out_bf, x2d):
    n_tok, d_model = x2d.shape
    tm = ROW_TILE
    return pl.pallas_call(
        _out_proj_kernel,
        grid=(n_tok // tm,),
        in_specs=[
            pl.BlockSpec((tm, o_attn.shape[1]), lambda i: (i, 0)),
            pl.BlockSpec((tm, y_conv.shape[1]), lambda i: (i, 0)),
            pl.BlockSpec(w_out_bf.shape, lambda i: (0, 0)),
            pl.BlockSpec((tm, d_model), lambda i: (i, 0)),
        ],
        out_specs=pl.BlockSpec((tm, d_model), lambda i: (i, 0)),
        out_shape=jax.ShapeDtypeStruct((n_tok, d_model), F32),
        compiler_params=_params("parallel"),
        name="out_proj",
    )(o_attn, y_conv, w_out_bf, x2d)


def _mem_kv_kernel(m_ref, g_ref, w_ref, o_ref):
    h = _rms_norm(m_ref[0], g_ref[...]).astype(BF16)
    o_ref[0] = _dot(h, w_ref[...]).astype(BF16)


def _mem_kv(mem, g, w_xkv_bf):
    bsz, n_mem, d_model = mem.shape
    width = w_xkv_bf.shape[1]
    return pl.pallas_call(
        _mem_kv_kernel,
        grid=(bsz,),
        in_specs=[
            pl.BlockSpec((1, n_mem, d_model), lambda b: (b, 0, 0)),
            pl.BlockSpec((1, d_model), lambda b: (0, 0)),
            pl.BlockSpec(w_xkv_bf.shape, lambda b: (0, 0)),
        ],
        out_specs=pl.BlockSpec((1, n_mem, width), lambda b: (b, 0, 0)),
        out_shape=jax.ShapeDtypeStruct((bsz, n_mem, width), BF16),
        compiler_params=_params("parallel"),
        name="mem_kv",
    )(mem, g, w_xkv_bf)


def _xattn_kernel(x_ref, g_ref, wq_ref, kv_ref, wo_ref, o_ref):
    x = x_ref[...]
    h = _rms_norm(x, g_ref[...]).astype(BF16)
    q = (_dot(h, wq_ref[...]) * (XATTN_HEAD_DIM ** -0.5)).astype(BF16)
    y = x
    for hh in range(N_XATTN_HEADS):
        lo = hh * XATTN_HEAD_DIM
        k = kv_ref[0, :, lo:lo + XATTN_HEAD_DIM]
        v = kv_ref[0, :, XATTN_WIDTH + lo:XATTN_WIDTH + lo + XATTN_HEAD_DIM]
        s = _dot_nt(q[:, lo:lo + XATTN_HEAD_DIM], k)
        p = jnp.exp(s - jnp.max(s, axis=-1, keepdims=True))
        o = _dot(p.astype(BF16), v) / jnp.sum(p, axis=-1, keepdims=True)
        y = y + _dot(o.astype(BF16), wo_ref[lo:lo + XATTN_HEAD_DIM, :])
    o_ref[...] = y


def _cross_attention(x2d, g, w_xq_bf, kv, w_xo_bf, seq_len):
    n_tok, d_model = x2d.shape
    tm = ROW_TILE
    tiles_per_seq = seq_len // tm
    return pl.pallas_call(
        _xattn_kernel,
        grid=(n_tok // tm,),
        in_specs=[
            pl.BlockSpec((tm, d_model), lambda i: (i, 0)),
            pl.BlockSpec((1, d_model), lambda i: (0, 0)),
            pl.BlockSpec(w_xq_bf.shape, lambda i: (0, 0)),
            pl.BlockSpec((1,) + kv.shape[1:], lambda i: (i // tiles_per_seq, 0, 0)),
            pl.BlockSpec(w_xo_bf.shape, lambda i: (0, 0)),
        ],
        out_specs=pl.BlockSpec((tm, d_model), lambda i: (i, 0)),
        out_shape=jax.ShapeDtypeStruct((n_tok, d_model), F32),
        compiler_params=_params("parallel"),
        name="cross_attention",
    )(x2d, g, w_xq_bf, kv, w_xo_bf)


def _ffn_kernel(final_norm, x_ref, g_ref, wg_ref, wu_ref, wd_ref, gf_ref, o_ref, h_ref):
    j = pl.program_id(1)

    @pl.when(j == 0)
    def _():
        x = x_ref[...]
        h_ref[...] = _rms_norm(x, g_ref[...]).astype(BF16)
        o_ref[...] = x

    h = h_ref[...]
    gate = _dot(h, wg_ref[...])
    up = _dot(h, wu_ref[...])
    act = (gate * jax.nn.sigmoid(gate) * up).astype(BF16)
    o_ref[...] += _dot(act, wd_ref[...])

    if final_norm:
        @pl.when(j == pl.num_programs(1) - 1)
        def _():
            o_ref[...] = _rms_norm(o_ref[...], gf_ref[...])


def _ffn(x2d, g, w_gate_bf, w_up_bf, w_down_bf, g_final, final_norm):
    n_tok, d_model = x2d.shape
    d_ff = w_gate_bf.shape[1]
    tm, tf = FFN_ROW_TILE, FF_TILE
    return pl.pallas_call(
        functools.partial(_ffn_kernel, final_norm),
        grid=(n_tok // tm, d_ff // tf),
        in_specs=[
            pl.BlockSpec((tm, d_model), lambda i, j: (i, 0)),
            pl.BlockSpec((1, d_model), lambda i, j: (0, 0)),
            pl.BlockSpec((d_model, tf), lambda i, j: (0, j)),
            pl.BlockSpec((d_model, tf), lambda i, j: (0, j)),
            pl.BlockSpec((tf, d_model), lambda i, j: (j, 0)),
            pl.BlockSpec((1, d_model), lambda i, j: (0, 0)),
        ],
        out_specs=pl.BlockSpec((tm, d_model), lambda i, j: (i, 0)),
        out_shape=jax.ShapeDtypeStruct((n_tok, d_model), F32),
        scratch_shapes=[pltpu.VMEM((tm, d_model), BF16)],
        compiler_params=_params("parallel", "arbitrary", vmem_limit=FFN_VMEM_LIMIT),
        name="ffn",
    )(x2d, g, w_gate_bf, w_up_bf, w_down_bf, g_final)


def kernel(x, mem, positions, g_mix, w_in, w_conv, b_conv, g_attn_out, g_conv_out, w_out, g_xattn, g_mem,
           w_xq, w_xkv, w_xo, g_ffn, w_gate, w_up, w_down, g_final):
    bsz, seq_len, d_model = x.shape
    depth = g_mix.shape[0]
    row = lambda a: a.reshape(1, -1)
    cos_t, sin_t = _rope_tables(positions)
    x2d = x.reshape(bsz * seq_len, d_model)
    for l in range(depth):
        w_in_bf = w_in[l].astype(BF16)
        qkv = _qkv_proj(x2d, row(g_mix[l]), w_in_bf, cos_t, sin_t)
        o_attn = _moba_attention(qkv, row(g_attn_out[l]), bsz, seq_len)
        y_conv = _conv_branch(x2d, row(g_mix[l]), w_in_bf, w_conv[l], row(b_conv[l]), row(g_conv_out[l]), seq_len)
        x2d = _out_proj(o_attn, y_conv, w_out[l].astype(BF16), x2d)
        kv = _mem_kv(mem, row(g_mem[l]), w_xkv[l].astype(BF16))
        x2d = _cross_attention(x2d, row(g_xattn[l]), w_xq[l].astype(BF16), kv, w_xo[l].astype(BF16), seq_len)
        x2d = _ffn(x2d, row(g_ffn[l]), w_gate[l].astype(BF16), w_up[l].astype(BF16), w_down[l].astype(BF16),
                   row(g_final), final_norm=l == depth - 1)
    return x2d.reshape(bsz, seq_len, d_model)
```

```python
import functools

import jax
import jax.numpy as jnp
from jax import lax
from jax.experimental import pallas as pl
from jax.experimental.pallas import tpu as pltpu

HEAD_DIM = 128
N_ATTN_HEADS = 8
ATTN_WIDTH = N_ATTN_HEADS * HEAD_DIM
CONV_KSIZE = 3
MOBA_BLOCK = 256
MOBA_TOPK = 3
ROPE_THETA = 10000.0
N_XATTN_HEADS = 4
XATTN_HEAD_DIM = 128
XATTN_WIDTH = N_XATTN_HEADS * XATTN_HEAD_DIM
EPS = 1e-6

LOG2_E = 1.4426950408889634
MASK_VALUE = -1e30
SUBLANES = 8
VMEM_LIMIT = 56 * 1024 * 1024

ROW_TILE = 512
FF_TILE = 512
CONV_COL_GROUP = 256
FFN_ROW_TILE = 1024
FFN_VMEM_LIMIT = 60 * 1024 * 1024
BF16_PACK = 16
MXU_DEPTH = 256

BF16 = jnp.bfloat16
F32 = jnp.float32


def _rms_norm(xf, g):
    y = xf * lax.rsqrt(jnp.mean(xf * xf, axis=-1, keepdims=True) + EPS)
    return y * g


def _dot(a, b):
    return jnp.dot(a, b, preferred_element_type=F32)


def _dot_nt(a, b):
    return lax.dot_general(a, b, (((1,), (1,)), ((), ())), preferred_element_type=F32)


def _dot_tn(a, b):
    return lax.dot_general(a, b, (((0,), (0,)), ((), ())), preferred_element_type=F32)


def _params(*semantics, vmem_limit=VMEM_LIMIT):
    return pltpu.CompilerParams(dimension_semantics=semantics, vmem_limit_bytes=vmem_limit)


def _rope_table_kernel(pos_ref, inv_ref, sign_ref, cos_ref, sin_ref):
    ang = pos_ref[...] * inv_ref[...]
    cos_ref[...] = jnp.cos(ang)
    sin_ref[...] = jnp.sin(ang) * sign_ref[...]


def _rope_tables(positions):
    n_tok = positions.size
    half = HEAD_DIM // 2
    inv_freq = ROPE_THETA ** (-jnp.arange(0, HEAD_DIM, 2, dtype=F32) / HEAD_DIM)
    inv_full = jnp.concatenate([inv_freq, inv_freq]).reshape(1, HEAD_DIM)
    sign = jnp.concatenate([-jnp.ones((half,), F32), jnp.ones((half,), F32)]).reshape(1, HEAD_DIM)
    pos = positions.astype(F32).reshape(n_tok, 1)
    tm = 2048
    row = pl.BlockSpec((tm, HEAD_DIM), lambda i: (i, 0))
    const = pl.BlockSpec((1, HEAD_DIM), lambda i: (0, 0))
    return pl.pallas_call(
        _rope_table_kernel,
        grid=(n_tok // tm,),
        in_specs=[pl.BlockSpec((tm, 1), lambda i: (i, 0)), const, const],
        out_specs=[row, row],
        out_shape=[jax.ShapeDtypeStruct((n_tok, HEAD_DIM), F32)] * 2,
        compiler_params=_params("parallel"),
        name="rope_tables",
    )(pos, inv_full, sign)


def _qkv_kernel(x_ref, g_ref, w_ref, cos_ref, sin_ref, o_ref):
    h = _rms_norm(x_ref[...], g_ref[...]).astype(BF16)
    cos = cos_ref[...]
    sin = sin_ref[...]
    scale = HEAD_DIM ** -0.5 * LOG2_E
    for part in range(3):
        c0 = part * ATTN_WIDTH
        r = _dot(h, w_ref[:, c0:c0 + ATTN_WIDTH])
        for hh in range(N_ATTN_HEADS):
            lo = hh * HEAD_DIM
            t = r[:, lo:lo + HEAD_DIM]
            if part < 2:
                t = t * cos + pltpu.roll(t, HEAD_DIM // 2, axis=1) * sin
            if part == 0:
                t = t * scale
            o_ref[:, c0 + lo:c0 + lo + HEAD_DIM] = t.astype(BF16)


def _qkv_proj(x2d, g, w_in_bf, cos_t, sin_t):
    n_tok, d_model = x2d.shape
    tm = ROW_TILE
    width = 3 * ATTN_WIDTH
    return pl.pallas_call(
        _qkv_kernel,
        grid=(n_tok // tm,),
        in_specs=[
            pl.BlockSpec((tm, d_model), lambda i: (i, 0)),
            pl.BlockSpec((1, d_model), lambda i: (0, 0)),
            pl.BlockSpec((d_model, width), lambda i: (0, 0)),
            pl.BlockSpec((tm, HEAD_DIM), lambda i: (i, 0)),
            pl.BlockSpec((tm, HEAD_DIM), lambda i: (i, 0)),
        ],
        out_specs=pl.BlockSpec((tm, width), lambda i: (i, 0)),
        out_shape=jax.ShapeDtypeStruct((n_tok, width), BF16),
        compiler_params=_params("parallel"),
        name="qkv_proj",
    )(x2d, g, w_in_bf, cos_t, sin_t)


def _conv_kernel(tiles_per_seq, x_ref, g_ref, w_ref, wc_ref, bc_ref, gc_ref, o_ref, uext_ref):
    tm = x_ref.shape[0]
    cw = o_ref.shape[1]

    @pl.when(pl.program_id(0) % tiles_per_seq == 0)
    def _():
        uext_ref[0:SUBLANES, :] = jnp.zeros((SUBLANES, cw), F32)

    h = _rms_norm(x_ref[...], g_ref[...]).astype(BF16)
    cg = CONV_COL_GROUP

    def project(c0):
        return tuple(_dot(h, w_ref[:, part * cw + c0:part * cw + c0 + cg]) for part in range(3))

    def finish(c0, c_gate, b_gate, x_in):
        cols = slice(c0, c0 + cg)
        u = c_gate * x_in
        uext_ref[SUBLANES:, cols] = u
        u1 = uext_ref[SUBLANES - 1:SUBLANES - 1 + tm, cols]
        u2 = uext_ref[SUBLANES - 2:SUBLANES - 2 + tm, cols]
        conv = wc_ref[0:1, cols] * u2 + wc_ref[1:2, cols] * u1 + wc_ref[2:3, cols] * u
        y = b_gate * (conv + bc_ref[:, cols])
        uext_ref[0:SUBLANES, cols] = u[tm - SUBLANES:, :]
        for lo in range(0, cg, HEAD_DIM):
            yg = _rms_norm(y[:, lo:lo + HEAD_DIM], gc_ref[:, c0 + lo:c0 + lo + HEAD_DIM])
            o_ref[:, c0 + lo:c0 + lo + HEAD_DIM] = yg.astype(BF16)

    pending = project(0)
    for c0 in range(0, cw, cg):
        current = pending
        if c0 + cg < cw:
            pending = project(c0 + cg)
        finish(c0, *current)


def _conv_branch(x2d, g, w_in_bf, w_conv, b_conv, g_conv, seq_len):
    n_tok, d_model = x2d.shape
    cw = w_conv.shape[1]
    tm = ROW_TILE
    const = lambda i: (0, 0)
    return pl.pallas_call(
        functools.partial(_conv_kernel, seq_len // tm),
        grid=(n_tok // tm,),
        in_specs=[
            pl.BlockSpec((tm, d_model), lambda i: (i, 0)),
            pl.BlockSpec((1, d_model), const),
            pl.BlockSpec((d_model, 3 * cw), lambda i: (0, 1)),
            pl.BlockSpec((CONV_KSIZE, cw), const),
            pl.BlockSpec((1, cw), const),
            pl.BlockSpec((1, cw), const),
        ],
        out_specs=pl.BlockSpec((tm, cw), lambda i: (i, 0)),
        out_shape=jax.ShapeDtypeStruct((n_tok, cw), BF16),
        scratch_shapes=[pltpu.VMEM((tm + SUBLANES, cw), F32)],
        compiler_params=_params("arbitrary"),
        name="conv_branch",
    )(x2d, g, w_in_bf, w_conv, b_conv, g_conv)


def _split_bf16(a, parts):
    out = []
    for _ in range(parts):
        piece = a.astype(BF16)
        out.append(piece)
        a = a - piece.astype(F32)
    return out


def _moba_kernel(q_ref, k_ref, v_ref, g_ref, o_ref, qt_ref, vt_ref, m_ref, acc_ref):
    blk = MOBA_BLOCK
    hd = HEAD_DIM
    seq = k_ref.shape[0]
    n_blk = seq // blk

    for c in range(n_blk):
        rows = slice(c * blk, (c + 1) * blk)
        qt_ref[0:hd, rows] = q_ref[rows, :].astype(F32).T.astype(BF16)
        vt_ref[0:hd, rows] = v_ref[rows, :].astype(F32).T.astype(BF16)
    qt_ref[hd + n_blk:, :] = jnp.zeros((qt_ref.shape[0] - hd - n_blk, seq), BF16)
    vt_ref[hd:, :] = jnp.ones((vt_ref.shape[0] - hd, seq), BF16)

    kf = k_ref[...].astype(F32).reshape(n_blk, blk, hd)
    kmean = jnp.sum(kf, axis=1) * (1.0 / blk)
    kmean_parts = _split_bf16(kmean, 3)

    key_id = lax.broadcasted_iota(jnp.int32, (blk, blk), 0)
    qry_id = lax.broadcasted_iota(jnp.int32, (blk, blk), 1)
    causal = key_id <= qry_id
    lane_id = lax.broadcasted_iota(jnp.int32, (blk, hd), 1)

    def mask_own_tile(s):
        own = jnp.where(causal, s[:, :blk], MASK_VALUE)
        return own if s.shape[1] == blk else jnp.concatenate([own, s[:, blk:]], axis=1)

    def finish_tile(c0):
        o = (acc_ref[0:hd, c0:c0 + blk] / acc_ref[hd:hd + 1, c0:c0 + blk]).T
        o_ref[c0:c0 + blk, :] = _rms_norm(o, g_ref[...]).astype(BF16)

    r = _dot(jnp.concatenate([k_ref[0:blk, :]] + kmean_parts, axis=0), qt_ref[0:hd, :])
    gate = (r[blk:blk + n_blk] + r[blk + n_blk:blk + 2 * n_blk]) + r[blk + 2 * n_blk:blk + 3 * n_blk]
    blk_id = lax.broadcasted_iota(jnp.int32, (n_blk, seq), 0)
    q_tile = lax.broadcasted_iota(jnp.int32, (n_blk, seq), 1) // blk
    rank = jnp.zeros((n_blk, seq), jnp.int32)
    for j in range(n_blk):
        gj = gate[j:j + 1, :]
        beats = (gj > gate) | ((gj == gate) & (j < blk_id))
        rank = rank + jnp.where(beats & (j < q_tile), 1, 0)
    visible = ((blk_id < q_tile) & (rank < MOBA_TOPK)) | (blk_id == q_tile)
    bias = jnp.where(visible, 0.0, MASK_VALUE).astype(F32)
    qt_ref[hd:hd + n_blk, :] = bias.astype(BF16)

    s = mask_own_tile(r[0:blk] + bias[0:1, :])
    m_new = jnp.max(s, axis=0, keepdims=True)
    acc_ref[...] = _dot(vt_ref[:, 0:blk], jnp.exp2(s - m_new).astype(BF16))
    m_ref[...] = m_new
    finish_tile(0)

    def scores(j):
        c0 = j * blk
        k_aug = jnp.concatenate([k_ref[c0:c0 + blk, :], jnp.where(lane_id == j, 1.0, 0.0).astype(BF16)], axis=1)
        return mask_own_tile(_dot(k_aug, qt_ref[:, c0:]))

    s_next = scores(1)
    for j in range(1, n_blk):
        c0 = j * blk
        s = s_next
        if j + 1 < n_blk:
            s_next = scores(j + 1)
        m_old = m_ref[:, c0:]
        m_new = jnp.maximum(m_old, jnp.max(s, axis=0, keepdims=True))
        alpha = jnp.exp2(m_old - m_new)
        p = jnp.exp2(s - m_new).astype(BF16)
        acc_ref[:, c0:] = alpha * acc_ref[:, c0:] + _dot(vt_ref[:, c0:c0 + blk], p)
        m_ref[:, c0:] = m_new
        finish_tile(c0)


def _moba_attention(qkv, g_attn, bsz, seq_len):
    n_tok = qkv.shape[0]
    nh = N_ATTN_HEADS
    n_blk = seq_len // MOBA_BLOCK
    return pl.pallas_call(
        _moba_kernel,
        grid=(bsz, nh),
        in_specs=[
            pl.BlockSpec((seq_len, HEAD_DIM), lambda b, h: (b, h)),
            pl.BlockSpec((seq_len, HEAD_DIM), lambda b, h: (b, nh + h)),
            pl.BlockSpec((seq_len, HEAD_DIM), lambda b, h: (b, 2 * nh + h)),
            pl.BlockSpec((1, HEAD_DIM), lambda b, h: (0, h)),
        ],
        out_specs=pl.BlockSpec((seq_len, HEAD_DIM), lambda b, h: (b, h)),
        out_shape=jax.ShapeDtypeStruct((n_tok, nh * HEAD_DIM), BF16),
        scratch_shapes=[
            pltpu.VMEM((MXU_DEPTH, seq_len), BF16),
            pltpu.VMEM((HEAD_DIM + BF16_PACK, seq_len), BF16),
            pltpu.VMEM((1, seq_len), F32),
            pltpu.VMEM((HEAD_DIM + BF16_PACK, seq_len), F32),
        ],
        compiler_params=_params("parallel", "parallel"),
        name="moba_attention",
    )(qkv, qkv, qkv, g_attn)


def _out_proj_kernel(a_ref, c_ref, w_ref, x_ref, o_ref):
    aw = a_ref.shape[1]
    y = _dot(a_ref[...], w_ref[0:aw, :]) + _dot(c_ref[...], w_ref[aw:, :])
    o_ref[...] = x_ref[...] + y


def _out_proj(o_attn, y_conv, w_out_bf, x2d):
    n_tok, d_model = x2d.shape
    tm = ROW_TILE
    return pl.pallas_call(
        _out_proj_kernel,
        grid=(n_tok // tm,),
        in_specs=[
            pl.BlockSpec((tm, o_attn.shape[1]), lambda i: (i, 0)),
            pl.BlockSpec((tm, y_conv.shape[1]), lambda i: (i, 0)),
            pl.BlockSpec(w_out_bf.shape, lambda i: (0, 0)),
            pl.BlockSpec((tm, d_model), lambda i: (i, 0)),
        ],
        out_specs=pl.BlockSpec((tm, d_model), lambda i: (i, 0)),
        out_shape=jax.ShapeDtypeStruct((n_tok, d_model), F32),
        compiler_params=_params("parallel"),
        name="out_proj",
    )(o_attn, y_conv, w_out_bf, x2d)


def _mem_kv_kernel(m_ref, g_ref, w_ref, o_ref):
    h = _rms_norm(m_ref[0], g_ref[...]).astype(BF16)
    o_ref[0] = _dot(h, w_ref[...]).astype(BF16)


def _mem_kv(mem, g, w_xkv_bf):
    bsz, n_mem, d_model = mem.shape
    width = w_xkv_bf.shape[1]
    return pl.pallas_call(
        _mem_kv_kernel,
        grid=(bsz,),
        in_specs=[
            pl.BlockSpec((1, n_mem, d_model), lambda b: (b, 0, 0)),
            pl.BlockSpec((1, d_model), lambda b: (0, 0)),
            pl.BlockSpec(w_xkv_bf.shape, lambda b: (0, 0)),
        ],
        out_specs=pl.BlockSpec((1, n_mem, width), lambda b: (b, 0, 0)),
        out_shape=jax.ShapeDtypeStruct((bsz, n_mem, width), BF16),
        compiler_params=_params("parallel"),
        name="mem_kv",
    )(mem, g, w_xkv_bf)


def _xattn_kernel(x_ref, g_ref, wq_ref, kv_ref, wo_ref, o_ref):
    x = x_ref[...]
    h = _rms_norm(x, g_ref[...]).astype(BF16)
    q = (_dot(h, wq_ref[...]) * (XATTN_HEAD_DIM ** -0.5)).astype(BF16)
    heads = []
    for hh in range(N_XATTN_HEADS):
        lo = hh * XATTN_HEAD_DIM
        k = kv_ref[0, :, lo:lo + XATTN_HEAD_DIM]
        v = kv_ref[0, :, XATTN_WIDTH + lo:XATTN_WIDTH + lo + XATTN_HEAD_DIM]
        s = _dot_nt(q[:, lo:lo + XATTN_HEAD_DIM], k)
        p = jnp.exp(s - jnp.max(s, axis=-1, keepdims=True))
        o = _dot(p.astype(BF16), v) / jnp.sum(p, axis=-1, keepdims=True)
        heads.append(o.astype(BF16))
    o_ref[...] = x + _dot(jnp.concatenate(heads, axis=1), wo_ref[...])


def _cross_attention(x2d, g, w_xq_bf, kv, w_xo_bf, seq_len):
    n_tok, d_model = x2d.shape
    tm = ROW_TILE
    tiles_per_seq = seq_len // tm
    return pl.pallas_call(
        _xattn_kernel,
        grid=(n_tok // tm,),
        in_specs=[
            pl.BlockSpec((tm, d_model), lambda i: (i, 0)),
            pl.BlockSpec((1, d_model), lambda i: (0, 0)),
            pl.BlockSpec(w_xq_bf.shape, lambda i: (0, 0)),
            pl.BlockSpec((1,) + kv.shape[1:], lambda i: (i // tiles_per_seq, 0, 0)),
            pl.BlockSpec(w_xo_bf.shape, lambda i: (0, 0)),
        ],
        out_specs=pl.BlockSpec((tm, d_model), lambda i: (i, 0)),
        out_shape=jax.ShapeDtypeStruct((n_tok, d_model), F32),
        compiler_params=_params("parallel"),
        name="cross_attention",
    )(x2d, g, w_xq_bf, kv, w_xo_bf)


def _ffn_kernel(final_norm, x_ref, g_ref, wg_ref, wu_ref, wd_ref, gf_ref, o_ref, h_ref):
    j = pl.program_id(1)

    @pl.when(j == 0)
    def _():
        x = x_ref[...]
        h_ref[...] = _rms_norm(x, g_ref[...]).astype(BF16)
        o_ref[...] = x

    h = h_ref[...]
    gate = _dot(h, wg_ref[...])
    up = _dot(h, wu_ref[...])
    act = (gate * jax.nn.sigmoid(gate) * up).astype(BF16)
    o_ref[...] += _dot(act, wd_ref[...])

    if final_norm:
        @pl.when(j == pl.num_programs(1) - 1)
        def _():
            o_ref[...] = _rms_norm(o_ref[...], gf_ref[...])


def _ffn(x2d, g, w_gate_bf, w_up_bf, w_down_bf, g_final, final_norm):
    n_tok, d_model = x2d.shape
    d_ff = w_gate_bf.shape[1]
    tm, tf = FFN_ROW_TILE, FF_TILE
    return pl.pallas_call(
        functools.partial(_ffn_kernel, final_norm),
        grid=(n_tok // tm, d_ff // tf),
        in_specs=[
            pl.BlockSpec((tm, d_model), lambda i, j: (i, 0)),
            pl.BlockSpec((1, d_model), lambda i, j: (0, 0)),
            pl.BlockSpec((d_model, tf), lambda i, j: (0, j)),
            pl.BlockSpec((d_model, tf), lambda i, j: (0, j)),
            pl.BlockSpec((tf, d_model), lambda i, j: (j, 0)),
            pl.BlockSpec((1, d_model), lambda i, j: (0, 0)),
        ],
        out_specs=pl.BlockSpec((tm, d_model), lambda i, j: (i, 0)),
        out_shape=jax.ShapeDtypeStruct((n_tok, d_model), F32),
        scratch_shapes=[pltpu.VMEM((tm, d_model), BF16)],
        compiler_params=_params("parallel", "arbitrary", vmem_limit=FFN_VMEM_LIMIT),
        name="ffn",
    )(x2d, g, w_gate_bf, w_up_bf, w_down_bf, g_final)


def kernel(x, mem, positions, g_mix, w_in, w_conv, b_conv, g_attn_out, g_conv_out, w_out, g_xattn, g_mem,
           w_xq, w_xkv, w_xo, g_ffn, w_gate, w_up, w_down, g_final):
    bsz, seq_len, d_model = x.shape
    depth = g_mix.shape[0]
    row = lambda a: a.reshape(1, -1)
    cos_t, sin_t = _rope_tables(positions)
    x2d = x.reshape(bsz * seq_len, d_model)
    for l in range(depth):
        w_in_bf = w_in[l].astype(BF16)
        qkv = _qkv_proj(x2d, row(g_mix[l]), w_in_bf, cos_t, sin_t)
        o_attn = _moba_attention(qkv, row(g_attn_out[l]), bsz, seq_len)
        y_conv = _conv_branch(x2d, row(g_mix[l]), w_in_bf, w_conv[l], row(b_conv[l]), row(g_conv_out[l]), seq_len)
        x2d = _out_proj(o_attn, y_conv, w_out[l].astype(BF16), x2d)
        kv = _mem_kv(mem, row(g_mem[l]), w_xkv[l].astype(BF16))
        x2d = _cross_attention(x2d, row(g_xattn[l]), w_xq[l].astype(BF16), kv, w_xo[l].astype(BF16), seq_len)
        x2d = _ffn(x2d, row(g_ffn[l]), w_gate[l].astype(BF16), w_up[l].astype(BF16), w_down[l].astype(BF16),
                   row(g_final), final_norm=l == depth - 1)
    return x2d.reshape(bsz, seq_len, d_model)
```

```python
import functools

import jax
import jax.numpy as jnp
from jax import lax
from jax.experimental import pallas as pl
from jax.experimental.pallas import tpu as pltpu

HEAD_DIM = 128
N_ATTN_HEADS = 8
ATTN_WIDTH = N_ATTN_HEADS * HEAD_DIM
CONV_KSIZE = 3
MOBA_BLOCK = 256
MOBA_TOPK = 3
ROPE_THETA = 10000.0
N_XATTN_HEADS = 4
XATTN_HEAD_DIM = 128
XATTN_WIDTH = N_XATTN_HEADS * XATTN_HEAD_DIM
EPS = 1e-6

LOG2_E = 1.4426950408889634
MASK_VALUE = -1e30
SUBLANES = 8
VMEM_LIMIT = 56 * 1024 * 1024

ROW_TILE = 512
FF_TILE = 512
CONV_COL_GROUP = 256
FFN_ROW_TILE = 1024
FFN_VMEM_LIMIT = 60 * 1024 * 1024
BF16_PACK = 16
MXU_DEPTH = 256

BF16 = jnp.bfloat16
F32 = jnp.float32


def _rms_norm(xf, g):
    y = xf * lax.rsqrt(jnp.mean(xf * xf, axis=-1, keepdims=True) + EPS)
    return y * g


def _dot(a, b):
    return jnp.dot(a, b, preferred_element_type=F32)


def _dot_nt(a, b):
    return lax.dot_general(a, b, (((1,), (1,)), ((), ())), preferred_element_type=F32)


def _dot_tn(a, b):
    return lax.dot_general(a, b, (((0,), (0,)), ((), ())), preferred_element_type=F32)


def _params(*semantics, vmem_limit=VMEM_LIMIT):
    return pltpu.CompilerParams(dimension_semantics=semantics, vmem_limit_bytes=vmem_limit)


def _rope_table_kernel(pos_ref, inv_ref, sign_ref, cos_ref, sin_ref):
    ang = pos_ref[...] * inv_ref[...]
    cos_ref[...] = jnp.cos(ang)
    sin_ref[...] = jnp.sin(ang) * sign_ref[...]


def _rope_tables(positions):
    n_tok = positions.size
    half = HEAD_DIM // 2
    inv_freq = ROPE_THETA ** (-jnp.arange(0, HEAD_DIM, 2, dtype=F32) / HEAD_DIM)
    inv_full = jnp.concatenate([inv_freq, inv_freq]).reshape(1, HEAD_DIM)
    sign = jnp.concatenate([-jnp.ones((half,), F32), jnp.ones((half,), F32)]).reshape(1, HEAD_DIM)
    pos = positions.astype(F32).reshape(n_tok, 1)
    tm = 2048
    row = pl.BlockSpec((tm, HEAD_DIM), lambda i: (i, 0))
    const = pl.BlockSpec((1, HEAD_DIM), lambda i: (0, 0))
    return pl.pallas_call(
        _rope_table_kernel,
        grid=(n_tok // tm,),
        in_specs=[pl.BlockSpec((tm, 1), lambda i: (i, 0)), const, const],
        out_specs=[row, row],
        out_shape=[jax.ShapeDtypeStruct((n_tok, HEAD_DIM), F32)] * 2,
        compiler_params=_params("parallel"),
        name="rope_tables",
    )(pos, inv_full, sign)


def _in_proj_kernel(tiles_per_seq, x_ref, g_ref, w_ref, cos_ref, sin_ref, wc_ref, bc_ref, gc_ref,
                    qkv_ref, conv_ref, uext_ref):
    tm = x_ref.shape[0]
    aw = qkv_ref.shape[1] // 3
    cw = conv_ref.shape[1]
    cg = CONV_COL_GROUP

    @pl.when(pl.program_id(0) % tiles_per_seq == 0)
    def _():
        uext_ref[0:SUBLANES, :] = jnp.zeros((SUBLANES, cw), F32)

    h = _rms_norm(x_ref[...], g_ref[...]).astype(BF16)

    def project_conv(c0):
        return tuple(_dot(h, w_ref[:, 3 * aw + part * cw + c0:3 * aw + part * cw + c0 + cg]) for part in range(3))

    def finish_conv(c0, c_gate, b_gate, x_in):
        cols = slice(c0, c0 + cg)
        u = c_gate * x_in
        uext_ref[SUBLANES:, cols] = u
        u1 = uext_ref[SUBLANES - 1:SUBLANES - 1 + tm, cols]
        u2 = uext_ref[SUBLANES - 2:SUBLANES - 2 + tm, cols]
        conv = wc_ref[0:1, cols] * u2 + wc_ref[1:2, cols] * u1 + wc_ref[2:3, cols] * u
        y = b_gate * (conv + bc_ref[:, cols])
        uext_ref[0:SUBLANES, cols] = u[tm - SUBLANES:, :]
        for lo in range(0, cg, HEAD_DIM):
            yg = _rms_norm(y[:, lo:lo + HEAD_DIM], gc_ref[:, c0 + lo:c0 + lo + HEAD_DIM])
            conv_ref[:, c0 + lo:c0 + lo + HEAD_DIM] = yg.astype(BF16)

    def project_attn(part):
        return (_dot(h, w_ref[:, part * aw:(part + 1) * aw]),)

    def finish_attn(part, r):
        scale = HEAD_DIM ** -0.5 * LOG2_E
        for lo in range(0, aw, HEAD_DIM):
            t = r[:, lo:lo + HEAD_DIM]
            if part < 2:
                t = t * cos_ref[...] + pltpu.roll(t, HEAD_DIM // 2, axis=1) * sin_ref[...]
            if part == 0:
                t = t * scale
            qkv_ref[:, part * aw + lo:part * aw + lo + HEAD_DIM] = t.astype(BF16)

    units = [(project_conv, finish_conv, c0) for c0 in range(0, cw, cg)]
    units += [(project_attn, finish_attn, part) for part in (0, 1, 2)]
    pending = units[0][0](units[0][2])
    for n, (_, finish, arg) in enumerate(units):
        current = pending
        if n + 1 < len(units):
            pending = units[n + 1][0](units[n + 1][2])
        finish(arg, *current)


def _in_proj(x2d, g, w_in_bf, cos_t, sin_t, w_conv, b_conv, g_conv, seq_len):
    n_tok, d_model = x2d.shape
    cw = w_conv.shape[1]
    aw = (w_in_bf.shape[1] - 3 * cw) // 3
    tm = ROW_TILE
    const = lambda i: (0, 0)
    row = lambda i: (i, 0)
    return pl.pallas_call(
        functools.partial(_in_proj_kernel, seq_len // tm),
        grid=(n_tok // tm,),
        in_specs=[
            pl.BlockSpec((tm, d_model), row),
            pl.BlockSpec((1, d_model), const),
            pl.BlockSpec(w_in_bf.shape, const, pipeline_mode=pl.Buffered(1)),
            pl.BlockSpec((tm, HEAD_DIM), row),
            pl.BlockSpec((tm, HEAD_DIM), row),
            pl.BlockSpec((CONV_KSIZE, cw), const),
            pl.BlockSpec((1, cw), const),
            pl.BlockSpec((1, cw), const),
        ],
        out_specs=[pl.BlockSpec((tm, 3 * aw), row), pl.BlockSpec((tm, cw), row)],
        out_shape=[jax.ShapeDtypeStruct((n_tok, 3 * aw), BF16), jax.ShapeDtypeStruct((n_tok, cw), BF16)],
        scratch_shapes=[pltpu.VMEM((tm + SUBLANES, cw), F32)],
        compiler_params=_params("arbitrary"),
        name="in_proj",
    )(x2d, g, w_in_bf, cos_t, sin_t, w_conv, b_conv, g_conv)


def _split_bf16(a, parts):
    out = []
    for _ in range(parts):
        piece = a.astype(BF16)
        out.append(piece)
        a = a - piece.astype(F32)
    return out


def _moba_kernel(q_ref, k_ref, v_ref, g_ref, o_ref, qt_ref, vt_ref, m_ref, acc_ref):
    blk = MOBA_BLOCK
    hd = HEAD_DIM
    seq = k_ref.shape[0]
    n_blk = seq // blk

    for c in range(n_blk):
        rows = slice(c * blk, (c + 1) * blk)
        qt_ref[0:hd, rows] = q_ref[rows, :].astype(F32).T.astype(BF16)
        vt_ref[0:hd, rows] = v_ref[rows, :].astype(F32).T.astype(BF16)
    qt_ref[hd + n_blk:, :] = jnp.zeros((qt_ref.shape[0] - hd - n_blk, seq), BF16)
    vt_ref[hd:, :] = jnp.ones((vt_ref.shape[0] - hd, seq), BF16)

    kf = k_ref[...].astype(F32).reshape(n_blk, blk, hd)
    kmean = jnp.sum(kf, axis=1) * (1.0 / blk)
    kmean_parts = _split_bf16(kmean, 3)

    key_id = lax.broadcasted_iota(jnp.int32, (blk, blk), 0)
    qry_id = lax.broadcasted_iota(jnp.int32, (blk, blk), 1)
    causal = key_id <= qry_id
    lane_id = lax.broadcasted_iota(jnp.int32, (blk, hd), 1)

    def mask_own_tile(s):
        own = jnp.where(causal, s[:, :blk], MASK_VALUE)
        return own if s.shape[1] == blk else jnp.concatenate([own, s[:, blk:]], axis=1)

    def finish_tile(c0):
        o = (acc_ref[0:hd, c0:c0 + blk] / acc_ref[hd:hd + 1, c0:c0 + blk]).T
        o_ref[c0:c0 + blk, :] = _rms_norm(o, g_ref[...]).astype(BF16)

    r = _dot(jnp.concatenate([k_ref[0:blk, :]] + kmean_parts, axis=0), qt_ref[0:hd, :])
    gate = (r[blk:blk + n_blk] + r[blk + n_blk:blk + 2 * n_blk]) + r[blk + 2 * n_blk:blk + 3 * n_blk]
    blk_id = lax.broadcasted_iota(jnp.int32, (n_blk, seq), 0)
    q_tile = lax.broadcasted_iota(jnp.int32, (n_blk, seq), 1) // blk
    rank = jnp.zeros((n_blk, seq), jnp.int32)
    for j in range(n_blk):
        gj = gate[j:j + 1, :]
        beats = (gj > gate) | ((gj == gate) & (j < blk_id))
        rank = rank + jnp.where(beats & (j < q_tile), 1, 0)
    visible = ((blk_id < q_tile) & (rank < MOBA_TOPK)) | (blk_id == q_tile)
    bias = jnp.where(visible, 0.0, MASK_VALUE).astype(F32)
    qt_ref[hd:hd + n_blk, :] = bias.astype(BF16)

    s = mask_own_tile(r[0:blk] + bias[0:1, :])
    m_new = jnp.max(s, axis=0, keepdims=True)
    acc_ref[...] = _dot(vt_ref[:, 0:blk], jnp.exp2(s - m_new).astype(BF16))
    m_ref[...] = m_new
    finish_tile(0)

    def scores(j):
        c0 = j * blk
        k_aug = jnp.concatenate([k_ref[c0:c0 + blk, :], jnp.where(lane_id == j, 1.0, 0.0).astype(BF16)], axis=1)
        return mask_own_tile(_dot(k_aug, qt_ref[:, c0:]))

    s_next = scores(1)
    for j in range(1, n_blk):
        c0 = j * blk
        s = s_next
        if j + 1 < n_blk:
            s_next = scores(j + 1)
        m_old = m_ref[:, c0:]
        m_new = jnp.maximum(m_old, jnp.max(s, axis=0, keepdims=True))
        alpha = jnp.exp2(m_old - m_new)
        p = jnp.exp2(s - m_new).astype(BF16)
        acc_ref[:, c0:] = alpha * acc_ref[:, c0:] + _dot(vt_ref[:, c0:c0 + blk], p)
        m_ref[:, c0:] = m_new
        finish_tile(c0)


def _moba_attention(qkv, g_attn, bsz, seq_len):
    n_tok = qkv.shape[0]
    nh = N_ATTN_HEADS
    n_blk = seq_len // MOBA_BLOCK
    return pl.pallas_call(
        _moba_kernel,
        grid=(bsz, nh),
        in_specs=[
            pl.BlockSpec((seq_len, HEAD_DIM), lambda b, h: (b, h)),
            pl.BlockSpec((seq_len, HEAD_DIM), lambda b, h: (b, nh + h)),
            pl.BlockSpec((seq_len, HEAD_DIM), lambda b, h: (b, 2 * nh + h)),
            pl.BlockSpec((1, HEAD_DIM), lambda b, h: (0, h)),
        ],
        out_specs=pl.BlockSpec((seq_len, HEAD_DIM), lambda b, h: (b, h)),
        out_shape=jax.ShapeDtypeStruct((n_tok, nh * HEAD_DIM), BF16),
        scratch_shapes=[
            pltpu.VMEM((MXU_DEPTH, seq_len), BF16),
            pltpu.VMEM((HEAD_DIM + BF16_PACK, seq_len), BF16),
            pltpu.VMEM((1, seq_len), F32),
            pltpu.VMEM((HEAD_DIM + BF16_PACK, seq_len), F32),
        ],
        compiler_params=_params("parallel", "parallel"),
        name="moba_attention",
    )(qkv, qkv, qkv, g_attn)


def _out_proj_kernel(a_ref, c_ref, w_ref, x_ref, o_ref):
    aw = a_ref.shape[1]
    y = _dot(a_ref[...], w_ref[0:aw, :]) + _dot(c_ref[...], w_ref[aw:, :])
    o_ref[...] = x_ref[...] + y


def _out_proj(o_attn, y_conv, w_out_bf, x2d):
    n_tok, d_model = x2d.shape
    tm = ROW_TILE
    return pl.pallas_call(
        _out_proj_kernel,
        grid=(n_tok // tm,),
        in_specs=[
            pl.BlockSpec((tm, o_attn.shape[1]), lambda i: (i, 0)),
            pl.BlockSpec((tm, y_conv.shape[1]), lambda i: (i, 0)),
            pl.BlockSpec(w_out_bf.shape, lambda i: (0, 0)),
            pl.BlockSpec((tm, d_model), lambda i: (i, 0)),
        ],
        out_specs=pl.BlockSpec((tm, d_model), lambda i: (i, 0)),
        out_shape=jax.ShapeDtypeStruct((n_tok, d_model), F32),
        compiler_params=_params("parallel"),
        name="out_proj",
    )(o_attn, y_conv, w_out_bf, x2d)


def _mem_kv_kernel(m_ref, g_ref, w_ref, o_ref):
    h = _rms_norm(m_ref[0], g_ref[...]).astype(BF16)
    o_ref[0] = _dot(h, w_ref[...]).astype(BF16)


def _mem_kv(mem, g, w_xkv_bf):
    bsz, n_mem, d_model = mem.shape
    width = w_xkv_bf.shape[1]
    return pl.pallas_call(
        _mem_kv_kernel,
        grid=(bsz,),
        in_specs=[
            pl.BlockSpec((1, n_mem, d_model), lambda b: (b, 0, 0)),
            pl.BlockSpec((1, d_model), lambda b: (0, 0)),
            pl.BlockSpec(w_xkv_bf.shape, lambda b: (0, 0)),
        ],
        out_specs=pl.BlockSpec((1, n_mem, width), lambda b: (b, 0, 0)),
        out_shape=jax.ShapeDtypeStruct((bsz, n_mem, width), BF16),
        compiler_params=_params("parallel"),
        name="mem_kv",
    )(mem, g, w_xkv_bf)


def _xattn_kernel(x_ref, g_ref, wq_ref, kv_ref, wo_ref, o_ref):
    x = x_ref[...]
    h = _rms_norm(x, g_ref[...]).astype(BF16)
    q = (_dot(h, wq_ref[...]) * (XATTN_HEAD_DIM ** -0.5)).astype(BF16)
    heads = []
    for hh in range(N_XATTN_HEADS):
        lo = hh * XATTN_HEAD_DIM
        k = kv_ref[0, :, lo:lo + XATTN_HEAD_DIM]
        v = kv_ref[0, :, XATTN_WIDTH + lo:XATTN_WIDTH + lo + XATTN_HEAD_DIM]
        s = _dot_nt(q[:, lo:lo + XATTN_HEAD_DIM], k)
        p = jnp.exp(s - jnp.max(s, axis=-1, keepdims=True))
        o = _dot(p.astype(BF16), v) / jnp.sum(p, axis=-1, keepdims=True)
        heads.append(o.astype(BF16))
    o_ref[...] = x + _dot(jnp.concatenate(heads, axis=1), wo_ref[...])


def _cross_attention(x2d, g, w_xq_bf, kv, w_xo_bf, seq_len):
    n_tok, d_model = x2d.shape
    tm = ROW_TILE
    tiles_per_seq = seq_len // tm
    return pl.pallas_call(
        _xattn_kernel,
        grid=(n_tok // tm,),
        in_specs=[
            pl.BlockSpec((tm, d_model), lambda i: (i, 0)),
            pl.BlockSpec((1, d_model), lambda i: (0, 0)),
            pl.BlockSpec(w_xq_bf.shape, lambda i: (0, 0)),
            pl.BlockSpec((1,) + kv.shape[1:], lambda i: (i // tiles_per_seq, 0, 0)),
            pl.BlockSpec(w_xo_bf.shape, lambda i: (0, 0)),
        ],
        out_specs=pl.BlockSpec((tm, d_model), lambda i: (i, 0)),
        out_shape=jax.ShapeDtypeStruct((n_tok, d_model), F32),
        compiler_params=_params("parallel"),
        name="cross_attention",
    )(x2d, g, w_xq_bf, kv, w_xo_bf)


def _ffn_kernel(final_norm, x_ref, g_ref, wg_ref, wu_ref, wd_ref, gf_ref, o_ref, h_ref):
    j = pl.program_id(1)

    @pl.when(j == 0)
    def _():
        x = x_ref[...]
        h_ref[...] = _rms_norm(x, g_ref[...]).astype(BF16)
        o_ref[...] = x

    h = h_ref[...]
    gate = _dot(h, wg_ref[...])
    up = _dot(h, wu_ref[...])
    act = (gate * jax.nn.sigmoid(gate) * up).astype(BF16)
    o_ref[...] += _dot(act, wd_ref[...])

    if final_norm:
        @pl.when(j == pl.num_programs(1) - 1)
        def _():
            o_ref[...] = _rms_norm(o_ref[...], gf_ref[...])


def _ffn(x2d, g, w_gate_bf, w_up_bf, w_down_bf, g_final, final_norm):
    n_tok, d_model = x2d.shape
    d_ff = w_gate_bf.shape[1]
    tm, tf = FFN_ROW_TILE, FF_TILE
    return pl.pallas_call(
        functools.partial(_ffn_kernel, final_norm),
        grid=(n_tok // tm, d_ff // tf),
        in_specs=[
            pl.BlockSpec((tm, d_model), lambda i, j: (i, 0)),
            pl.BlockSpec((1, d_model), lambda i, j: (0, 0)),
            pl.BlockSpec((d_model, tf), lambda i, j: (0, j)),
            pl.BlockSpec((d_model, tf), lambda i, j: (0, j)),
            pl.BlockSpec((tf, d_model), lambda i, j: (j, 0)),
            pl.BlockSpec((1, d_model), lambda i, j: (0, 0)),
        ],
        out_specs=pl.BlockSpec((tm, d_model), lambda i, j: (i, 0)),
        out_shape=jax.ShapeDtypeStruct((n_tok, d_model), F32),
        scratch_shapes=[pltpu.VMEM((tm, d_model), BF16)],
        compiler_params=_params("parallel", "arbitrary", vmem_limit=FFN_VMEM_LIMIT),
        name="ffn",
    )(x2d, g, w_gate_bf, w_up_bf, w_down_bf, g_final)


def kernel(x, mem, positions, g_mix, w_in, w_conv, b_conv, g_attn_out, g_conv_out, w_out, g_xattn, g_mem,
           w_xq, w_xkv, w_xo, g_ffn, w_gate, w_up, w_down, g_final):
    bsz, seq_len, d_model = x.shape
    depth = g_mix.shape[0]
    row = lambda a: a.reshape(1, -1)
    cos_t, sin_t = _rope_tables(positions)
    x2d = x.reshape(bsz * seq_len, d_model)
    for l in range(depth):
        w_in_bf = w_in[l].astype(BF16)
        qkv, y_conv = _in_proj(x2d, row(g_mix[l]), w_in_bf, cos_t, sin_t, w_conv[l], row(b_conv[l]),
                               row(g_conv_out[l]), seq_len)
        o_attn = _moba_attention(qkv, row(g_attn_out[l]), bsz, seq_len)
        x2d = _out_proj(o_attn, y_conv, w_out[l].astype(BF16), x2d)
        kv = _mem_kv(mem, row(g_mem[l]), w_xkv[l].astype(BF16))
        x2d = _cross_attention(x2d, row(g_xattn[l]), w_xq[l].astype(BF16), kv, w_xo[l].astype(BF16), seq_len)
        x2d = _ffn(x2d, row(g_ffn[l]), w_gate[l].astype(BF16), w_up[l].astype(BF16), w_down[l].astype(BF16),
                   row(g_final), final_norm=l == depth - 1)
    return x2d.reshape(bsz, seq_len, d_model)
```

```python
import functools

import jax
import jax.numpy as jnp
from jax import lax
from jax.experimental import pallas as pl
from jax.experimental.pallas import tpu as pltpu

HEAD_DIM = 128
N_ATTN_HEADS = 8
ATTN_WIDTH = N_ATTN_HEADS * HEAD_DIM
CONV_KSIZE = 3
MOBA_BLOCK = 256
MOBA_TOPK = 3
ROPE_THETA = 10000.0
N_XATTN_HEADS = 4
XATTN_HEAD_DIM = 128
XATTN_WIDTH = N_XATTN_HEADS * XATTN_HEAD_DIM
EPS = 1e-6

LOG2_E = 1.4426950408889634
MASK_VALUE = -1e30
SUBLANES = 8
VMEM_LIMIT = 56 * 1024 * 1024

ROW_TILE = 512
FF_TILE = 512
CONV_COL_GROUP = 256
FFN_ROW_TILE = 1024
FFN_VMEM_LIMIT = 60 * 1024 * 1024
BF16_PACK = 16
MXU_DEPTH = 256

BF16 = jnp.bfloat16
F32 = jnp.float32


def _rms_norm(xf, g):
    y = xf * lax.rsqrt(jnp.mean(xf * xf, axis=-1, keepdims=True) + EPS)
    return y * g


def _dot(a, b):
    return jnp.dot(a, b, preferred_element_type=F32)


def _dot_nt(a, b):
    return lax.dot_general(a, b, (((1,), (1,)), ((), ())), preferred_element_type=F32)


def _dot_tn(a, b):
    return lax.dot_general(a, b, (((0,), (0,)), ((), ())), preferred_element_type=F32)


def _params(*semantics, vmem_limit=VMEM_LIMIT):
    return pltpu.CompilerParams(dimension_semantics=semantics, vmem_limit_bytes=vmem_limit)


def _rope_inputs(positions):
    half = HEAD_DIM // 2
    inv_freq = ROPE_THETA ** (-jnp.arange(0, HEAD_DIM, 2, dtype=F32) / HEAD_DIM)
    inv_full = jnp.concatenate([inv_freq, inv_freq]).reshape(1, HEAD_DIM)
    sign = jnp.concatenate([-jnp.ones((half,), F32), jnp.ones((half,), F32)]).reshape(1, HEAD_DIM)
    return positions.astype(F32).reshape(positions.size, 1), inv_full, sign


def _in_proj_kernel(tiles_per_seq, x_ref, g_ref, w_ref, pos_ref, inv_ref, sign_ref, wc_ref, bc_ref, gc_ref,
                    qkv_ref, conv_ref, uext_ref, cos_ref, sin_ref):
    tm = x_ref.shape[0]
    aw = qkv_ref.shape[1] // 3
    cw = conv_ref.shape[1]
    cg = CONV_COL_GROUP

    @pl.when(pl.program_id(0) % tiles_per_seq == 0)
    def _():
        uext_ref[0:SUBLANES, :] = jnp.zeros((SUBLANES, cw), F32)

    h = _rms_norm(x_ref[...], g_ref[...]).astype(BF16)

    def project_conv(c0):
        return tuple(_dot(h, w_ref[:, 3 * aw + part * cw + c0:3 * aw + part * cw + c0 + cg]) for part in range(3))

    def finish_conv(c0, c_gate, b_gate, x_in):
        cols = slice(c0, c0 + cg)
        u = c_gate * x_in
        uext_ref[SUBLANES:, cols] = u
        u1 = uext_ref[SUBLANES - 1:SUBLANES - 1 + tm, cols]
        u2 = uext_ref[SUBLANES - 2:SUBLANES - 2 + tm, cols]
        conv = wc_ref[0:1, cols] * u2 + wc_ref[1:2, cols] * u1 + wc_ref[2:3, cols] * u
        y = b_gate * (conv + bc_ref[:, cols])
        uext_ref[0:SUBLANES, cols] = u[tm - SUBLANES:, :]
        for lo in range(0, cg, HEAD_DIM):
            yg = _rms_norm(y[:, lo:lo + HEAD_DIM], gc_ref[:, c0 + lo:c0 + lo + HEAD_DIM])
            conv_ref[:, c0 + lo:c0 + lo + HEAD_DIM] = yg.astype(BF16)

    def project_attn(part):
        return (_dot(h, w_ref[:, part * aw:(part + 1) * aw]),)

    def finish_attn(part, r):
        scale = HEAD_DIM ** -0.5 * LOG2_E
        for lo in range(0, aw, HEAD_DIM):
            t = r[:, lo:lo + HEAD_DIM]
            if part < 2:
                t = t * cos_ref[...] + pltpu.roll(t, HEAD_DIM // 2, axis=1) * sin_ref[...]
            if part == 0:
                t = t * scale
            qkv_ref[:, part * aw + lo:part * aw + lo + HEAD_DIM] = t.astype(BF16)

    units = [(project_conv, finish_conv, c0) for c0 in range(0, cw, cg)]
    units += [(project_attn, finish_attn, part) for part in (0, 1, 2)]
    pending = units[0][0](units[0][2])
    ang = pos_ref[...] * inv_ref[...]
    cos_ref[...] = jnp.cos(ang)
    sin_ref[...] = jnp.sin(ang) * sign_ref[...]
    for n, (_, finish, arg) in enumerate(units):
        current = pending
        if n + 1 < len(units):
            pending = units[n + 1][0](units[n + 1][2])
        finish(arg, *current)


def _in_proj(x2d, g, w_in_bf, pos, inv_full, sign, w_conv, b_conv, g_conv, seq_len):
    n_tok, d_model = x2d.shape
    cw = w_conv.shape[1]
    aw = (w_in_bf.shape[1] - 3 * cw) // 3
    tm = ROW_TILE
    const = lambda i: (0, 0)
    row = lambda i: (i, 0)
    return pl.pallas_call(
        functools.partial(_in_proj_kernel, seq_len // tm),
        grid=(n_tok // tm,),
        in_specs=[
            pl.BlockSpec((tm, d_model), row),
            pl.BlockSpec((1, d_model), const),
            pl.BlockSpec(w_in_bf.shape, const, pipeline_mode=pl.Buffered(1)),
            pl.BlockSpec((tm, 1), row),
            pl.BlockSpec((1, HEAD_DIM), const),
            pl.BlockSpec((1, HEAD_DIM), const),
            pl.BlockSpec((CONV_KSIZE, cw), const),
            pl.BlockSpec((1, cw), const),
            pl.BlockSpec((1, cw), const),
        ],
        out_specs=[pl.BlockSpec((tm, 3 * aw), row), pl.BlockSpec((tm, cw), row)],
        out_shape=[jax.ShapeDtypeStruct((n_tok, 3 * aw), BF16), jax.ShapeDtypeStruct((n_tok, cw), BF16)],
        scratch_shapes=[pltpu.VMEM((tm + SUBLANES, cw), F32),
                        pltpu.VMEM((tm, HEAD_DIM), F32), pltpu.VMEM((tm, HEAD_DIM), F32)],
        compiler_params=_params("arbitrary"),
        name="in_proj",
    )(x2d, g, w_in_bf, pos, inv_full, sign, w_conv, b_conv, g_conv)


def _split_bf16(a, parts):
    out = []
    for _ in range(parts):
        piece = a.astype(BF16)
        out.append(piece)
        a = a - piece.astype(F32)
    return out


def _moba_kernel(q_ref, k_ref, v_ref, g_ref, o_ref, qt_ref, vt_ref, m_ref, acc_ref):
    blk = MOBA_BLOCK
    hd = HEAD_DIM
    seq = k_ref.shape[0]
    n_blk = seq // blk

    for c in range(n_blk):
        rows = slice(c * blk, (c + 1) * blk)
        qt_ref[0:hd, rows] = q_ref[rows, :].astype(F32).T.astype(BF16)
        vt_ref[0:hd, rows] = v_ref[rows, :].astype(F32).T.astype(BF16)
    qt_ref[hd + n_blk:, :] = jnp.zeros((qt_ref.shape[0] - hd - n_blk, seq), BF16)
    vt_ref[hd:, :] = jnp.ones((vt_ref.shape[0] - hd, seq), BF16)

    kf = k_ref[...].astype(F32).reshape(n_blk, blk, hd)
    kmean = jnp.sum(kf, axis=1) * (1.0 / blk)
    kmean_parts = _split_bf16(kmean, 3)

    key_id = lax.broadcasted_iota(jnp.int32, (blk, blk), 0)
    qry_id = lax.broadcasted_iota(jnp.int32, (blk, blk), 1)
    causal = key_id <= qry_id
    lane_id = lax.broadcasted_iota(jnp.int32, (blk, hd), 1)

    def mask_own_tile(s):
        own = jnp.where(causal, s[:, :blk], MASK_VALUE)
        return own if s.shape[1] == blk else jnp.concatenate([own, s[:, blk:]], axis=1)

    def finish_tile(c0):
        o = (acc_ref[0:hd, c0:c0 + blk] / acc_ref[hd:hd + 1, c0:c0 + blk]).T
        o_ref[c0:c0 + blk, :] = _rms_norm(o, g_ref[...]).astype(BF16)

    r = _dot(jnp.concatenate([k_ref[0:blk, :]] + kmean_parts, axis=0), qt_ref[0:hd, :])
    gate = (r[blk:blk + n_blk] + r[blk + n_blk:blk + 2 * n_blk]) + r[blk + 2 * n_blk:blk + 3 * n_blk]
    blk_id = lax.broadcasted_iota(jnp.int32, (n_blk, seq), 0)
    q_tile = lax.broadcasted_iota(jnp.int32, (n_blk, seq), 1) // blk
    rank = jnp.zeros((n_blk, seq), jnp.int32)
    for j in range(n_blk):
        gj = gate[j:j + 1, :]
        beats = (gj > gate) | ((gj == gate) & (j < blk_id))
        rank = rank + jnp.where(beats & (j < q_tile), 1, 0)
    visible = ((blk_id < q_tile) & (rank < MOBA_TOPK)) | (blk_id == q_tile)
    bias = jnp.where(visible, 0.0, MASK_VALUE).astype(F32)
    qt_ref[hd:hd + n_blk, :] = bias.astype(BF16)

    s = mask_own_tile(r[0:blk] + bias[0:1, :])
    m_new = jnp.max(s, axis=0, keepdims=True)
    acc_ref[...] = _dot(vt_ref[:, 0:blk], jnp.exp2(s - m_new).astype(BF16))
    m_ref[...] = m_new
    finish_tile(0)

    def scores(j):
        c0 = j * blk
        k_aug = jnp.concatenate([k_ref[c0:c0 + blk, :], jnp.where(lane_id == j, 1.0, 0.0).astype(BF16)], axis=1)
        return mask_own_tile(_dot(k_aug, qt_ref[:, c0:]))

    s_next = scores(1)
    for j in range(1, n_blk):
        c0 = j * blk
        s = s_next
        if j + 1 < n_blk:
            s_next = scores(j + 1)
        m_old = m_ref[:, c0:]
        m_new = jnp.maximum(m_old, jnp.max(s, axis=0, keepdims=True))
        alpha = jnp.exp2(m_old - m_new)
        p = jnp.exp2(s - m_new).astype(BF16)
        acc_ref[:, c0:] = alpha * acc_ref[:, c0:] + _dot(vt_ref[:, c0:c0 + blk], p)
        m_ref[:, c0:] = m_new
        finish_tile(c0)


def _moba_attention(qkv, g_attn, bsz, seq_len):
    n_tok = qkv.shape[0]
    nh = N_ATTN_HEADS
    n_blk = seq_len // MOBA_BLOCK
    return pl.pallas_call(
        _moba_kernel,
        grid=(bsz, nh),
        in_specs=[
            pl.BlockSpec((seq_len, HEAD_DIM), lambda b, h: (b, h)),
            pl.BlockSpec((seq_len, HEAD_DIM), lambda b, h: (b, nh + h)),
            pl.BlockSpec((seq_len, HEAD_DIM), lambda b, h: (b, 2 * nh + h)),
            pl.BlockSpec((1, HEAD_DIM), lambda b, h: (0, h)),
        ],
        out_specs=pl.BlockSpec((seq_len, HEAD_DIM), lambda b, h: (b, h)),
        out_shape=jax.ShapeDtypeStruct((n_tok, nh * HEAD_DIM), BF16),
        scratch_shapes=[
            pltpu.VMEM((MXU_DEPTH, seq_len), BF16),
            pltpu.VMEM((HEAD_DIM + BF16_PACK, seq_len), BF16),
            pltpu.VMEM((1, seq_len), F32),
            pltpu.VMEM((HEAD_DIM + BF16_PACK, seq_len), F32),
        ],
        compiler_params=_params("parallel", "parallel"),
        name="moba_attention",
    )(qkv, qkv, qkv, g_attn)


def _out_proj_kernel(a_ref, c_ref, w_ref, x_ref, o_ref):
    aw = a_ref.shape[1]
    y = _dot(a_ref[...], w_ref[0:aw, :]) + _dot(c_ref[...], w_ref[aw:, :])
    o_ref[...] = x_ref[...] + y


def _out_proj(o_attn, y_conv, w_out_bf, x2d):
    n_tok, d_model = x2d.shape
    tm = ROW_TILE
    return pl.pallas_call(
        _out_proj_kernel,
        grid=(n_tok // tm,),
        in_specs=[
            pl.BlockSpec((tm, o_attn.shape[1]), lambda i: (i, 0)),
            pl.BlockSpec((tm, y_conv.shape[1]), lambda i: (i, 0)),
            pl.BlockSpec(w_out_bf.shape, lambda i: (0, 0)),
            pl.BlockSpec((tm, d_model), lambda i: (i, 0)),
        ],
        out_specs=pl.BlockSpec((tm, d_model), lambda i: (i, 0)),
        out_shape=jax.ShapeDtypeStruct((n_tok, d_model), F32),
        compiler_params=_params("parallel"),
        name="out_proj",
    )(o_attn, y_conv, w_out_bf, x2d)


def _mem_kv_kernel(m_ref, g_ref, w_ref, o_ref):
    h = _rms_norm(m_ref[0], g_ref[...]).astype(BF16)
    o_ref[0] = _dot(h, w_ref[...]).astype(BF16)


def _mem_kv(mem, g, w_xkv_bf):
    bsz, n_mem, d_model = mem.shape
    width = w_xkv_bf.shape[1]
    return pl.pallas_call(
        _mem_kv_kernel,
        grid=(bsz,),
        in_specs=[
            pl.BlockSpec((1, n_mem, d_model), lambda b: (b, 0, 0)),
            pl.BlockSpec((1, d_model), lambda b: (0, 0)),
            pl.BlockSpec(w_xkv_bf.shape, lambda b: (0, 0)),
        ],
        out_specs=pl.BlockSpec((1, n_mem, width), lambda b: (b, 0, 0)),
        out_shape=jax.ShapeDtypeStruct((bsz, n_mem, width), BF16),
        compiler_params=_params("parallel"),
        name="mem_kv",
    )(mem, g, w_xkv_bf)


def _xattn_kernel(x_ref, g_ref, wq_ref, kv_ref, wo_ref, o_ref):
    x = x_ref[...]
    h = _rms_norm(x, g_ref[...]).astype(BF16)
    q = (_dot(h, wq_ref[...]) * (XATTN_HEAD_DIM ** -0.5)).astype(BF16)
    heads = []
    for hh in range(N_XATTN_HEADS):
        lo = hh * XATTN_HEAD_DIM
        k = kv_ref[0, :, lo:lo + XATTN_HEAD_DIM]
        v = kv_ref[0, :, XATTN_WIDTH + lo:XATTN_WIDTH + lo + XATTN_HEAD_DIM]
        s = _dot_nt(q[:, lo:lo + XATTN_HEAD_DIM], k)
        p = jnp.exp(s - jnp.max(s, axis=-1, keepdims=True))
        o = _dot(p.astype(BF16), v) / jnp.sum(p, axis=-1, keepdims=True)
        heads.append(o.astype(BF16))
    o_ref[...] = x + _dot(jnp.concatenate(heads, axis=1), wo_ref[...])


def _cross_attention(x2d, g, w_xq_bf, kv, w_xo_bf, seq_len):
    n_tok, d_model = x2d.shape
    tm = ROW_TILE
    tiles_per_seq = seq_len // tm
    return pl.pallas_call(
        _xattn_kernel,
        grid=(n_tok // tm,),
        in_specs=[
            pl.BlockSpec((tm, d_model), lambda i: (i, 0)),
            pl.BlockSpec((1, d_model), lambda i: (0, 0)),
            pl.BlockSpec(w_xq_bf.shape, lambda i: (0, 0)),
            pl.BlockSpec((1,) + kv.shape[1:], lambda i: (i // tiles_per_seq, 0, 0)),
            pl.BlockSpec(w_xo_bf.shape, lambda i: (0, 0)),
        ],
        out_specs=pl.BlockSpec((tm, d_model), lambda i: (i, 0)),
        out_shape=jax.ShapeDtypeStruct((n_tok, d_model), F32),
        compiler_params=_params("parallel"),
        name="cross_attention",
    )(x2d, g, w_xq_bf, kv, w_xo_bf)


def _ffn_kernel(final_norm, x_ref, g_ref, wg_ref, wu_ref, wd_ref, gf_ref, o_ref, h_ref):
    j = pl.program_id(1)

    @pl.when(j == 0)
    def _():
        x = x_ref[...]
        h_ref[...] = _rms_norm(x, g_ref[...]).astype(BF16)
        o_ref[...] = x

    h = h_ref[...]
    gate = _dot(h, wg_ref[...])
    up = _dot(h, wu_ref[...])
    act = (gate * jax.nn.sigmoid(gate) * up).astype(BF16)
    o_ref[...] += _dot(act, wd_ref[...])

    if final_norm:
        @pl.when(j == pl.num_programs(1) - 1)
        def _():
            o_ref[...] = _rms_norm(o_ref[...], gf_ref[...])


def _ffn(x2d, g, w_gate_bf, w_up_bf, w_down_bf, g_final, final_norm):
    n_tok, d_model = x2d.shape
    d_ff = w_gate_bf.shape[1]
    tm, tf = FFN_ROW_TILE, FF_TILE
    return pl.pallas_call(
        functools.partial(_ffn_kernel, final_norm),
        grid=(n_tok // tm, d_ff // tf),
        in_specs=[
            pl.BlockSpec((tm, d_model), lambda i, j: (i, 0)),
            pl.BlockSpec((1, d_model), lambda i, j: (0, 0)),
            pl.BlockSpec((d_model, tf), lambda i, j: (0, j)),
            pl.BlockSpec((d_model, tf), lambda i, j: (0, j)),
            pl.BlockSpec((tf, d_model), lambda i, j: (j, 0)),
            pl.BlockSpec((1, d_model), lambda i, j: (0, 0)),
        ],
        out_specs=pl.BlockSpec((tm, d_model), lambda i, j: (i, 0)),
        out_shape=jax.ShapeDtypeStruct((n_tok, d_model), F32),
        scratch_shapes=[pltpu.VMEM((tm, d_model), BF16)],
        compiler_params=_params("parallel", "arbitrary", vmem_limit=FFN_VMEM_LIMIT),
        name="ffn",
    )(x2d, g, w_gate_bf, w_up_bf, w_down_bf, g_final)


def kernel(x, mem, positions, g_mix, w_in, w_conv, b_conv, g_attn_out, g_conv_out, w_out, g_xattn, g_mem,
           w_xq, w_xkv, w_xo, g_ffn, w_gate, w_up, w_down, g_final):
    bsz, seq_len, d_model = x.shape
    depth = g_mix.shape[0]
    row = lambda a: a.reshape(1, -1)
    pos, inv_full, sign = _rope_inputs(positions)
    x2d = x.reshape(bsz * seq_len, d_model)
    for l in range(depth):
        w_in_bf = w_in[l].astype(BF16)
        qkv, y_conv = _in_proj(x2d, row(g_mix[l]), w_in_bf, pos, inv_full, sign, w_conv[l], row(b_conv[l]),
                               row(g_conv_out[l]), seq_len)
        o_attn = _moba_attention(qkv, row(g_attn_out[l]), bsz, seq_len)
        x2d = _out_proj(o_attn, y_conv, w_out[l].astype(BF16), x2d)
        kv = _mem_kv(mem, row(g_mem[l]), w_xkv[l].astype(BF16))
        x2d = _cross_attention(x2d, row(g_xattn[l]), w_xq[l].astype(BF16), kv, w_xo[l].astype(BF16), seq_len)
        x2d = _ffn(x2d, row(g_ffn[l]), w_gate[l].astype(BF16), w_up[l].astype(BF16), w_down[l].astype(BF16),
                   row(g_final), final_norm=l == depth - 1)
    return x2d.reshape(bsz, seq_len, d_model)
```

```python
import functools

import jax
import jax.numpy as jnp
from jax import lax
from jax.experimental import pallas as pl
from jax.experimental.pallas import tpu as pltpu

HEAD_DIM = 128
N_ATTN_HEADS = 8
ATTN_WIDTH = N_ATTN_HEADS * HEAD_DIM
CONV_KSIZE = 3
MOBA_BLOCK = 256
MOBA_TOPK = 3
ROPE_THETA = 10000.0
N_XATTN_HEADS = 4
XATTN_HEAD_DIM = 128
XATTN_WIDTH = N_XATTN_HEADS * XATTN_HEAD_DIM
EPS = 1e-6

LOG2_E = 1.4426950408889634
MASK_VALUE = -1e30
SUBLANES = 8
VMEM_LIMIT = 56 * 1024 * 1024

ROW_TILE = 512
FF_TILE = 512
CONV_COL_GROUP = 256
FFN_ROW_TILE = 1024
FFN_VMEM_LIMIT = 60 * 1024 * 1024
BF16_PACK = 16
MXU_DEPTH = 256

BF16 = jnp.bfloat16
F32 = jnp.float32


def _rms_norm(xf, g):
    y = xf * lax.rsqrt(jnp.mean(xf * xf, axis=-1, keepdims=True) + EPS)
    return y * g


def _dot(a, b):
    return jnp.dot(a, b, preferred_element_type=F32)


def _dot_nt(a, b):
    return lax.dot_general(a, b, (((1,), (1,)), ((), ())), preferred_element_type=F32)


def _dot_tn(a, b):
    return lax.dot_general(a, b, (((0,), (0,)), ((), ())), preferred_element_type=F32)


def _params(*semantics, vmem_limit=VMEM_LIMIT):
    return pltpu.CompilerParams(dimension_semantics=semantics, vmem_limit_bytes=vmem_limit)


def _rope_inputs(positions):
    half = HEAD_DIM // 2
    inv_freq = ROPE_THETA ** (-jnp.arange(0, HEAD_DIM, 2, dtype=F32) / HEAD_DIM)
    inv_full = jnp.concatenate([inv_freq, inv_freq]).reshape(1, HEAD_DIM)
    sign = jnp.concatenate([-jnp.ones((half,), F32), jnp.ones((half,), F32)]).reshape(1, HEAD_DIM)
    return positions.astype(F32).reshape(positions.size, 1), inv_full, sign


def _in_proj_kernel(tiles_per_seq, x_ref, g_ref, w_ref, pos_ref, inv_ref, sign_ref, wc_ref, bc_ref, gc_ref,
                    qkv_ref, conv_ref, uext_ref, cos_ref, sin_ref):
    tm = x_ref.shape[0]
    aw = qkv_ref.shape[1] // 3
    cw = conv_ref.shape[1]
    cg = CONV_COL_GROUP

    @pl.when(pl.program_id(0) % tiles_per_seq == 0)
    def _():
        uext_ref[0:SUBLANES, :] = jnp.zeros((SUBLANES, cw), F32)

    h = _rms_norm(x_ref[...], g_ref[...]).astype(BF16)

    def project_conv(c0):
        return tuple(_dot(h, w_ref[:, 3 * aw + part * cw + c0:3 * aw + part * cw + c0 + cg]) for part in range(3))

    def finish_conv(c0, c_gate, b_gate, x_in):
        cols = slice(c0, c0 + cg)
        u = c_gate * x_in
        uext_ref[SUBLANES:, cols] = u
        u1 = uext_ref[SUBLANES - 1:SUBLANES - 1 + tm, cols]
        u2 = uext_ref[SUBLANES - 2:SUBLANES - 2 + tm, cols]
        conv = wc_ref[0:1, cols] * u2 + wc_ref[1:2, cols] * u1 + wc_ref[2:3, cols] * u
        y = b_gate * (conv + bc_ref[:, cols])
        uext_ref[0:SUBLANES, cols] = u[tm - SUBLANES:, :]
        for lo in range(0, cg, HEAD_DIM):
            yg = _rms_norm(y[:, lo:lo + HEAD_DIM], gc_ref[:, c0 + lo:c0 + lo + HEAD_DIM])
            conv_ref[:, c0 + lo:c0 + lo + HEAD_DIM] = yg.astype(BF16)

    def project_attn(part):
        return (_dot(h, w_ref[:, part * aw:(part + 1) * aw]),)

    def finish_attn(part, r):
        scale = HEAD_DIM ** -0.5 * LOG2_E
        for lo in range(0, aw, HEAD_DIM):
            t = r[:, lo:lo + HEAD_DIM]
            if part < 2:
                t = t * cos_ref[...] + pltpu.roll(t, HEAD_DIM // 2, axis=1) * sin_ref[...]
            if part == 0:
                t = t * scale
            qkv_ref[:, part * aw + lo:part * aw + lo + HEAD_DIM] = t.astype(BF16)

    units = [(project_conv, finish_conv, c0) for c0 in range(0, cw, cg)]
    units += [(project_attn, finish_attn, part) for part in (0, 1, 2)]
    pending = units[0][0](units[0][2])
    ang = pos_ref[...] * inv_ref[...]
    cos_ref[...] = jnp.cos(ang)
    sin_ref[...] = jnp.sin(ang) * sign_ref[...]
    for n, (_, finish, arg) in enumerate(units):
        current = pending
        if n + 1 < len(units):
            pending = units[n + 1][0](units[n + 1][2])
        finish(arg, *current)


def _in_proj(x2d, g, w_in_bf, pos, inv_full, sign, w_conv, b_conv, g_conv, seq_len):
    n_tok, d_model = x2d.shape
    cw = w_conv.shape[1]
    aw = (w_in_bf.shape[1] - 3 * cw) // 3
    tm = ROW_TILE
    const = lambda i: (0, 0)
    row = lambda i: (i, 0)
    return pl.pallas_call(
        functools.partial(_in_proj_kernel, seq_len // tm),
        grid=(n_tok // tm,),
        in_specs=[
            pl.BlockSpec((tm, d_model), row),
            pl.BlockSpec((1, d_model), const),
            pl.BlockSpec(w_in_bf.shape, const, pipeline_mode=pl.Buffered(1)),
            pl.BlockSpec((tm, 1), row),
            pl.BlockSpec((1, HEAD_DIM), const),
            pl.BlockSpec((1, HEAD_DIM), const),
            pl.BlockSpec((CONV_KSIZE, cw), const),
            pl.BlockSpec((1, cw), const),
            pl.BlockSpec((1, cw), const),
        ],
        out_specs=[pl.BlockSpec((tm, 3 * aw), row), pl.BlockSpec((tm, cw), row)],
        out_shape=[jax.ShapeDtypeStruct((n_tok, 3 * aw), BF16), jax.ShapeDtypeStruct((n_tok, cw), BF16)],
        scratch_shapes=[pltpu.VMEM((tm + SUBLANES, cw), F32),
                        pltpu.VMEM((tm, HEAD_DIM), F32), pltpu.VMEM((tm, HEAD_DIM), F32)],
        compiler_params=_params("arbitrary"),
        name="in_proj",
    )(x2d, g, w_in_bf, pos, inv_full, sign, w_conv, b_conv, g_conv)


def _split_bf16(a, parts):
    out = []
    for _ in range(parts):
        piece = a.astype(BF16)
        out.append(piece)
        a = a - piece.astype(F32)
    return out


def _moba_kernel(n_cast, q_ref, k_ref, v_ref, g_ref, *refs):
    cast_in, o_ref, cast_out = refs[:n_cast], refs[n_cast], refs[n_cast + 1:2 * n_cast + 1]
    qt_ref, vt_ref, m_ref, acc_ref = refs[2 * n_cast + 1:]
    blk = MOBA_BLOCK
    hd = HEAD_DIM
    seq = k_ref.shape[0]
    n_blk = seq // blk

    for c in range(n_blk):
        rows = slice(c * blk, (c + 1) * blk)
        qt_ref[0:hd, rows] = q_ref[rows, :].astype(F32).T.astype(BF16)
        vt_ref[0:hd, rows] = v_ref[rows, :].astype(F32).T.astype(BF16)
    qt_ref[hd + n_blk:, :] = jnp.zeros((qt_ref.shape[0] - hd - n_blk, seq), BF16)
    vt_ref[hd:, :] = jnp.ones((vt_ref.shape[0] - hd, seq), BF16)

    kf = k_ref[...].astype(F32).reshape(n_blk, blk, hd)
    kmean = jnp.sum(kf, axis=1) * (1.0 / blk)
    kmean_parts = _split_bf16(kmean, 3)

    key_id = lax.broadcasted_iota(jnp.int32, (blk, blk), 0)
    qry_id = lax.broadcasted_iota(jnp.int32, (blk, blk), 1)
    causal = key_id <= qry_id
    lane_id = lax.broadcasted_iota(jnp.int32, (blk, hd), 1)

    def mask_own_tile(s):
        own = jnp.where(causal, s[:, :blk], MASK_VALUE)
        return own if s.shape[1] == blk else jnp.concatenate([own, s[:, blk:]], axis=1)

    def finish_tile(c0):
        o = (acc_ref[0:hd, c0:c0 + blk] / acc_ref[hd:hd + 1, c0:c0 + blk]).T
        o_ref[c0:c0 + blk, :] = _rms_norm(o, g_ref[...]).astype(BF16)

    r = _dot(jnp.concatenate([k_ref[0:blk, :]] + kmean_parts, axis=0), qt_ref[0:hd, :])
    for src, dst in zip(cast_in, cast_out):
        dst[...] = src[...].astype(BF16)
    gate = (r[blk:blk + n_blk] + r[blk + n_blk:blk + 2 * n_blk]) + r[blk + 2 * n_blk:blk + 3 * n_blk]
    blk_id = lax.broadcasted_iota(jnp.int32, (n_blk, seq), 0)
    q_tile = lax.broadcasted_iota(jnp.int32, (n_blk, seq), 1) // blk
    rank = jnp.zeros((n_blk, seq), jnp.int32)
    for j in range(n_blk):
        gj = gate[j:j + 1, :]
        beats = (gj > gate) | ((gj == gate) & (j < blk_id))
        rank = rank + jnp.where(beats & (j < q_tile), 1, 0)
    visible = ((blk_id < q_tile) & (rank < MOBA_TOPK)) | (blk_id == q_tile)
    bias = jnp.where(visible, 0.0, MASK_VALUE).astype(F32)
    qt_ref[hd:hd + n_blk, :] = bias.astype(BF16)

    s = mask_own_tile(r[0:blk] + bias[0:1, :])
    m_new = jnp.max(s, axis=0, keepdims=True)
    acc_ref[...] = _dot(vt_ref[:, 0:blk], jnp.exp2(s - m_new).astype(BF16))
    m_ref[...] = m_new
    finish_tile(0)

    def scores(j):
        c0 = j * blk
        k_aug = jnp.concatenate([k_ref[c0:c0 + blk, :], jnp.where(lane_id == j, 1.0, 0.0).astype(BF16)], axis=1)
        return mask_own_tile(_dot(k_aug, qt_ref[:, c0:]))

    s_next = scores(1)
    for j in range(1, n_blk):
        c0 = j * blk
        s = s_next
        if j + 1 < n_blk:
            s_next = scores(j + 1)
        m_old = m_ref[:, c0:]
        m_new = jnp.maximum(m_old, jnp.max(s, axis=0, keepdims=True))
        alpha = jnp.exp2(m_old - m_new)
        p = jnp.exp2(s - m_new).astype(BF16)
        acc_ref[:, c0:] = alpha * acc_ref[:, c0:] + _dot(vt_ref[:, c0:c0 + blk], p)
        m_ref[:, c0:] = m_new
        finish_tile(c0)


def _moba_attention(qkv, g_attn, bsz, seq_len, later_weights):
    n_tok = qkv.shape[0]
    nh = N_ATTN_HEADS
    n_blk = seq_len // MOBA_BLOCK
    steps = bsz * nh
    slab = lambda b, h: (b * nh + h, 0)
    for w in later_weights:
        assert w.shape[0] % (steps * BF16_PACK) == 0, w.shape
    w_blocks = [(w.shape[0] // steps, w.shape[1]) for w in later_weights]
    outs = pl.pallas_call(
        functools.partial(_moba_kernel, len(later_weights)),
        grid=(bsz, nh),
        in_specs=[
            pl.BlockSpec((seq_len, HEAD_DIM), lambda b, h: (b, h)),
            pl.BlockSpec((seq_len, HEAD_DIM), lambda b, h: (b, nh + h)),
            pl.BlockSpec((seq_len, HEAD_DIM), lambda b, h: (b, 2 * nh + h)),
            pl.BlockSpec((1, HEAD_DIM), lambda b, h: (0, h)),
        ] + [pl.BlockSpec(blk, slab) for blk in w_blocks],
        out_specs=[pl.BlockSpec((seq_len, HEAD_DIM), lambda b, h: (b, h))]
        + [pl.BlockSpec(blk, slab) for blk in w_blocks],
        out_shape=[jax.ShapeDtypeStruct((n_tok, nh * HEAD_DIM), BF16)]
        + [jax.ShapeDtypeStruct(w.shape, BF16) for w in later_weights],
        scratch_shapes=[
            pltpu.VMEM((MXU_DEPTH, seq_len), BF16),
            pltpu.VMEM((HEAD_DIM + BF16_PACK, seq_len), BF16),
            pltpu.VMEM((1, seq_len), F32),
            pltpu.VMEM((HEAD_DIM + BF16_PACK, seq_len), F32),
        ],
        compiler_params=_params("parallel", "parallel"),
        name="moba_attention",
    )(qkv, qkv, qkv, g_attn, *later_weights)
    return outs[0], outs[1:]


def _out_proj_kernel(a_ref, c_ref, w_ref, x_ref, o_ref):
    aw = a_ref.shape[1]
    y = _dot(a_ref[...], w_ref[0:aw, :]) + _dot(c_ref[...], w_ref[aw:, :])
    o_ref[...] = x_ref[...] + y


def _out_proj(o_attn, y_conv, w_out_bf, x2d):
    n_tok, d_model = x2d.shape
    tm = ROW_TILE
    return pl.pallas_call(
        _out_proj_kernel,
        grid=(n_tok // tm,),
        in_specs=[
            pl.BlockSpec((tm, o_attn.shape[1]), lambda i: (i, 0)),
            pl.BlockSpec((tm, y_conv.shape[1]), lambda i: (i, 0)),
            pl.BlockSpec(w_out_bf.shape, lambda i: (0, 0)),
            pl.BlockSpec((tm, d_model), lambda i: (i, 0)),
        ],
        out_specs=pl.BlockSpec((tm, d_model), lambda i: (i, 0)),
        out_shape=jax.ShapeDtypeStruct((n_tok, d_model), F32),
        compiler_params=_params("parallel"),
        name="out_proj",
    )(o_attn, y_conv, w_out_bf, x2d)


def _mem_kv_kernel(m_ref, g_ref, w_ref, o_ref):
    h = _rms_norm(m_ref[0], g_ref[...]).astype(BF16)
    o_ref[0] = _dot(h, w_ref[...]).astype(BF16)


def _mem_kv(mem, g, w_xkv_bf):
    bsz, n_mem, d_model = mem.shape
    width = w_xkv_bf.shape[1]
    return pl.pallas_call(
        _mem_kv_kernel,
        grid=(bsz,),
        in_specs=[
            pl.BlockSpec((1, n_mem, d_model), lambda b: (b, 0, 0)),
            pl.BlockSpec((1, d_model), lambda b: (0, 0)),
            pl.BlockSpec(w_xkv_bf.shape, lambda b: (0, 0)),
        ],
        out_specs=pl.BlockSpec((1, n_mem, width), lambda b: (b, 0, 0)),
        out_shape=jax.ShapeDtypeStruct((bsz, n_mem, width), BF16),
        compiler_params=_params("parallel"),
        name="mem_kv",
    )(mem, g, w_xkv_bf)


def _xattn_kernel(x_ref, g_ref, wq_ref, kv_ref, wo_ref, o_ref):
    x = x_ref[...]
    h = _rms_norm(x, g_ref[...]).astype(BF16)
    q = (_dot(h, wq_ref[...]) * (XATTN_HEAD_DIM ** -0.5)).astype(BF16)
    heads = []
    for hh in range(N_XATTN_HEADS):
        lo = hh * XATTN_HEAD_DIM
        k = kv_ref[0, :, lo:lo + XATTN_HEAD_DIM]
        v = kv_ref[0, :, XATTN_WIDTH + lo:XATTN_WIDTH + lo + XATTN_HEAD_DIM]
        s = _dot_nt(q[:, lo:lo + XATTN_HEAD_DIM], k)
        p = jnp.exp(s - jnp.max(s, axis=-1, keepdims=True))
        o = _dot(p.astype(BF16), v) / jnp.sum(p, axis=-1, keepdims=True)
        heads.append(o.astype(BF16))
    o_ref[...] = x + _dot(jnp.concatenate(heads, axis=1), wo_ref[...])


def _cross_attention(x2d, g, w_xq_bf, kv, w_xo_bf, seq_len):
    n_tok, d_model = x2d.shape
    tm = ROW_TILE
    tiles_per_seq = seq_len // tm
    return pl.pallas_call(
        _xattn_kernel,
        grid=(n_tok // tm,),
        in_specs=[
            pl.BlockSpec((tm, d_model), lambda i: (i, 0)),
            pl.BlockSpec((1, d_model), lambda i: (0, 0)),
            pl.BlockSpec(w_xq_bf.shape, lambda i: (0, 0)),
            pl.BlockSpec((1,) + kv.shape[1:], lambda i: (i // tiles_per_seq, 0, 0)),
            pl.BlockSpec(w_xo_bf.shape, lambda i: (0, 0)),
        ],
        out_specs=pl.BlockSpec((tm, d_model), lambda i: (i, 0)),
        out_shape=jax.ShapeDtypeStruct((n_tok, d_model), F32),
        compiler_params=_params("parallel"),
        name="cross_attention",
    )(x2d, g, w_xq_bf, kv, w_xo_bf)


def _ffn_kernel(final_norm, x_ref, g_ref, wg_ref, wu_ref, wd_ref, gf_ref, o_ref, h_ref):
    j = pl.program_id(1)

    @pl.when(j == 0)
    def _():
        x = x_ref[...]
        h_ref[...] = _rms_norm(x, g_ref[...]).astype(BF16)
        o_ref[...] = x

    h = h_ref[...]
    gate = _dot(h, wg_ref[...])
    up = _dot(h, wu_ref[...])
    act = (gate * jax.nn.sigmoid(gate) * up).astype(BF16)
    o_ref[...] += _dot(act, wd_ref[...])

    if final_norm:
        @pl.when(j == pl.num_programs(1) - 1)
        def _():
            o_ref[...] = _rms_norm(o_ref[...], gf_ref[...])


def _ffn(x2d, g, w_gate_bf, w_up_bf, w_down_bf, g_final, final_norm):
    n_tok, d_model = x2d.shape
    d_ff = w_gate_bf.shape[1]
    tm, tf = FFN_ROW_TILE, FF_TILE
    return pl.pallas_call(
        functools.partial(_ffn_kernel, final_norm),
        grid=(n_tok // tm, d_ff // tf),
        in_specs=[
            pl.BlockSpec((tm, d_model), lambda i, j: (i, 0)),
            pl.BlockSpec((1, d_model), lambda i, j: (0, 0)),
            pl.BlockSpec((d_model, tf), lambda i, j: (0, j)),
            pl.BlockSpec((d_model, tf), lambda i, j: (0, j)),
            pl.BlockSpec((tf, d_model), lambda i, j: (j, 0)),
            pl.BlockSpec((1, d_model), lambda i, j: (0, 0)),
        ],
        out_specs=pl.BlockSpec((tm, d_model), lambda i, j: (i, 0)),
        out_shape=jax.ShapeDtypeStruct((n_tok, d_model), F32),
        scratch_shapes=[pltpu.VMEM((tm, d_model), BF16)],
        compiler_params=_params("parallel", "arbitrary", vmem_limit=FFN_VMEM_LIMIT),
        name="ffn",
    )(x2d, g, w_gate_bf, w_up_bf, w_down_bf, g_final)


def kernel(x, mem, positions, g_mix, w_in, w_conv, b_conv, g_attn_out, g_conv_out, w_out, g_xattn, g_mem,
           w_xq, w_xkv, w_xo, g_ffn, w_gate, w_up, w_down, g_final):
    bsz, seq_len, d_model = x.shape
    depth = g_mix.shape[0]
    row = lambda a: a.reshape(1, -1)
    pos, inv_full, sign = _rope_inputs(positions)
    x2d = x.reshape(bsz * seq_len, d_model)
    for l in range(depth):
        qkv, y_conv = _in_proj(x2d, row(g_mix[l]), w_in[l].astype(BF16), pos, inv_full, sign, w_conv[l],
                               row(b_conv[l]), row(g_conv_out[l]), seq_len)
        o_attn, (w_out_bf, w_xq_bf, w_xkv_bf, w_xo_bf, w_gate_bf, w_up_bf, w_down_bf) = _moba_attention(
            qkv, row(g_attn_out[l]), bsz, seq_len,
            [w_out[l], w_xq[l], w_xkv[l], w_xo[l], w_gate[l], w_up[l], w_down[l]])
        x2d = _out_proj(o_attn, y_conv, w_out_bf, x2d)
        kv = _mem_kv(mem, row(g_mem[l]), w_xkv_bf)
        x2d = _cross_attention(x2d, row(g_xattn[l]), w_xq_bf, kv, w_xo_bf, seq_len)
        x2d = _ffn(x2d, row(g_ffn[l]), w_gate_bf, w_up_bf, w_down_bf, row(g_final), final_norm=l == depth - 1)
    return x2d.reshape(bsz, seq_len, d_model)
```

```python
import functools

import jax
import jax.numpy as jnp
from jax import lax
from jax.experimental import pallas as pl
from jax.experimental.pallas import tpu as pltpu

HEAD_DIM = 128
N_ATTN_HEADS = 8
ATTN_WIDTH = N_ATTN_HEADS * HEAD_DIM
CONV_KSIZE = 3
MOBA_BLOCK = 256
MOBA_TOPK = 3
ROPE_THETA = 10000.0
N_XATTN_HEADS = 4
XATTN_HEAD_DIM = 128
XATTN_WIDTH = N_XATTN_HEADS * XATTN_HEAD_DIM
EPS = 1e-6

LOG2_E = 1.4426950408889634
MASK_VALUE = -1e30
SUBLANES = 8
VMEM_LIMIT = 56 * 1024 * 1024

ROW_TILE = 512
FF_TILE = 512
CONV_COL_GROUP = 256
FFN_ROW_TILE = 1024
FFN_VMEM_LIMIT = 60 * 1024 * 1024
BF16_PACK = 16
MXU_DEPTH = 256

BF16 = jnp.bfloat16
F32 = jnp.float32


def _rms_norm(xf, g):
    y = xf * lax.rsqrt(jnp.mean(xf * xf, axis=-1, keepdims=True) + EPS)
    return y * g


def _dot(a, b):
    return jnp.dot(a, b, preferred_element_type=F32)


def _dot_nt(a, b):
    return lax.dot_general(a, b, (((1,), (1,)), ((), ())), preferred_element_type=F32)


def _dot_tn(a, b):
    return lax.dot_general(a, b, (((0,), (0,)), ((), ())), preferred_element_type=F32)


def _params(*semantics, vmem_limit=VMEM_LIMIT):
    return pltpu.CompilerParams(dimension_semantics=semantics, vmem_limit_bytes=vmem_limit)


def _rope_inputs(positions):
    half = HEAD_DIM // 2
    inv_freq = ROPE_THETA ** (-jnp.arange(0, HEAD_DIM, 2, dtype=F32) / HEAD_DIM)
    inv_full = jnp.concatenate([inv_freq, inv_freq]).reshape(1, HEAD_DIM)
    sign = jnp.concatenate([-jnp.ones((half,), F32), jnp.ones((half,), F32)]).reshape(1, HEAD_DIM)
    return positions.astype(F32).reshape(positions.size, 1), inv_full, sign


def _in_proj_kernel(tiles_per_seq, x_ref, g_ref, w_ref, pos_ref, inv_ref, sign_ref, wc_ref, bc_ref, gc_ref,
                    qkv_ref, conv_ref, uext_ref, cos_ref, sin_ref):
    tm = x_ref.shape[0]
    aw = qkv_ref.shape[1] // 3
    cw = conv_ref.shape[1]
    cg = CONV_COL_GROUP

    @pl.when(pl.program_id(0) % tiles_per_seq == 0)
    def _():
        uext_ref[0:SUBLANES, :] = jnp.zeros((SUBLANES, cw), F32)

    h = _rms_norm(x_ref[...], g_ref[...]).astype(BF16)

    def project_conv(c0):
        return tuple(_dot(h, w_ref[:, 3 * aw + part * cw + c0:3 * aw + part * cw + c0 + cg]) for part in range(3))

    def finish_conv(c0, c_gate, b_gate, x_in):
        cols = slice(c0, c0 + cg)
        u = c_gate * x_in
        uext_ref[SUBLANES:, cols] = u
        u1 = uext_ref[SUBLANES - 1:SUBLANES - 1 + tm, cols]
        u2 = uext_ref[SUBLANES - 2:SUBLANES - 2 + tm, cols]
        conv = wc_ref[0:1, cols] * u2 + wc_ref[1:2, cols] * u1 + wc_ref[2:3, cols] * u
        y = b_gate * (conv + bc_ref[:, cols])
        uext_ref[0:SUBLANES, cols] = u[tm - SUBLANES:, :]
        for lo in range(0, cg, HEAD_DIM):
            yg = _rms_norm(y[:, lo:lo + HEAD_DIM], gc_ref[:, c0 + lo:c0 + lo + HEAD_DIM])
            conv_ref[:, c0 + lo:c0 + lo + HEAD_DIM] = yg.astype(BF16)

    def project_attn(part):
        return (_dot(h, w_ref[:, part * aw:(part + 1) * aw]),)

    def finish_attn(part, r):
        scale = HEAD_DIM ** -0.5 * LOG2_E
        for lo in range(0, aw, HEAD_DIM):
            t = r[:, lo:lo + HEAD_DIM]
            if part < 2:
                t = t * cos_ref[...] + pltpu.roll(t, HEAD_DIM // 2, axis=1) * sin_ref[...]
            if part == 0:
                t = t * scale
            qkv_ref[:, part * aw + lo:part * aw + lo + HEAD_DIM] = t.astype(BF16)

    units = [(project_conv, finish_conv, c0) for c0 in range(0, cw, cg)]
    units += [(project_attn, finish_attn, part) for part in (0, 1, 2)]
    pending = units[0][0](units[0][2])
    ang = pos_ref[...] * inv_ref[...]
    cos_ref[...] = jnp.cos(ang)
    sin_ref[...] = jnp.sin(ang) * sign_ref[...]
    for n, (_, finish, arg) in enumerate(units):
        current = pending
        if n + 1 < len(units):
            pending = units[n + 1][0](units[n + 1][2])
        finish(arg, *current)


def _in_proj(x2d, g, w_in_bf, pos, inv_full, sign, w_conv, b_conv, g_conv, seq_len):
    n_tok, d_model = x2d.shape
    cw = w_conv.shape[1]
    aw = (w_in_bf.shape[1] - 3 * cw) // 3
    tm = ROW_TILE
    const = lambda i: (0, 0)
    row = lambda i: (i, 0)
    return pl.pallas_call(
        functools.partial(_in_proj_kernel, seq_len // tm),
        grid=(n_tok // tm,),
        in_specs=[
            pl.BlockSpec((tm, d_model), row),
            pl.BlockSpec((1, d_model), const),
            pl.BlockSpec(w_in_bf.shape, const, pipeline_mode=pl.Buffered(1)),
            pl.BlockSpec((tm, 1), row),
            pl.BlockSpec((1, HEAD_DIM), const),
            pl.BlockSpec((1, HEAD_DIM), const),
            pl.BlockSpec((CONV_KSIZE, cw), const),
            pl.BlockSpec((1, cw), const),
            pl.BlockSpec((1, cw), const),
        ],
        out_specs=[pl.BlockSpec((tm, 3 * aw), row), pl.BlockSpec((tm, cw), row)],
        out_shape=[jax.ShapeDtypeStruct((n_tok, 3 * aw), BF16), jax.ShapeDtypeStruct((n_tok, cw), BF16)],
        scratch_shapes=[pltpu.VMEM((tm + SUBLANES, cw), F32),
                        pltpu.VMEM((tm, HEAD_DIM), F32), pltpu.VMEM((tm, HEAD_DIM), F32)],
        compiler_params=_params("arbitrary"),
        name="in_proj",
    )(x2d, g, w_in_bf, pos, inv_full, sign, w_conv, b_conv, g_conv)


def _split_bf16(a, parts):
    out = []
    for _ in range(parts):
        piece = a.astype(BF16)
        out.append(piece)
        a = a - piece.astype(F32)
    return out


def _moba_kernel(n_cast, q_ref, k_ref, v_ref, g_ref, *refs):
    cast_in, o_ref, cast_out = refs[:n_cast], refs[n_cast], refs[n_cast + 1:2 * n_cast + 1]
    qt_ref, vt_ref, m_ref, acc_ref = refs[2 * n_cast + 1:]
    blk = MOBA_BLOCK
    hd = HEAD_DIM
    seq = k_ref.shape[0]
    n_blk = seq // blk

    for c in range(n_blk):
        rows = slice(c * blk, (c + 1) * blk)
        qt_ref[0:hd, rows] = q_ref[rows, :].astype(F32).T.astype(BF16)
        vt_ref[0:hd, rows] = v_ref[rows, :].astype(F32).T.astype(BF16)
    qt_ref[hd + n_blk:, :] = jnp.zeros((qt_ref.shape[0] - hd - n_blk, seq), BF16)
    vt_ref[hd:, :] = jnp.ones((vt_ref.shape[0] - hd, seq), BF16)

    kf = k_ref[...].astype(F32).reshape(n_blk, blk, hd)
    kmean = jnp.sum(kf, axis=1) * (1.0 / blk)
    kmean_parts = _split_bf16(kmean, 3)

    key_id = lax.broadcasted_iota(jnp.int32, (blk, blk), 0)
    qry_id = lax.broadcasted_iota(jnp.int32, (blk, blk), 1)
    causal = key_id <= qry_id
    lane_id = lax.broadcasted_iota(jnp.int32, (blk, hd), 1)

    def mask_own_tile(s):
        own = jnp.where(causal, s[:, :blk], MASK_VALUE)
        return own if s.shape[1] == blk else jnp.concatenate([own, s[:, blk:]], axis=1)

    def finish_tile(c0):
        o = (acc_ref[0:hd, c0:c0 + blk] / acc_ref[hd:hd + 1, c0:c0 + blk]).T
        o_ref[c0:c0 + blk, :] = _rms_norm(o, g_ref[...]).astype(BF16)

    r = _dot(jnp.concatenate([k_ref[0:blk, :]] + kmean_parts, axis=0), qt_ref[0:hd, :])
    for src, dst in zip(cast_in, cast_out):
        dst[...] = src[...].astype(BF16)
    gate = (r[blk:blk + n_blk] + r[blk + n_blk:blk + 2 * n_blk]) + r[blk + 2 * n_blk:blk + 3 * n_blk]
    blk_id = lax.broadcasted_iota(jnp.int32, (n_blk, seq), 0)
    q_tile = lax.broadcasted_iota(jnp.int32, (n_blk, seq), 1) // blk
    rank = jnp.zeros((n_blk, seq), jnp.int32)
    for j in range(n_blk):
        gj = gate[j:j + 1, :]
        beats = (gj > gate) | ((gj == gate) & (j < blk_id))
        rank = rank + jnp.where(beats & (j < q_tile), 1, 0)
    visible = ((blk_id < q_tile) & (rank < MOBA_TOPK)) | (blk_id == q_tile)
    bias = jnp.where(visible, 0.0, MASK_VALUE).astype(F32)
    qt_ref[hd:hd + n_blk, :] = bias.astype(BF16)

    s = mask_own_tile(r[0:blk] + bias[0:1, :])
    m_new = jnp.max(s, axis=0, keepdims=True)
    acc_ref[...] = _dot(vt_ref[:, 0:blk], jnp.exp2(s - m_new).astype(BF16))
    m_ref[...] = m_new
    finish_tile(0)

    def scores(j):
        c0 = j * blk
        k_aug = jnp.concatenate([k_ref[c0:c0 + blk, :], jnp.where(lane_id == j, 1.0, 0.0).astype(BF16)], axis=1)
        s = mask_own_tile(_dot(k_aug, qt_ref[:, c0:]))
        return s, jnp.max(s, axis=0, keepdims=True)

    pending = scores(1)
    for j in range(1, n_blk):
        c0 = j * blk
        s, s_max = pending
        if j + 1 < n_blk:
            pending = scores(j + 1)
        m_old = m_ref[:, c0:]
        m_new = jnp.maximum(m_old, s_max)
        alpha = jnp.exp2(m_old - m_new)
        p = jnp.exp2(s - m_new).astype(BF16)
        acc_ref[:, c0:] = alpha * acc_ref[:, c0:] + _dot(vt_ref[:, c0:c0 + blk], p)
        m_ref[:, c0:] = m_new
        finish_tile(c0)


def _moba_attention(qkv, g_attn, bsz, seq_len, later_weights):
    n_tok = qkv.shape[0]
    nh = N_ATTN_HEADS
    n_blk = seq_len // MOBA_BLOCK
    steps = bsz * nh
    slab = lambda b, h: (b * nh + h, 0)
    for w in later_weights:
        assert w.shape[0] % (steps * BF16_PACK) == 0, w.shape
    w_blocks = [(w.shape[0] // steps, w.shape[1]) for w in later_weights]
    outs = pl.pallas_call(
        functools.partial(_moba_kernel, len(later_weights)),
        grid=(bsz, nh),
        in_specs=[
            pl.BlockSpec((seq_len, HEAD_DIM), lambda b, h: (b, h)),
            pl.BlockSpec((seq_len, HEAD_DIM), lambda b, h: (b, nh + h)),
            pl.BlockSpec((seq_len, HEAD_DIM), lambda b, h: (b, 2 * nh + h)),
            pl.BlockSpec((1, HEAD_DIM), lambda b, h: (0, h)),
        ] + [pl.BlockSpec(blk, slab) for blk in w_blocks],
        out_specs=[pl.BlockSpec((seq_len, HEAD_DIM), lambda b, h: (b, h))]
        + [pl.BlockSpec(blk, slab) for blk in w_blocks],
        out_shape=[jax.ShapeDtypeStruct((n_tok, nh * HEAD_DIM), BF16)]
        + [jax.ShapeDtypeStruct(w.shape, BF16) for w in later_weights],
        scratch_shapes=[
            pltpu.VMEM((MXU_DEPTH, seq_len), BF16),
            pltpu.VMEM((HEAD_DIM + BF16_PACK, seq_len), BF16),
            pltpu.VMEM((1, seq_len), F32),
            pltpu.VMEM((HEAD_DIM + BF16_PACK, seq_len), F32),
        ],
        compiler_params=_params("parallel", "parallel"),
        name="moba_attention",
    )(qkv, qkv, qkv, g_attn, *later_weights)
    return outs[0], outs[1:]


def _out_proj_kernel(a_ref, c_ref, w_ref, x_ref, o_ref):
    aw = a_ref.shape[1]
    y = _dot(a_ref[...], w_ref[0:aw, :]) + _dot(c_ref[...], w_ref[aw:, :])
    o_ref[...] = x_ref[...] + y


def _out_proj(o_attn, y_conv, w_out_bf, x2d):
    n_tok, d_model = x2d.shape
    tm = ROW_TILE
    return pl.pallas_call(
        _out_proj_kernel,
        grid=(n_tok // tm,),
        in_specs=[
            pl.BlockSpec((tm, o_attn.shape[1]), lambda i: (i, 0)),
            pl.BlockSpec((tm, y_conv.shape[1]), lambda i: (i, 0)),
            pl.BlockSpec(w_out_bf.shape, lambda i: (0, 0)),
            pl.BlockSpec((tm, d_model), lambda i: (i, 0)),
        ],
        out_specs=pl.BlockSpec((tm, d_model), lambda i: (i, 0)),
        out_shape=jax.ShapeDtypeStruct((n_tok, d_model), F32),
        compiler_params=_params("parallel"),
        name="out_proj",
    )(o_attn, y_conv, w_out_bf, x2d)


def _mem_kv_kernel(m_ref, g_ref, w_ref, o_ref):
    h = _rms_norm(m_ref[0], g_ref[...]).astype(BF16)
    o_ref[0] = _dot(h, w_ref[...]).astype(BF16)


def _mem_kv(mem, g, w_xkv_bf):
    bsz, n_mem, d_model = mem.shape
    width = w_xkv_bf.shape[1]
    return pl.pallas_call(
        _mem_kv_kernel,
        grid=(bsz,),
        in_specs=[
            pl.BlockSpec((1, n_mem, d_model), lambda b: (b, 0, 0)),
            pl.BlockSpec((1, d_model), lambda b: (0, 0)),
            pl.BlockSpec(w_xkv_bf.shape, lambda b: (0, 0)),
        ],
        out_specs=pl.BlockSpec((1, n_mem, width), lambda b: (b, 0, 0)),
        out_shape=jax.ShapeDtypeStruct((bsz, n_mem, width), BF16),
        compiler_params=_params("parallel"),
        name="mem_kv",
    )(mem, g, w_xkv_bf)


def _xattn_kernel(x_ref, g_ref, wq_ref, kv_ref, wo_ref, o_ref):
    x = x_ref[...]
    h = _rms_norm(x, g_ref[...]).astype(BF16)
    q = (_dot(h, wq_ref[...]) * (XATTN_HEAD_DIM ** -0.5)).astype(BF16)
    heads = []
    for hh in range(N_XATTN_HEADS):
        lo = hh * XATTN_HEAD_DIM
        k = kv_ref[0, :, lo:lo + XATTN_HEAD_DIM]
        v = kv_ref[0, :, XATTN_WIDTH + lo:XATTN_WIDTH + lo + XATTN_HEAD_DIM]
        s = _dot_nt(q[:, lo:lo + XATTN_HEAD_DIM], k)
        p = jnp.exp(s - jnp.max(s, axis=-1, keepdims=True))
        o = _dot(p.astype(BF16), v) / jnp.sum(p, axis=-1, keepdims=True)
        heads.append(o.astype(BF16))
    o_ref[...] = x + _dot(jnp.concatenate(heads, axis=1), wo_ref[...])


def _cross_attention(x2d, g, w_xq_bf, kv, w_xo_bf, seq_len):
    n_tok, d_model = x2d.shape
    tm = ROW_TILE
    tiles_per_seq = seq_len // tm
    return pl.pallas_call(
        _xattn_kernel,
        grid=(n_tok // tm,),
        in_specs=[
            pl.BlockSpec((tm, d_model), lambda i: (i, 0)),
            pl.BlockSpec((1, d_model), lambda i: (0, 0)),
            pl.BlockSpec(w_xq_bf.shape, lambda i: (0, 0)),
            pl.BlockSpec((1,) + kv.shape[1:], lambda i: (i // tiles_per_seq, 0, 0)),
            pl.BlockSpec(w_xo_bf.shape, lambda i: (0, 0)),
        ],
        out_specs=pl.BlockSpec((tm, d_model), lambda i: (i, 0)),
        out_shape=jax.ShapeDtypeStruct((n_tok, d_model), F32),
        compiler_params=_params("parallel"),
        name="cross_attention",
    )(x2d, g, w_xq_bf, kv, w_xo_bf)


def _ffn_kernel(final_norm, x_ref, g_ref, wg_ref, wu_ref, wd_ref, gf_ref, o_ref, h_ref):
    j = pl.program_id(1)

    @pl.when(j == 0)
    def _():
        x = x_ref[...]
        h_ref[...] = _rms_norm(x, g_ref[...]).astype(BF16)
        o_ref[...] = x

    h = h_ref[...]
    half = wg_ref.shape[1] // 2
    gu = [(_dot(h, wg_ref[:, c:c + half]), _dot(h, wu_ref[:, c:c + half])) for c in (0, half)]
    down = [_dot((gate * jax.nn.sigmoid(gate) * up).astype(BF16), wd_ref[c:c + half, :])
            for c, (gate, up) in zip((0, half), gu)]
    o_ref[...] += down[0] + down[1]

    if final_norm:
        @pl.when(j == pl.num_programs(1) - 1)
        def _():
            o_ref[...] = _rms_norm(o_ref[...], gf_ref[...])


def _ffn(x2d, g, w_gate_bf, w_up_bf, w_down_bf, g_final, final_norm):
    n_tok, d_model = x2d.shape
    d_ff = w_gate_bf.shape[1]
    tm, tf = FFN_ROW_TILE, FF_TILE
    return pl.pallas_call(
        functools.partial(_ffn_kernel, final_norm),
        grid=(n_tok // tm, d_ff // tf),
        in_specs=[
            pl.BlockSpec((tm, d_model), lambda i, j: (i, 0)),
            pl.BlockSpec((1, d_model), lambda i, j: (0, 0)),
            pl.BlockSpec((d_model, tf), lambda i, j: (0, j)),
            pl.BlockSpec((d_model, tf), lambda i, j: (0, j)),
            pl.BlockSpec((tf, d_model), lambda i, j: (j, 0)),
            pl.BlockSpec((1, d_model), lambda i, j: (0, 0)),
        ],
        out_specs=pl.BlockSpec((tm, d_model), lambda i, j: (i, 0)),
        out_shape=jax.ShapeDtypeStruct((n_tok, d_model), F32),
        scratch_shapes=[pltpu.VMEM((tm, d_model), BF16)],
        compiler_params=_params("parallel", "arbitrary", vmem_limit=FFN_VMEM_LIMIT),
        name="ffn",
    )(x2d, g, w_gate_bf, w_up_bf, w_down_bf, g_final)


def kernel(x, mem, positions, g_mix, w_in, w_conv, b_conv, g_attn_out, g_conv_out, w_out, g_xattn, g_mem,
           w_xq, w_xkv, w_xo, g_ffn, w_gate, w_up, w_down, g_final):
    bsz, seq_len, d_model = x.shape
    depth = g_mix.shape[0]
    row = lambda a: a.reshape(1, -1)
    pos, inv_full, sign = _rope_inputs(positions)
    x2d = x.reshape(bsz * seq_len, d_model)
    for l in range(depth):
        qkv, y_conv = _in_proj(x2d, row(g_mix[l]), w_in[l].astype(BF16), pos, inv_full, sign, w_conv[l],
                               row(b_conv[l]), row(g_conv_out[l]), seq_len)
        o_attn, (w_out_bf, w_xq_bf, w_xkv_bf, w_xo_bf, w_gate_bf, w_up_bf, w_down_bf) = _moba_attention(
            qkv, row(g_attn_out[l]), bsz, seq_len,
            [w_out[l], w_xq[l], w_xkv[l], w_xo[l], w_gate[l], w_up[l], w_down[l]])
        x2d = _out_proj(o_attn, y_conv, w_out_bf, x2d)
        kv = _mem_kv(mem, row(g_mem[l]), w_xkv_bf)
        x2d = _cross_attention(x2d, row(g_xattn[l]), w_xq_bf, kv, w_xo_bf, seq_len)
        x2d = _ffn(x2d, row(g_ffn[l]), w_gate_bf, w_up_bf, w_down_bf, row(g_final), final_norm=l == depth - 1)
    return x2d.reshape(bsz, seq_len, d_model)
```

```python
import functools

import jax
import jax.numpy as jnp
from jax import lax
from jax.experimental import pallas as pl
from jax.experimental.pallas import tpu as pltpu

HEAD_DIM = 128
N_ATTN_HEADS = 8
ATTN_WIDTH = N_ATTN_HEADS * HEAD_DIM
CONV_KSIZE = 3
MOBA_BLOCK = 256
MOBA_TOPK = 3
ROPE_THETA = 10000.0
N_XATTN_HEADS = 4
XATTN_HEAD_DIM = 128
XATTN_WIDTH = N_XATTN_HEADS * XATTN_HEAD_DIM
EPS = 1e-6

LOG2_E = 1.4426950408889634
MASK_VALUE = -1e30
SUBLANES = 8
VMEM_LIMIT = 56 * 1024 * 1024

ROW_TILE = 512
FF_TILE = 512
CONV_COL_GROUP = 256
FFN_ROW_TILE = 1024
FFN_VMEM_LIMIT = 60 * 1024 * 1024
BF16_PACK = 16
MXU_DEPTH = 256

BF16 = jnp.bfloat16
F32 = jnp.float32


def _rms_norm(xf, g):
    y = xf * lax.rsqrt(jnp.mean(xf * xf, axis=-1, keepdims=True) + EPS)
    return y * g


def _dot(a, b):
    return jnp.dot(a, b, preferred_element_type=F32)


def _dot_nt(a, b):
    return lax.dot_general(a, b, (((1,), (1,)), ((), ())), preferred_element_type=F32)


def _dot_tn(a, b):
    return lax.dot_general(a, b, (((0,), (0,)), ((), ())), preferred_element_type=F32)


def _params(*semantics, vmem_limit=VMEM_LIMIT):
    return pltpu.CompilerParams(dimension_semantics=semantics, vmem_limit_bytes=vmem_limit)


def _rope_inputs(positions):
    half = HEAD_DIM // 2
    inv_freq = ROPE_THETA ** (-jnp.arange(0, HEAD_DIM, 2, dtype=F32) / HEAD_DIM)
    inv_full = jnp.concatenate([inv_freq, inv_freq]).reshape(1, HEAD_DIM)
    sign = jnp.concatenate([-jnp.ones((half,), F32), jnp.ones((half,), F32)]).reshape(1, HEAD_DIM)
    return positions.astype(F32).reshape(positions.size, 1), inv_full, sign


def _in_proj_kernel(tiles_per_seq, x_ref, g_ref, w_ref, pos_ref, inv_ref, sign_ref, wc_ref, bc_ref, gc_ref,
                    qkv_ref, conv_ref, uext_ref, cos_ref, sin_ref):
    tm = x_ref.shape[0]
    aw = qkv_ref.shape[1] // 3
    cw = conv_ref.shape[1]
    cg = CONV_COL_GROUP

    @pl.when(pl.program_id(0) % tiles_per_seq == 0)
    def _():
        uext_ref[0:SUBLANES, :] = jnp.zeros((SUBLANES, cw), F32)

    h = _rms_norm(x_ref[...], g_ref[...]).astype(BF16)

    def project_conv(c0):
        return tuple(_dot(h, w_ref[:, 3 * aw + part * cw + c0:3 * aw + part * cw + c0 + cg]) for part in range(3))

    def finish_conv(c0, c_gate, b_gate, x_in):
        cols = slice(c0, c0 + cg)
        u = c_gate * x_in
        uext_ref[SUBLANES:, cols] = u
        u1 = uext_ref[SUBLANES - 1:SUBLANES - 1 + tm, cols]
        u2 = uext_ref[SUBLANES - 2:SUBLANES - 2 + tm, cols]
        conv = wc_ref[0:1, cols] * u2 + wc_ref[1:2, cols] * u1 + wc_ref[2:3, cols] * u
        y = b_gate * (conv + bc_ref[:, cols])
        uext_ref[0:SUBLANES, cols] = u[tm - SUBLANES:, :]
        for lo in range(0, cg, HEAD_DIM):
            yg = _rms_norm(y[:, lo:lo + HEAD_DIM], gc_ref[:, c0 + lo:c0 + lo + HEAD_DIM])
            conv_ref[:, c0 + lo:c0 + lo + HEAD_DIM] = yg.astype(BF16)

    def project_attn(part):
        return (_dot(h, w_ref[:, part * aw:(part + 1) * aw]),)

    def finish_attn(part, r):
        scale = HEAD_DIM ** -0.5 * LOG2_E
        for lo in range(0, aw, HEAD_DIM):
            t = r[:, lo:lo + HEAD_DIM]
            if part < 2:
                t = t * cos_ref[...] + pltpu.roll(t, HEAD_DIM // 2, axis=1) * sin_ref[...]
            if part == 0:
                t = t * scale
            qkv_ref[:, part * aw + lo:part * aw + lo + HEAD_DIM] = t.astype(BF16)

    units = [(project_conv, finish_conv, c0) for c0 in range(0, cw, cg)]
    units += [(project_attn, finish_attn, part) for part in (0, 1, 2)]
    pending = units[0][0](units[0][2])
    ang = pos_ref[...] * inv_ref[...]
    cos_ref[...] = jnp.cos(ang)
    sin_ref[...] = jnp.sin(ang) * sign_ref[...]
    for n, (_, finish, arg) in enumerate(units):
        current = pending
        if n + 1 < len(units):
            pending = units[n + 1][0](units[n + 1][2])
        finish(arg, *current)


def _in_proj(x2d, g, w_in_bf, pos, inv_full, sign, w_conv, b_conv, g_conv, seq_len):
    n_tok, d_model = x2d.shape
    cw = w_conv.shape[1]
    aw = (w_in_bf.shape[1] - 3 * cw) // 3
    tm = ROW_TILE
    const = lambda i: (0, 0)
    row = lambda i: (i, 0)
    return pl.pallas_call(
        functools.partial(_in_proj_kernel, seq_len // tm),
        grid=(n_tok // tm,),
        in_specs=[
            pl.BlockSpec((tm, d_model), row),
            pl.BlockSpec((1, d_model), const),
            pl.BlockSpec(w_in_bf.shape, const, pipeline_mode=pl.Buffered(1)),
            pl.BlockSpec((tm, 1), row),
            pl.BlockSpec((1, HEAD_DIM), const),
            pl.BlockSpec((1, HEAD_DIM), const),
            pl.BlockSpec((CONV_KSIZE, cw), const),
            pl.BlockSpec((1, cw), const),
            pl.BlockSpec((1, cw), const),
        ],
        out_specs=[pl.BlockSpec((tm, 3 * aw), row), pl.BlockSpec((tm, cw), row)],
        out_shape=[jax.ShapeDtypeStruct((n_tok, 3 * aw), BF16), jax.ShapeDtypeStruct((n_tok, cw), BF16)],
        scratch_shapes=[pltpu.VMEM((tm + SUBLANES, cw), F32),
                        pltpu.VMEM((tm, HEAD_DIM), F32), pltpu.VMEM((tm, HEAD_DIM), F32)],
        compiler_params=_params("arbitrary"),
        name="in_proj",
    )(x2d, g, w_in_bf, pos, inv_full, sign, w_conv, b_conv, g_conv)


def _split_bf16(a, parts):
    out = []
    for _ in range(parts):
        piece = a.astype(BF16)
        out.append(piece)
        a = a - piece.astype(F32)
    return out


def _moba_kernel(n_cast, q_ref, k_ref, v_ref, g_ref, *refs):
    cast_in, o_ref, cast_out = refs[:n_cast], refs[n_cast], refs[n_cast + 1:2 * n_cast + 1]
    qt_ref, vt_ref, m_ref, acc_ref = refs[2 * n_cast + 1:]
    blk = MOBA_BLOCK
    hd = HEAD_DIM
    seq = k_ref.shape[0]
    n_blk = seq // blk

    for c in range(n_blk):
        rows = slice(c * blk, (c + 1) * blk)
        qt_ref[0:hd, rows] = q_ref[rows, :].astype(F32).T.astype(BF16)
        vt_ref[0:hd, rows] = v_ref[rows, :].astype(F32).T.astype(BF16)
    qt_ref[hd + n_blk:, :] = jnp.zeros((qt_ref.shape[0] - hd - n_blk, seq), BF16)
    vt_ref[hd:, :] = jnp.ones((vt_ref.shape[0] - hd, seq), BF16)

    kf = k_ref[...].astype(F32).reshape(n_blk, blk, hd)
    kmean = jnp.sum(kf, axis=1) * (1.0 / blk)
    kmean_parts = _split_bf16(kmean, 3)

    key_id = lax.broadcasted_iota(jnp.int32, (blk, blk), 0)
    qry_id = lax.broadcasted_iota(jnp.int32, (blk, blk), 1)
    causal = key_id <= qry_id
    lane_id = lax.broadcasted_iota(jnp.int32, (blk, hd), 1)

    def mask_own_tile(s):
        own = jnp.where(causal, s[:, :blk], MASK_VALUE)
        return own if s.shape[1] == blk else jnp.concatenate([own, s[:, blk:]], axis=1)

    def finish_tile(c0):
        o = (acc_ref[0:hd, c0:c0 + blk] / acc_ref[hd:hd + 1, c0:c0 + blk]).T
        o_ref[c0:c0 + blk, :] = _rms_norm(o, g_ref[...]).astype(BF16)

    r = _dot(jnp.concatenate([k_ref[0:blk, :]] + kmean_parts, axis=0), qt_ref[0:hd, :])
    for src, dst in zip(cast_in, cast_out):
        dst[...] = src[...].astype(BF16)
    gate = (r[blk:blk + n_blk] + r[blk + n_blk:blk + 2 * n_blk]) + r[blk + 2 * n_blk:blk + 3 * n_blk]
    blk_id = lax.broadcasted_iota(jnp.int32, (n_blk, seq), 0)
    q_tile = lax.broadcasted_iota(jnp.int32, (n_blk, seq), 1) // blk
    rank = jnp.zeros((n_blk, seq), jnp.int32)
    for j in range(n_blk):
        gj = gate[j:j + 1, :]
        beats = (gj > gate) | ((gj == gate) & (j < blk_id))
        rank = rank + jnp.where(beats & (j < q_tile), 1, 0)
    visible = ((blk_id < q_tile) & (rank < MOBA_TOPK)) | (blk_id == q_tile)
    bias = jnp.where(visible, 0.0, MASK_VALUE).astype(F32)
    qt_ref[hd:hd + n_blk, :] = bias.astype(BF16)

    s = mask_own_tile(r[0:blk] + bias[0:1, :])
    m_new = jnp.max(s, axis=0, keepdims=True)
    acc_ref[...] = _dot(vt_ref[:, 0:blk], jnp.exp2(s - m_new).astype(BF16))
    m_ref[...] = m_new
    finish_tile(0)

    def scores(j):
        c0 = j * blk
        k_aug = jnp.concatenate([k_ref[c0:c0 + blk, :], jnp.where(lane_id == j, 1.0, 0.0).astype(BF16)], axis=1)
        s = mask_own_tile(_dot(k_aug, qt_ref[:, c0:]))
        return s, jnp.max(s, axis=0, keepdims=True)

    pending = scores(1)
    for j in range(1, n_blk):
        c0 = j * blk
        s, s_max = pending
        if j + 1 < n_blk:
            pending = scores(j + 1)
        m_old = m_ref[:, c0:]
        m_new = jnp.maximum(m_old, s_max)
        alpha = jnp.exp2(m_old - m_new)
        p = jnp.exp2(s - m_new).astype(BF16)
        acc_ref[:, c0:] = alpha * acc_ref[:, c0:] + _dot(vt_ref[:, c0:c0 + blk], p)
        m_ref[:, c0:] = m_new
        finish_tile(c0)


def _moba_attention(qkv, g_attn, bsz, seq_len, later_weights):
    n_tok = qkv.shape[0]
    nh = N_ATTN_HEADS
    n_blk = seq_len // MOBA_BLOCK
    steps = bsz * nh
    slab = lambda b, h: (b * nh + h, 0)
    for w in later_weights:
        assert w.shape[0] % (steps * BF16_PACK) == 0, w.shape
    w_blocks = [(w.shape[0] // steps, w.shape[1]) for w in later_weights]
    outs = pl.pallas_call(
        functools.partial(_moba_kernel, len(later_weights)),
        grid=(bsz, nh),
        in_specs=[
            pl.BlockSpec((seq_len, HEAD_DIM), lambda b, h: (b, h)),
            pl.BlockSpec((seq_len, HEAD_DIM), lambda b, h: (b, nh + h)),
            pl.BlockSpec((seq_len, HEAD_DIM), lambda b, h: (b, 2 * nh + h)),
            pl.BlockSpec((1, HEAD_DIM), lambda b, h: (0, h)),
        ] + [pl.BlockSpec(blk, slab) for blk in w_blocks],
        out_specs=[pl.BlockSpec((seq_len, HEAD_DIM), lambda b, h: (b, h))]
        + [pl.BlockSpec(blk, slab) for blk in w_blocks],
        out_shape=[jax.ShapeDtypeStruct((n_tok, nh * HEAD_DIM), BF16)]
        + [jax.ShapeDtypeStruct(w.shape, BF16) for w in later_weights],
        scratch_shapes=[
            pltpu.VMEM((MXU_DEPTH, seq_len), BF16),
            pltpu.VMEM((HEAD_DIM + BF16_PACK, seq_len), BF16),
            pltpu.VMEM((1, seq_len), F32),
            pltpu.VMEM((HEAD_DIM + BF16_PACK, seq_len), F32),
        ],
        compiler_params=_params("parallel", "parallel"),
        name="moba_attention",
    )(qkv, qkv, qkv, g_attn, *later_weights)
    return outs[0], outs[1:]


def _mem_kv_kernel(m_ref, g_ref, w_ref, o_ref):
    h = _rms_norm(m_ref[0], g_ref[...]).astype(BF16)
    o_ref[0] = _dot(h, w_ref[...]).astype(BF16)


def _mem_kv(mem, g, w_xkv_bf):
    bsz, n_mem, d_model = mem.shape
    width = w_xkv_bf.shape[1]
    return pl.pallas_call(
        _mem_kv_kernel,
        grid=(bsz,),
        in_specs=[
            pl.BlockSpec((1, n_mem, d_model), lambda b: (b, 0, 0)),
            pl.BlockSpec((1, d_model), lambda b: (0, 0)),
            pl.BlockSpec(w_xkv_bf.shape, lambda b: (0, 0)),
        ],
        out_specs=pl.BlockSpec((1, n_mem, width), lambda b: (b, 0, 0)),
        out_shape=jax.ShapeDtypeStruct((bsz, n_mem, width), BF16),
        compiler_params=_params("parallel"),
        name="mem_kv",
    )(mem, g, w_xkv_bf)


def _mix_kernel(a_ref, c_ref, wout_ref, x_ref, g_ref, wq_ref, kv_ref, wo_ref, o_ref):
    half = x_ref.shape[0] // 2
    aw = a_ref.shape[1]
    halves = (slice(0, half), slice(half, 2 * half))

    def out_proj(rows):
        y = _dot(a_ref[rows, :], wout_ref[0:aw, :]) + _dot(c_ref[rows, :], wout_ref[aw:, :])
        return x_ref[rows, :] + y

    def queries(x1):
        h = _rms_norm(x1, g_ref[...]).astype(BF16)
        return (_dot(h, wq_ref[...]) * (XATTN_HEAD_DIM ** -0.5)).astype(BF16)

    def attend(q):
        heads = []
        for hh in range(N_XATTN_HEADS):
            lo = hh * XATTN_HEAD_DIM
            k = kv_ref[0, :, lo:lo + XATTN_HEAD_DIM]
            v = kv_ref[0, :, XATTN_WIDTH + lo:XATTN_WIDTH + lo + XATTN_HEAD_DIM]
            s = _dot_nt(q[:, lo:lo + XATTN_HEAD_DIM], k)
            p = jnp.exp(s - jnp.max(s, axis=-1, keepdims=True))
            o = _dot(p.astype(BF16), v) / jnp.sum(p, axis=-1, keepdims=True)
            heads.append(o.astype(BF16))
        return jnp.concatenate(heads, axis=1)

    x1 = [out_proj(rows) for rows in halves]
    q = [queries(t) for t in x1]
    for rows, t, qh in zip(halves, x1, q):
        o_ref[rows, :] = t + _dot(attend(qh), wo_ref[...])


def _mix(o_attn, y_conv, w_out_bf, x2d, g, w_xq_bf, kv, w_xo_bf, seq_len):
    n_tok, d_model = x2d.shape
    tm = ROW_TILE
    tiles_per_seq = seq_len // tm
    const = lambda i: (0, 0)
    row = lambda i: (i, 0)
    resident = pl.Buffered(1)
    return pl.pallas_call(
        _mix_kernel,
        grid=(n_tok // tm,),
        in_specs=[
            pl.BlockSpec((tm, o_attn.shape[1]), row),
            pl.BlockSpec((tm, y_conv.shape[1]), row),
            pl.BlockSpec(w_out_bf.shape, const, pipeline_mode=resident),
            pl.BlockSpec((tm, d_model), row),
            pl.BlockSpec((1, d_model), const),
            pl.BlockSpec(w_xq_bf.shape, const, pipeline_mode=resident),
            pl.BlockSpec((1,) + kv.shape[1:], lambda i: (i // tiles_per_seq, 0, 0)),
            pl.BlockSpec(w_xo_bf.shape, const, pipeline_mode=resident),
        ],
        out_specs=pl.BlockSpec((tm, d_model), row),
        out_shape=jax.ShapeDtypeStruct((n_tok, d_model), F32),
        compiler_params=_params("parallel"),
        name="mix_out_xattn",
    )(o_attn, y_conv, w_out_bf, x2d, g, w_xq_bf, kv, w_xo_bf)


def _ffn_kernel(final_norm, x_ref, g_ref, wg_ref, wu_ref, wd_ref, gf_ref, o_ref, h_ref):
    j = pl.program_id(1)

    @pl.when(j == 0)
    def _():
        x = x_ref[...]
        h_ref[...] = _rms_norm(x, g_ref[...]).astype(BF16)
        o_ref[...] = x

    h = h_ref[...]
    half = wg_ref.shape[1] // 2
    gu = [(_dot(h, wg_ref[:, c:c + half]), _dot(h, wu_ref[:, c:c + half])) for c in (0, half)]
    down = [_dot((gate * jax.nn.sigmoid(gate) * up).astype(BF16), wd_ref[c:c + half, :])
            for c, (gate, up) in zip((0, half), gu)]
    o_ref[...] += down[0] + down[1]

    if final_norm:
        @pl.when(j == pl.num_programs(1) - 1)
        def _():
            o_ref[...] = _rms_norm(o_ref[...], gf_ref[...])


def _ffn(x2d, g, w_gate_bf, w_up_bf, w_down_bf, g_final, final_norm):
    n_tok, d_model = x2d.shape
    d_ff = w_gate_bf.shape[1]
    tm, tf = FFN_ROW_TILE, FF_TILE
    return pl.pallas_call(
        functools.partial(_ffn_kernel, final_norm),
        grid=(n_tok // tm, d_ff // tf),
        in_specs=[
            pl.BlockSpec((tm, d_model), lambda i, j: (i, 0)),
            pl.BlockSpec((1, d_model), lambda i, j: (0, 0)),
            pl.BlockSpec((d_model, tf), lambda i, j: (0, j)),
            pl.BlockSpec((d_model, tf), lambda i, j: (0, j)),
            pl.BlockSpec((tf, d_model), lambda i, j: (j, 0)),
            pl.BlockSpec((1, d_model), lambda i, j: (0, 0)),
        ],
        out_specs=pl.BlockSpec((tm, d_model), lambda i, j: (i, 0)),
        out_shape=jax.ShapeDtypeStruct((n_tok, d_model), F32),
        scratch_shapes=[pltpu.VMEM((tm, d_model), BF16)],
        compiler_params=_params("parallel", "arbitrary", vmem_limit=FFN_VMEM_LIMIT),
        name="ffn",
    )(x2d, g, w_gate_bf, w_up_bf, w_down_bf, g_final)


def kernel(x, mem, positions, g_mix, w_in, w_conv, b_conv, g_attn_out, g_conv_out, w_out, g_xattn, g_mem,
           w_xq, w_xkv, w_xo, g_ffn, w_gate, w_up, w_down, g_final):
    bsz, seq_len, d_model = x.shape
    depth = g_mix.shape[0]
    row = lambda a: a.reshape(1, -1)
    pos, inv_full, sign = _rope_inputs(positions)
    x2d = x.reshape(bsz * seq_len, d_model)
    for l in range(depth):
        qkv, y_conv = _in_proj(x2d, row(g_mix[l]), w_in[l].astype(BF16), pos, inv_full, sign, w_conv[l],
                               row(b_conv[l]), row(g_conv_out[l]), seq_len)
        o_attn, (w_out_bf, w_xq_bf, w_xkv_bf, w_xo_bf, w_gate_bf, w_up_bf, w_down_bf) = _moba_attention(
            qkv, row(g_attn_out[l]), bsz, seq_len,
            [w_out[l], w_xq[l], w_xkv[l], w_xo[l], w_gate[l], w_up[l], w_down[l]])
        kv = _mem_kv(mem, row(g_mem[l]), w_xkv_bf)
        x2d = _mix(o_attn, y_conv, w_out_bf, x2d, row(g_xattn[l]), w_xq_bf, kv, w_xo_bf, seq_len)
        x2d = _ffn(x2d, row(g_ffn[l]), w_gate_bf, w_up_bf, w_down_bf, row(g_final), final_norm=l == depth - 1)
    return x2d.reshape(bsz, seq_len, d_model)
```

```python
import functools

import jax
import jax.numpy as jnp
from jax import lax
from jax.experimental import pallas as pl
from jax.experimental.pallas import tpu as pltpu

HEAD_DIM = 128
N_ATTN_HEADS = 8
CONV_KSIZE = 3
MOBA_BLOCK = 256
MOBA_TOPK = 3
ROPE_THETA = 10000.0
N_XATTN_HEADS = 4
XATTN_HEAD_DIM = 128
XATTN_WIDTH = N_XATTN_HEADS * XATTN_HEAD_DIM
EPS = 1e-6

LOG2_E = 1.4426950408889634
MASK_VALUE = -1e30
SUBLANES = 8
VMEM_LIMIT = 56 * 1024 * 1024

ROW_TILE = 512
FF_TILE = 512
CONV_COL_GROUP = 256
FFN_ROW_TILE = 1024
FFN_VMEM_LIMIT = 60 * 1024 * 1024
BF16_PACK = 16
MXU_DEPTH = 256

BF16 = jnp.bfloat16
F32 = jnp.float32


def _rms_norm(xf, g):
    y = xf * lax.rsqrt(jnp.mean(xf * xf, axis=-1, keepdims=True) + EPS)
    return y * g


def _dot(a, b):
    return jnp.dot(a, b, preferred_element_type=F32)


def _dot_nt(a, b):
    return lax.dot_general(a, b, (((1,), (1,)), ((), ())), preferred_element_type=F32)


def _params(*semantics, vmem_limit=VMEM_LIMIT):
    return pltpu.CompilerParams(dimension_semantics=semantics, vmem_limit_bytes=vmem_limit)


def _rope_inputs(positions):
    half = HEAD_DIM // 2
    inv_freq = ROPE_THETA ** (-jnp.arange(0, HEAD_DIM, 2, dtype=F32) / HEAD_DIM)
    inv_full = jnp.concatenate([inv_freq, inv_freq]).reshape(1, HEAD_DIM)
    sign = jnp.concatenate([-jnp.ones((half,), F32), jnp.ones((half,), F32)]).reshape(1, HEAD_DIM)
    return positions.astype(F32).reshape(positions.size, 1), inv_full, sign


def _in_proj_kernel(tiles_per_seq, x_ref, g_ref, w_ref, pos_ref, inv_ref, sign_ref, wc_ref, bc_ref, gc_ref,
                    qkv_ref, conv_ref, uext_ref, cos_ref, sin_ref):
    tm = x_ref.shape[0]
    aw = qkv_ref.shape[1] // 3
    cw = conv_ref.shape[1]
    cg = CONV_COL_GROUP

    @pl.when(pl.program_id(0) % tiles_per_seq == 0)
    def _():
        uext_ref[0:SUBLANES, :] = jnp.zeros((SUBLANES, cw), F32)

    h = _rms_norm(x_ref[...], g_ref[...]).astype(BF16)

    def project_conv(c0):
        return tuple(_dot(h, w_ref[:, 3 * aw + part * cw + c0:3 * aw + part * cw + c0 + cg]) for part in range(3))

    def finish_conv(c0, c_gate, b_gate, x_in):
        cols = slice(c0, c0 + cg)
        u = c_gate * x_in
        uext_ref[SUBLANES:, cols] = u
        u1 = uext_ref[SUBLANES - 1:SUBLANES - 1 + tm, cols]
        u2 = uext_ref[SUBLANES - 2:SUBLANES - 2 + tm, cols]
        conv = wc_ref[0:1, cols] * u2 + wc_ref[1:2, cols] * u1 + wc_ref[2:3, cols] * u
        y = b_gate * (conv + bc_ref[:, cols])
        uext_ref[0:SUBLANES, cols] = u[tm - SUBLANES:, :]
        for lo in range(0, cg, HEAD_DIM):
            yg = _rms_norm(y[:, lo:lo + HEAD_DIM], gc_ref[:, c0 + lo:c0 + lo + HEAD_DIM])
            conv_ref[:, c0 + lo:c0 + lo + HEAD_DIM] = yg.astype(BF16)

    def project_attn(part):
        return (_dot(h, w_ref[:, part * aw:(part + 1) * aw]),)

    def finish_attn(part, r):
        scale = HEAD_DIM ** -0.5 * LOG2_E
        for lo in range(0, aw, HEAD_DIM):
            t = r[:, lo:lo + HEAD_DIM]
            if part < 2:
                t = t * cos_ref[...] + pltpu.roll(t, HEAD_DIM // 2, axis=1) * sin_ref[...]
            if part == 0:
                t = t * scale
            qkv_ref[:, part * aw + lo:part * aw + lo + HEAD_DIM] = t.astype(BF16)

    units = [(project_conv, finish_conv, c0) for c0 in range(0, cw, cg)]
    units += [(project_attn, finish_attn, part) for part in (0, 1, 2)]
    pending = units[0][0](units[0][2])
    ang = pos_ref[...] * inv_ref[...]
    cos_ref[...] = jnp.cos(ang)
    sin_ref[...] = jnp.sin(ang) * sign_ref[...]
    for n, (_, finish, arg) in enumerate(units):
        current = pending
        if n + 1 < len(units):
            pending = units[n + 1][0](units[n + 1][2])
        finish(arg, *current)


def _in_proj(x2d, g, w_in_bf, pos, inv_full, sign, w_conv, b_conv, g_conv, seq_len):
    n_tok, d_model = x2d.shape
    cw = w_conv.shape[1]
    aw = (w_in_bf.shape[1] - 3 * cw) // 3
    tm = ROW_TILE
    const = lambda i: (0, 0)
    row = lambda i: (i, 0)
    return pl.pallas_call(
        functools.partial(_in_proj_kernel, seq_len // tm),
        grid=(n_tok // tm,),
        in_specs=[
            pl.BlockSpec((tm, d_model), row),
            pl.BlockSpec((1, d_model), const),
            pl.BlockSpec(w_in_bf.shape, const, pipeline_mode=pl.Buffered(1)),
            pl.BlockSpec((tm, 1), row),
            pl.BlockSpec((1, HEAD_DIM), const),
            pl.BlockSpec((1, HEAD_DIM), const),
            pl.BlockSpec((CONV_KSIZE, cw), const),
            pl.BlockSpec((1, cw), const),
            pl.BlockSpec((1, cw), const),
        ],
        out_specs=[pl.BlockSpec((tm, 3 * aw), row), pl.BlockSpec((tm, cw), row)],
        out_shape=[jax.ShapeDtypeStruct((n_tok, 3 * aw), BF16), jax.ShapeDtypeStruct((n_tok, cw), BF16)],
        scratch_shapes=[pltpu.VMEM((tm + SUBLANES, cw), F32),
                        pltpu.VMEM((tm, HEAD_DIM), F32), pltpu.VMEM((tm, HEAD_DIM), F32)],
        compiler_params=_params("arbitrary"),
        name="in_proj",
    )(x2d, g, w_in_bf, pos, inv_full, sign, w_conv, b_conv, g_conv)


def _split_bf16(a, parts):
    out = []
    for _ in range(parts):
        piece = a.astype(BF16)
        out.append(piece)
        a = a - piece.astype(F32)
    return out


def _moba_kernel(n_cast, q_ref, k_ref, v_ref, g_ref, *refs):
    cast_in, o_ref, cast_out = refs[:n_cast], refs[n_cast], refs[n_cast + 1:2 * n_cast + 1]
    qt_ref, vt_ref, m_ref, acc_ref = refs[2 * n_cast + 1:]
    blk = MOBA_BLOCK
    hd = HEAD_DIM
    seq = k_ref.shape[0]
    n_blk = seq // blk

    for c in range(n_blk):
        rows = slice(c * blk, (c + 1) * blk)
        qt_ref[0:hd, rows] = q_ref[rows, :].astype(F32).T.astype(BF16)
        vt_ref[0:hd, rows] = v_ref[rows, :].astype(F32).T.astype(BF16)
    qt_ref[hd + n_blk:, :] = jnp.zeros((qt_ref.shape[0] - hd - n_blk, seq), BF16)
    vt_ref[hd:, :] = jnp.ones((vt_ref.shape[0] - hd, seq), BF16)

    kf = k_ref[...].astype(F32).reshape(n_blk, blk, hd)
    kmean = jnp.sum(kf, axis=1) * (1.0 / blk)
    kmean_parts = _split_bf16(kmean, 3)

    key_id = lax.broadcasted_iota(jnp.int32, (blk, blk), 0)
    qry_id = lax.broadcasted_iota(jnp.int32, (blk, blk), 1)
    causal = key_id <= qry_id
    lane_id = lax.broadcasted_iota(jnp.int32, (blk, hd), 1)

    def mask_own_tile(s):
        own = jnp.where(causal, s[:, :blk], MASK_VALUE)
        return own if s.shape[1] == blk else jnp.concatenate([own, s[:, blk:]], axis=1)

    def finish_tile(c0):
        o = (acc_ref[0:hd, c0:c0 + blk] / acc_ref[hd:hd + 1, c0:c0 + blk]).T
        o_ref[c0:c0 + blk, :] = _rms_norm(o, g_ref[...]).astype(BF16)

    r = _dot(jnp.concatenate([k_ref[0:blk, :]] + kmean_parts, axis=0), qt_ref[0:hd, :])
    for src, dst in zip(cast_in, cast_out):
        dst[...] = src[...].astype(BF16)
    gate = (r[blk:blk + n_blk] + r[blk + n_blk:blk + 2 * n_blk]) + r[blk + 2 * n_blk:blk + 3 * n_blk]
    blk_id = lax.broadcasted_iota(jnp.int32, (n_blk, seq), 0)
    q_tile = lax.broadcasted_iota(jnp.int32, (n_blk, seq), 1) // blk
    rank = jnp.zeros((n_blk, seq), jnp.int32)
    for j in range(n_blk):
        gj = gate[j:j + 1, :]
        beats = (gj > gate) | ((gj == gate) & (j < blk_id))
        rank = rank + jnp.where(beats & (j < q_tile), 1, 0)
    visible = ((blk_id < q_tile) & (rank < MOBA_TOPK)) | (blk_id == q_tile)
    bias = jnp.where(visible, 0.0, MASK_VALUE).astype(F32)
    qt_ref[hd:hd + n_blk, :] = bias.astype(BF16)

    s = mask_own_tile(r[0:blk] + bias[0:1, :])
    m_new = jnp.max(s, axis=0, keepdims=True)
    acc_ref[...] = _dot(vt_ref[:, 0:blk], jnp.exp2(s - m_new).astype(BF16))
    m_ref[...] = m_new
    finish_tile(0)

    def scores(j):
        c0 = j * blk
        k_aug = jnp.concatenate([k_ref[c0:c0 + blk, :], jnp.where(lane_id == j, 1.0, 0.0).astype(BF16)], axis=1)
        s = mask_own_tile(_dot(k_aug, qt_ref[:, c0:]))
        return s, jnp.max(s, axis=0, keepdims=True)

    pending = scores(1)
    for j in range(1, n_blk):
        c0 = j * blk
        s, s_max = pending
        if j + 1 < n_blk:
            pending = scores(j + 1)
        m_old = m_ref[:, c0:]
        m_new = jnp.maximum(m_old, s_max)
        alpha = jnp.exp2(m_old - m_new)
        p = jnp.exp2(s - m_new).astype(BF16)
        acc_ref[:, c0:] = alpha * acc_ref[:, c0:] + _dot(vt_ref[:, c0:c0 + blk], p)
        m_ref[:, c0:] = m_new
        finish_tile(c0)


def _moba_attention(qkv, g_attn, bsz, seq_len, later_weights):
    n_tok = qkv.shape[0]
    nh = N_ATTN_HEADS
    n_blk = seq_len // MOBA_BLOCK
    steps = bsz * nh
    slab = lambda b, h: (b * nh + h, 0)
    for w in later_weights:
        assert w.shape[0] % (steps * BF16_PACK) == 0, w.shape
    w_blocks = [(w.shape[0] // steps, w.shape[1]) for w in later_weights]
    outs = pl.pallas_call(
        functools.partial(_moba_kernel, len(later_weights)),
        grid=(bsz, nh),
        in_specs=[
            pl.BlockSpec((seq_len, HEAD_DIM), lambda b, h: (b, h)),
            pl.BlockSpec((seq_len, HEAD_DIM), lambda b, h: (b, nh + h)),
            pl.BlockSpec((seq_len, HEAD_DIM), lambda b, h: (b, 2 * nh + h)),
            pl.BlockSpec((1, HEAD_DIM), lambda b, h: (0, h)),
        ] + [pl.BlockSpec(blk, slab) for blk in w_blocks],
        out_specs=[pl.BlockSpec((seq_len, HEAD_DIM), lambda b, h: (b, h))]
        + [pl.BlockSpec(blk, slab) for blk in w_blocks],
        out_shape=[jax.ShapeDtypeStruct((n_tok, nh * HEAD_DIM), BF16)]
        + [jax.ShapeDtypeStruct(w.shape, BF16) for w in later_weights],
        scratch_shapes=[
            pltpu.VMEM((MXU_DEPTH, seq_len), BF16),
            pltpu.VMEM((HEAD_DIM + BF16_PACK, seq_len), BF16),
            pltpu.VMEM((1, seq_len), F32),
            pltpu.VMEM((HEAD_DIM + BF16_PACK, seq_len), F32),
        ],
        compiler_params=_params("parallel", "parallel"),
        name="moba_attention",
    )(qkv, qkv, qkv, g_attn, *later_weights)
    return outs[0], outs[1:]


def _mem_kv_kernel(m_ref, g_ref, w_ref, o_ref):
    h = _rms_norm(m_ref[0], g_ref[...]).astype(BF16)
    o_ref[0] = _dot(h, w_ref[...]).astype(BF16)


def _mem_kv(mem, g, w_xkv_bf):
    bsz, n_mem, d_model = mem.shape
    width = w_xkv_bf.shape[1]
    return pl.pallas_call(
        _mem_kv_kernel,
        grid=(bsz,),
        in_specs=[
            pl.BlockSpec((1, n_mem, d_model), lambda b: (b, 0, 0)),
            pl.BlockSpec((1, d_model), lambda b: (0, 0)),
            pl.BlockSpec(w_xkv_bf.shape, lambda b: (0, 0)),
        ],
        out_specs=pl.BlockSpec((1, n_mem, width), lambda b: (b, 0, 0)),
        out_shape=jax.ShapeDtypeStruct((bsz, n_mem, width), BF16),
        compiler_params=_params("parallel"),
        name="mem_kv",
    )(mem, g, w_xkv_bf)


def _mix_kernel(a_ref, c_ref, wout_ref, x_ref, g_ref, wq_ref, kv_ref, wo_ref, o_ref):
    half = x_ref.shape[0] // 2
    aw = a_ref.shape[1]
    halves = (slice(0, half), slice(half, 2 * half))

    def out_proj(rows):
        y = _dot(a_ref[rows, :], wout_ref[0:aw, :]) + _dot(c_ref[rows, :], wout_ref[aw:, :])
        return x_ref[rows, :] + y

    def queries(x1):
        h = _rms_norm(x1, g_ref[...]).astype(BF16)
        return (_dot(h, wq_ref[...]) * (XATTN_HEAD_DIM ** -0.5)).astype(BF16)

    def attend(q):
        heads = []
        for hh in range(N_XATTN_HEADS):
            lo = hh * XATTN_HEAD_DIM
            k = kv_ref[0, :, lo:lo + XATTN_HEAD_DIM]
            v = kv_ref[0, :, XATTN_WIDTH + lo:XATTN_WIDTH + lo + XATTN_HEAD_DIM]
            s = _dot_nt(q[:, lo:lo + XATTN_HEAD_DIM], k)
            p = jnp.exp(s - jnp.max(s, axis=-1, keepdims=True))
            o = _dot(p.astype(BF16), v) / jnp.sum(p, axis=-1, keepdims=True)
            heads.append(o.astype(BF16))
        return jnp.concatenate(heads, axis=1)

    x1 = [out_proj(rows) for rows in halves]
    q = [queries(t) for t in x1]
    for rows, t, qh in zip(halves, x1, q):
        o_ref[rows, :] = t + _dot(attend(qh), wo_ref[...])


def _mix(o_attn, y_conv, w_out_bf, x2d, g, w_xq_bf, kv, w_xo_bf, seq_len):
    n_tok, d_model = x2d.shape
    tm = ROW_TILE
    tiles_per_seq = seq_len // tm
    const = lambda i: (0, 0)
    row = lambda i: (i, 0)
    resident = pl.Buffered(1)
    return pl.pallas_call(
        _mix_kernel,
        grid=(n_tok // tm,),
        in_specs=[
            pl.BlockSpec((tm, o_attn.shape[1]), row),
            pl.BlockSpec((tm, y_conv.shape[1]), row),
            pl.BlockSpec(w_out_bf.shape, const, pipeline_mode=resident),
            pl.BlockSpec((tm, d_model), row),
            pl.BlockSpec((1, d_model), const),
            pl.BlockSpec(w_xq_bf.shape, const, pipeline_mode=resident),
            pl.BlockSpec((1,) + kv.shape[1:], lambda i: (i // tiles_per_seq, 0, 0)),
            pl.BlockSpec(w_xo_bf.shape, const, pipeline_mode=resident),
        ],
        out_specs=pl.BlockSpec((tm, d_model), row),
        out_shape=jax.ShapeDtypeStruct((n_tok, d_model), F32),
        compiler_params=_params("parallel"),
        name="mix_out_xattn",
    )(o_attn, y_conv, w_out_bf, x2d, g, w_xq_bf, kv, w_xo_bf)


def _ffn_kernel(final_norm, x_ref, g_ref, wg_ref, wu_ref, wd_ref, gf_ref, o_ref, h_ref):
    j = pl.program_id(1)

    @pl.when(j == 0)
    def _():
        x = x_ref[...]
        h_ref[...] = _rms_norm(x, g_ref[...]).astype(BF16)
        o_ref[...] = x

    h = h_ref[...]
    half = wg_ref.shape[1] // 2
    gu = [(_dot(h, wg_ref[:, c:c + half]), _dot(h, wu_ref[:, c:c + half])) for c in (0, half)]
    down = [_dot((gate * jax.nn.sigmoid(gate) * up).astype(BF16), wd_ref[c:c + half, :])
            for c, (gate, up) in zip((0, half), gu)]
    o_ref[...] += down[0] + down[1]

    if final_norm:
        @pl.when(j == pl.num_programs(1) - 1)
        def _():
            o_ref[...] = _rms_norm(o_ref[...], gf_ref[...])


def _ffn(x2d, g, w_gate_bf, w_up_bf, w_down_bf, g_final, final_norm):
    n_tok, d_model = x2d.shape
    d_ff = w_gate_bf.shape[1]
    tm, tf = FFN_ROW_TILE, FF_TILE
    return pl.pallas_call(
        functools.partial(_ffn_kernel, final_norm),
        grid=(n_tok // tm, d_ff // tf),
        in_specs=[
            pl.BlockSpec((tm, d_model), lambda i, j: (i, 0)),
            pl.BlockSpec((1, d_model), lambda i, j: (0, 0)),
            pl.BlockSpec((d_model, tf), lambda i, j: (0, j)),
            pl.BlockSpec((d_model, tf), lambda i, j: (0, j)),
            pl.BlockSpec((tf, d_model), lambda i, j: (j, 0)),
            pl.BlockSpec((1, d_model), lambda i, j: (0, 0)),
        ],
        out_specs=pl.BlockSpec((tm, d_model), lambda i, j: (i, 0)),
        out_shape=jax.ShapeDtypeStruct((n_tok, d_model), F32),
        scratch_shapes=[pltpu.VMEM((tm, d_model), BF16)],
        compiler_params=_params("parallel", "arbitrary", vmem_limit=FFN_VMEM_LIMIT),
        name="ffn",
    )(x2d, g, w_gate_bf, w_up_bf, w_down_bf, g_final)


def kernel(x, mem, positions, g_mix, w_in, w_conv, b_conv, g_attn_out, g_conv_out, w_out, g_xattn, g_mem,
           w_xq, w_xkv, w_xo, g_ffn, w_gate, w_up, w_down, g_final):
    bsz, seq_len, d_model = x.shape
    depth = g_mix.shape[0]
    row = lambda a: a.reshape(1, -1)
    pos, inv_full, sign = _rope_inputs(positions)
    x2d = x.reshape(bsz * seq_len, d_model)
    for l in range(depth):
        qkv, y_conv = _in_proj(x2d, row(g_mix[l]), w_in[l].astype(BF16), pos, inv_full, sign, w_conv[l],
                               row(b_conv[l]), row(g_conv_out[l]), seq_len)
        o_attn, (w_out_bf, w_xq_bf, w_xkv_bf, w_xo_bf, w_gate_bf, w_up_bf, w_down_bf) = _moba_attention(
            qkv, row(g_attn_out[l]), bsz, seq_len,
            [w_out[l], w_xq[l], w_xkv[l], w_xo[l], w_gate[l], w_up[l], w_down[l]])
        kv = _mem_kv(mem, row(g_mem[l]), w_xkv_bf)
        x2d = _mix(o_attn, y_conv, w_out_bf, x2d, row(g_xattn[l]), w_xq_bf, kv, w_xo_bf, seq_len)
        x2d = _ffn(x2d, row(g_ffn[l]), w_gate_bf, w_up_bf, w_down_bf, row(g_final), final_norm=l == depth - 1)
    return x2d.reshape(bsz, seq_len, d_model)
```

```python
import functools

import jax
import jax.numpy as jnp
from jax import lax
from jax.experimental import pallas as pl
from jax.experimental.pallas import tpu as pltpu

HEAD_DIM = 128
N_ATTN_HEADS = 8
CONV_KSIZE = 3
MOBA_BLOCK = 256
MOBA_TOPK = 3
ROPE_THETA = 10000.0
N_XATTN_HEADS = 4
XATTN_HEAD_DIM = 128
XATTN_WIDTH = N_XATTN_HEADS * XATTN_HEAD_DIM
EPS = 1e-6

LOG2_E = 1.4426950408889634
MASK_VALUE = -1e30
SUBLANES = 8
VMEM_LIMIT = 56 * 1024 * 1024

ROW_TILE = 512
FF_TILE = 512
CONV_COL_GROUP = 256
FFN_ROW_TILE = 1024
FFN_VMEM_LIMIT = 60 * 1024 * 1024
MOBA_HEADS_PER_STEP = 2
BF16_PACK = 16
MXU_DEPTH = 256

BF16 = jnp.bfloat16
F32 = jnp.float32


def _rms_norm(xf, g):
    y = xf * lax.rsqrt(jnp.mean(xf * xf, axis=-1, keepdims=True) + EPS)
    return y * g


def _dot(a, b):
    return jnp.dot(a, b, preferred_element_type=F32)


def _dot_nt(a, b):
    return lax.dot_general(a, b, (((1,), (1,)), ((), ())), preferred_element_type=F32)


def _params(*semantics, vmem_limit=VMEM_LIMIT):
    return pltpu.CompilerParams(dimension_semantics=semantics, vmem_limit_bytes=vmem_limit)


def _rope_inputs(positions):
    half = HEAD_DIM // 2
    inv_freq = ROPE_THETA ** (-jnp.arange(0, HEAD_DIM, 2, dtype=F32) / HEAD_DIM)
    inv_full = jnp.concatenate([inv_freq, inv_freq]).reshape(1, HEAD_DIM)
    sign = jnp.concatenate([-jnp.ones((half,), F32), jnp.ones((half,), F32)]).reshape(1, HEAD_DIM)
    return positions.astype(F32).reshape(positions.size, 1), inv_full, sign


def _in_proj_kernel(tiles_per_seq, x_ref, g_ref, w_ref, pos_ref, inv_ref, sign_ref, wc_ref, bc_ref, gc_ref,
                    qkv_ref, conv_ref, uext_ref, cos_ref, sin_ref):
    tm = x_ref.shape[0]
    aw = qkv_ref.shape[1] // 3
    cw = conv_ref.shape[1]
    cg = CONV_COL_GROUP

    @pl.when(pl.program_id(0) % tiles_per_seq == 0)
    def _():
        uext_ref[0:SUBLANES, :] = jnp.zeros((SUBLANES, cw), F32)

    h = _rms_norm(x_ref[...], g_ref[...]).astype(BF16)

    def project_conv(c0):
        return tuple(_dot(h, w_ref[:, 3 * aw + part * cw + c0:3 * aw + part * cw + c0 + cg]) for part in range(3))

    def finish_conv(c0, c_gate, b_gate, x_in):
        cols = slice(c0, c0 + cg)
        u = c_gate * x_in
        uext_ref[SUBLANES:, cols] = u
        u1 = uext_ref[SUBLANES - 1:SUBLANES - 1 + tm, cols]
        u2 = uext_ref[SUBLANES - 2:SUBLANES - 2 + tm, cols]
        conv = wc_ref[0:1, cols] * u2 + wc_ref[1:2, cols] * u1 + wc_ref[2:3, cols] * u
        y = b_gate * (conv + bc_ref[:, cols])
        uext_ref[0:SUBLANES, cols] = u[tm - SUBLANES:, :]
        for lo in range(0, cg, HEAD_DIM):
            yg = _rms_norm(y[:, lo:lo + HEAD_DIM], gc_ref[:, c0 + lo:c0 + lo + HEAD_DIM])
            conv_ref[:, c0 + lo:c0 + lo + HEAD_DIM] = yg.astype(BF16)

    def project_attn(part):
        return (_dot(h, w_ref[:, part * aw:(part + 1) * aw]),)

    def finish_attn(part, r):
        scale = HEAD_DIM ** -0.5 * LOG2_E
        for lo in range(0, aw, HEAD_DIM):
            t = r[:, lo:lo + HEAD_DIM]
            if part < 2:
                t = t * cos_ref[...] + pltpu.roll(t, HEAD_DIM // 2, axis=1) * sin_ref[...]
            if part == 0:
                t = t * scale
            qkv_ref[:, part * aw + lo:part * aw + lo + HEAD_DIM] = t.astype(BF16)

    units = [(project_conv, finish_conv, c0) for c0 in range(0, cw, cg)]
    units += [(project_attn, finish_attn, part) for part in (0, 1, 2)]
    pending = units[0][0](units[0][2])
    ang = pos_ref[...] * inv_ref[...]
    cos_ref[...] = jnp.cos(ang)
    sin_ref[...] = jnp.sin(ang) * sign_ref[...]
    for n, (_, finish, arg) in enumerate(units):
        current = pending
        if n + 1 < len(units):
            pending = units[n + 1][0](units[n + 1][2])
        finish(arg, *current)


def _in_proj(x2d, g, w_in_bf, pos, inv_full, sign, w_conv, b_conv, g_conv, seq_len):
    n_tok, d_model = x2d.shape
    cw = w_conv.shape[1]
    aw = (w_in_bf.shape[1] - 3 * cw) // 3
    tm = ROW_TILE
    assert n_tok % tm == 0 and seq_len % tm == 0 and cw % CONV_COL_GROUP == 0 and aw % HEAD_DIM == 0
    const = lambda i: (0, 0)
    row = lambda i: (i, 0)
    return pl.pallas_call(
        functools.partial(_in_proj_kernel, seq_len // tm),
        grid=(n_tok // tm,),
        in_specs=[
            pl.BlockSpec((tm, d_model), row),
            pl.BlockSpec((1, d_model), const),
            pl.BlockSpec(w_in_bf.shape, const, pipeline_mode=pl.Buffered(1)),
            pl.BlockSpec((tm, 1), row),
            pl.BlockSpec((1, HEAD_DIM), const),
            pl.BlockSpec((1, HEAD_DIM), const),
            pl.BlockSpec((CONV_KSIZE, cw), const),
            pl.BlockSpec((1, cw), const),
            pl.BlockSpec((1, cw), const),
        ],
        out_specs=[pl.BlockSpec((tm, 3 * aw), row), pl.BlockSpec((tm, cw), row)],
        out_shape=[jax.ShapeDtypeStruct((n_tok, 3 * aw), BF16), jax.ShapeDtypeStruct((n_tok, cw), BF16)],
        scratch_shapes=[pltpu.VMEM((tm + SUBLANES, cw), F32),
                        pltpu.VMEM((tm, HEAD_DIM), F32), pltpu.VMEM((tm, HEAD_DIM), F32)],
        compiler_params=_params("arbitrary"),
        name="in_proj",
    )(x2d, g, w_in_bf, pos, inv_full, sign, w_conv, b_conv, g_conv)


def _split_bf16(a, parts):
    out = []
    for _ in range(parts):
        piece = a.astype(BF16)
        out.append(piece)
        a = a - piece.astype(F32)
    return out


def _moba_kernel(n_cast, q_ref, k_ref, v_ref, g_ref, *refs):
    cast_in, o_ref, cast_out = refs[:n_cast], refs[n_cast], refs[n_cast + 1:2 * n_cast + 1]
    qt_ref, vt_ref, m_ref, acc_ref = refs[2 * n_cast + 1:]
    blk = MOBA_BLOCK
    hd = HEAD_DIM
    seq = k_ref.shape[0]
    n_blk = seq // blk
    heads = range(q_ref.shape[1] // hd)

    key_id = lax.broadcasted_iota(jnp.int32, (blk, blk), 0)
    qry_id = lax.broadcasted_iota(jnp.int32, (blk, blk), 1)
    causal = key_id <= qry_id
    lane_id = lax.broadcasted_iota(jnp.int32, (blk, hd), 1)
    blk_id = lax.broadcasted_iota(jnp.int32, (n_blk, seq), 0)
    q_tile = lax.broadcasted_iota(jnp.int32, (n_blk, seq), 1) // blk

    def mask_own_tile(s):
        own = jnp.where(causal, s[:, :blk], MASK_VALUE)
        return own if s.shape[1] == blk else jnp.concatenate([own, s[:, blk:]], axis=1)

    def setup(h):
        cols = slice(h * hd, (h + 1) * hd)
        for c in range(n_blk):
            rows = slice(c * blk, (c + 1) * blk)
            qt_ref[h, 0:hd, rows] = q_ref[rows, cols].astype(F32).T.astype(BF16)
            vt_ref[h, 0:hd, rows] = v_ref[rows, cols].astype(F32).T.astype(BF16)
        qt_ref[h, hd + n_blk:, :] = jnp.zeros((qt_ref.shape[1] - hd - n_blk, seq), BF16)
        vt_ref[h, hd:, :] = jnp.ones((vt_ref.shape[1] - hd, seq), BF16)

    def finish_tile(h, c0):
        cols = slice(h * hd, (h + 1) * hd)
        o = (acc_ref[h, 0:hd, c0:c0 + blk] / acc_ref[h, hd:hd + 1, c0:c0 + blk]).T
        o_ref[c0:c0 + blk, cols] = _rms_norm(o, g_ref[:, cols]).astype(BF16)

    def block0_scores(h):
        cols = slice(h * hd, (h + 1) * hd)
        kf = k_ref[:, cols].astype(F32).reshape(n_blk, blk, hd)
        kmean = jnp.sum(kf, axis=1) * (1.0 / blk)
        return _dot(jnp.concatenate([k_ref[0:blk, cols]] + _split_bf16(kmean, 3), axis=0), qt_ref[h, 0:hd, :])

    def block0_update(h, r):
        gate = (r[blk:blk + n_blk] + r[blk + n_blk:blk + 2 * n_blk]) + r[blk + 2 * n_blk:blk + 3 * n_blk]
        rank = jnp.zeros((n_blk, seq), jnp.int32)
        for j in range(n_blk):
            gj = gate[j:j + 1, :]
            beats = (gj > gate) | ((gj == gate) & (j < blk_id))
            rank = rank + jnp.where(beats & (j < q_tile), 1, 0)
        visible = ((blk_id < q_tile) & (rank < MOBA_TOPK)) | (blk_id == q_tile)
        bias = jnp.where(visible, 0.0, MASK_VALUE).astype(F32)
        qt_ref[h, hd:hd + n_blk, :] = bias.astype(BF16)
        s = mask_own_tile(r[0:blk] + bias[0:1, :])
        m_new = jnp.max(s, axis=0, keepdims=True)
        acc_ref[h] = _dot(vt_ref[h, :, 0:blk], jnp.exp2(s - m_new).astype(BF16))
        m_ref[h] = m_new
        finish_tile(h, 0)

    def scores(h, j):
        c0 = j * blk
        one_hot = jnp.where(lane_id == j, 1.0, 0.0).astype(BF16)
        k_aug = jnp.concatenate([k_ref[c0:c0 + blk, h * hd:(h + 1) * hd], one_hot], axis=1)
        s = mask_own_tile(_dot(k_aug, qt_ref[h, :, c0:]))
        return s, jnp.max(s, axis=0, keepdims=True)

    def update(h, j, s, s_max):
        c0 = j * blk
        m_old = m_ref[h, :, c0:]
        m_new = jnp.maximum(m_old, s_max)
        alpha = jnp.exp2(m_old - m_new)
        p = jnp.exp2(s - m_new).astype(BF16)
        acc_ref[h, :, c0:] = alpha * acc_ref[h, :, c0:] + _dot(vt_ref[h, :, c0:c0 + blk], p)
        m_ref[h, :, c0:] = m_new
        finish_tile(h, c0)

    for h in heads:
        setup(h)
    r0 = [block0_scores(h) for h in heads]
    for src, dst in zip(cast_in, cast_out):
        dst[...] = src[...].astype(BF16)
    for h in heads:
        block0_update(h, r0[h])
    pending = [scores(h, 1) for h in heads]
    for j in range(1, n_blk):
        for h in heads:
            s, s_max = pending[h]
            if j + 1 < n_blk:
                pending[h] = scores(h, j + 1)
            update(h, j, s, s_max)


def _moba_attention(qkv, g_attn, bsz, seq_len, later_weights):
    n_tok = qkv.shape[0]
    hps = MOBA_HEADS_PER_STEP
    groups = N_ATTN_HEADS // hps
    width = hps * HEAD_DIM
    n_blk = seq_len // MOBA_BLOCK
    steps = bsz * groups
    slab = lambda b, h: (b * groups + h, 0)
    assert seq_len % MOBA_BLOCK == 0 and n_blk % BF16_PACK == 0 and HEAD_DIM + n_blk <= MXU_DEPTH
    for w in later_weights:
        assert w.shape[0] % (steps * BF16_PACK) == 0, w.shape
    w_blocks = [(w.shape[0] // steps, w.shape[1]) for w in later_weights]
    outs = pl.pallas_call(
        functools.partial(_moba_kernel, len(later_weights)),
        grid=(bsz, groups),
        in_specs=[
            pl.BlockSpec((seq_len, width), lambda b, h: (b, h)),
            pl.BlockSpec((seq_len, width), lambda b, h: (b, groups + h)),
            pl.BlockSpec((seq_len, width), lambda b, h: (b, 2 * groups + h)),
            pl.BlockSpec((1, width), lambda b, h: (0, h)),
        ] + [pl.BlockSpec(blk, slab) for blk in w_blocks],
        out_specs=[pl.BlockSpec((seq_len, width), lambda b, h: (b, h))]
        + [pl.BlockSpec(blk, slab) for blk in w_blocks],
        out_shape=[jax.ShapeDtypeStruct((n_tok, N_ATTN_HEADS * HEAD_DIM), BF16)]
        + [jax.ShapeDtypeStruct(w.shape, BF16) for w in later_weights],
        scratch_shapes=[
            pltpu.VMEM((hps, MXU_DEPTH, seq_len), BF16),
            pltpu.VMEM((hps, HEAD_DIM + BF16_PACK, seq_len), BF16),
            pltpu.VMEM((hps, 1, seq_len), F32),
            pltpu.VMEM((hps, HEAD_DIM + BF16_PACK, seq_len), F32),
        ],
        compiler_params=_params("parallel", "parallel"),
        name="moba_attention",
    )(qkv, qkv, qkv, g_attn, *later_weights)
    return outs[0], outs[1:]


def _mem_kv_kernel(m_ref, g_ref, w_ref, o_ref):
    h = _rms_norm(m_ref[0], g_ref[...]).astype(BF16)
    o_ref[0] = _dot(h, w_ref[...]).astype(BF16)


def _mem_kv(mem, g, w_xkv_bf):
    bsz, n_mem, d_model = mem.shape
    width = w_xkv_bf.shape[1]
    return pl.pallas_call(
        _mem_kv_kernel,
        grid=(bsz,),
        in_specs=[
            pl.BlockSpec((1, n_mem, d_model), lambda b: (b, 0, 0)),
            pl.BlockSpec((1, d_model), lambda b: (0, 0)),
            pl.BlockSpec(w_xkv_bf.shape, lambda b: (0, 0)),
        ],
        out_specs=pl.BlockSpec((1, n_mem, width), lambda b: (b, 0, 0)),
        out_shape=jax.ShapeDtypeStruct((bsz, n_mem, width), BF16),
        compiler_params=_params("parallel"),
        name="mem_kv",
    )(mem, g, w_xkv_bf)


def _mix_kernel(n_cast, a_ref, c_ref, wout_ref, x_ref, g_ref, wq_ref, kv_ref, wo_ref, *refs):
    cast_in, o_ref, cast_out = refs[:n_cast], refs[n_cast], refs[n_cast + 1:]
    half = x_ref.shape[0] // 2
    aw = a_ref.shape[1]
    halves = (slice(0, half), slice(half, 2 * half))

    def out_proj(rows):
        y = _dot(a_ref[rows, :], wout_ref[0:aw, :]) + _dot(c_ref[rows, :], wout_ref[aw:, :])
        return x_ref[rows, :] + y

    def queries(x1):
        h = _rms_norm(x1, g_ref[...]).astype(BF16)
        return (_dot(h, wq_ref[...]) * (XATTN_HEAD_DIM ** -0.5)).astype(BF16)

    def attend(q):
        heads = []
        for hh in range(N_XATTN_HEADS):
            lo = hh * XATTN_HEAD_DIM
            k = kv_ref[0, :, lo:lo + XATTN_HEAD_DIM]
            v = kv_ref[0, :, XATTN_WIDTH + lo:XATTN_WIDTH + lo + XATTN_HEAD_DIM]
            s = _dot_nt(q[:, lo:lo + XATTN_HEAD_DIM], k)
            p = jnp.exp(s - jnp.max(s, axis=-1, keepdims=True))
            o = _dot(p.astype(BF16), v) / jnp.sum(p, axis=-1, keepdims=True)
            heads.append(o.astype(BF16))
        return jnp.concatenate(heads, axis=1)

    x1 = [out_proj(rows) for rows in halves]
    for src, dst in zip(cast_in, cast_out):
        dst[...] = src[...].astype(BF16)
    q = [queries(t) for t in x1]
    for rows, t, qh in zip(halves, x1, q):
        o_ref[rows, :] = t + _dot(attend(qh), wo_ref[...])


def _mix(o_attn, y_conv, w_out_bf, x2d, g, w_xq_bf, kv, w_xo_bf, seq_len, later_weights):
    n_tok, d_model = x2d.shape
    tm = ROW_TILE
    assert n_tok % tm == 0 and seq_len % tm == 0
    steps = n_tok // tm
    tiles_per_seq = seq_len // tm
    const = lambda i: (0, 0)
    row = lambda i: (i, 0)
    resident = pl.Buffered(1)
    for w in later_weights:
        assert w.shape[0] % (steps * BF16_PACK) == 0, w.shape
    w_specs = [pl.BlockSpec((w.shape[0] // steps, w.shape[1]), row) for w in later_weights]
    outs = pl.pallas_call(
        functools.partial(_mix_kernel, len(later_weights)),
        grid=(steps,),
        in_specs=[
            pl.BlockSpec((tm, o_attn.shape[1]), row),
            pl.BlockSpec((tm, y_conv.shape[1]), row),
            pl.BlockSpec(w_out_bf.shape, const, pipeline_mode=resident),
            pl.BlockSpec((tm, d_model), row),
            pl.BlockSpec((1, d_model), const),
            pl.BlockSpec(w_xq_bf.shape, const, pipeline_mode=resident),
            pl.BlockSpec((1,) + kv.shape[1:], lambda i: (i // tiles_per_seq, 0, 0)),
            pl.BlockSpec(w_xo_bf.shape, const, pipeline_mode=resident),
        ] + w_specs,
        out_specs=[pl.BlockSpec((tm, d_model), row)] + w_specs,
        out_shape=[jax.ShapeDtypeStruct((n_tok, d_model), F32)]
        + [jax.ShapeDtypeStruct(w.shape, BF16) for w in later_weights],
        compiler_params=_params("parallel"),
        name="mix_out_xattn",
    )(o_attn, y_conv, w_out_bf, x2d, g, w_xq_bf, kv, w_xo_bf, *later_weights)
    return outs[0], outs[1:]


def _ffn_kernel(final_norm, x_ref, g_ref, wg_ref, wu_ref, wd_ref, gf_ref, o_ref, h_ref):
    j = pl.program_id(1)

    @pl.when(j == 0)
    def _():
        x = x_ref[...]
        h_ref[...] = _rms_norm(x, g_ref[...]).astype(BF16)
        o_ref[...] = x

    h = h_ref[...]
    half = wg_ref.shape[1] // 2
    gu = [(_dot(h, wg_ref[:, c:c + half]), _dot(h, wu_ref[:, c:c + half])) for c in (0, half)]
    down = [_dot((gate * jax.nn.sigmoid(gate) * up).astype(BF16), wd_ref[c:c + half, :])
            for c, (gate, up) in zip((0, half), gu)]
    o_ref[...] += down[0] + down[1]

    if final_norm:
        @pl.when(j == pl.num_programs(1) - 1)
        def _():
            o_ref[...] = _rms_norm(o_ref[...], gf_ref[...])


def _ffn(x2d, g, w_gate_bf, w_up_bf, w_down_bf, g_final, final_norm):
    n_tok, d_model = x2d.shape
    d_ff = w_gate_bf.shape[1]
    tm, tf = FFN_ROW_TILE, FF_TILE
    assert n_tok % tm == 0 and d_ff % tf == 0
    return pl.pallas_call(
        functools.partial(_ffn_kernel, final_norm),
        grid=(n_tok // tm, d_ff // tf),
        in_specs=[
            pl.BlockSpec((tm, d_model), lambda i, j: (i, 0)),
            pl.BlockSpec((1, d_model), lambda i, j: (0, 0)),
            pl.BlockSpec((d_model, tf), lambda i, j: (0, j)),
            pl.BlockSpec((d_model, tf), lambda i, j: (0, j)),
            pl.BlockSpec((tf, d_model), lambda i, j: (j, 0)),
            pl.BlockSpec((1, d_model), lambda i, j: (0, 0)),
        ],
        out_specs=pl.BlockSpec((tm, d_model), lambda i, j: (i, 0)),
        out_shape=jax.ShapeDtypeStruct((n_tok, d_model), F32),
        scratch_shapes=[pltpu.VMEM((tm, d_model), BF16)],
        compiler_params=_params("parallel", "arbitrary", vmem_limit=FFN_VMEM_LIMIT),
        name="ffn",
    )(x2d, g, w_gate_bf, w_up_bf, w_down_bf, g_final)


def kernel(x, mem, positions, g_mix, w_in, w_conv, b_conv, g_attn_out, g_conv_out, w_out, g_xattn, g_mem,
           w_xq, w_xkv, w_xo, g_ffn, w_gate, w_up, w_down, g_final):
    bsz, seq_len, d_model = x.shape
    depth = g_mix.shape[0]
    row = lambda a: a.reshape(1, -1)
    pos, inv_full, sign = _rope_inputs(positions)
    x2d = x.reshape(bsz * seq_len, d_model)
    for l in range(depth):
        qkv, y_conv = _in_proj(x2d, row(g_mix[l]), w_in[l].astype(BF16), pos, inv_full, sign, w_conv[l],
                               row(b_conv[l]), row(g_conv_out[l]), seq_len)
        o_attn, (w_out_bf, w_xq_bf, w_xkv_bf, w_xo_bf) = _moba_attention(
            qkv, row(g_attn_out[l]), bsz, seq_len, [w_out[l], w_xq[l], w_xkv[l], w_xo[l]])
        kv = _mem_kv(mem, row(g_mem[l]), w_xkv_bf)
        x2d, (w_gate_bf, w_up_bf, w_down_bf) = _mix(o_attn, y_conv, w_out_bf, x2d, row(g_xattn[l]), w_xq_bf, kv,
                                                    w_xo_bf, seq_len, [w_gate[l], w_up[l], w_down[l]])
        x2d = _ffn(x2d, row(g_ffn[l]), w_gate_bf, w_up_bf, w_down_bf, row(g_final), final_norm=l == depth - 1)
    return x2d.reshape(bsz, seq_len, d_model)
```

```python
import functools

import jax
import jax.numpy as jnp
from jax import lax
from jax.experimental import pallas as pl
from jax.experimental.pallas import tpu as pltpu

HEAD_DIM = 128
N_ATTN_HEADS = 8
CONV_KSIZE = 3
MOBA_BLOCK = 256
MOBA_TOPK = 3
ROPE_THETA = 10000.0
N_XATTN_HEADS = 4
XATTN_HEAD_DIM = 128
XATTN_WIDTH = N_XATTN_HEADS * XATTN_HEAD_DIM
EPS = 1e-6

LOG2_E = 1.4426950408889634
MASK_VALUE = -1e30
SUBLANES = 8
VMEM_LIMIT = 56 * 1024 * 1024

ROW_TILE = 512
FF_TILE = 512
CONV_COL_GROUP = 256
FFN_ROW_TILE = 1024
FFN_VMEM_LIMIT = 60 * 1024 * 1024
MOBA_QUERY_PIECE = 1024
BF16_PACK = 16
MXU_DEPTH = 256

BF16 = jnp.bfloat16
F32 = jnp.float32


def _rms_norm(xf, g):
    y = xf * lax.rsqrt(jnp.mean(xf * xf, axis=-1, keepdims=True) + EPS)
    return y * g


def _dot(a, b):
    return jnp.dot(a, b, preferred_element_type=F32)


def _dot_nt(a, b):
    return lax.dot_general(a, b, (((1,), (1,)), ((), ())), preferred_element_type=F32)


def _params(*semantics, vmem_limit=VMEM_LIMIT):
    return pltpu.CompilerParams(dimension_semantics=semantics, vmem_limit_bytes=vmem_limit)


def _rope_inputs(positions):
    half = HEAD_DIM // 2
    inv_freq = ROPE_THETA ** (-jnp.arange(0, HEAD_DIM, 2, dtype=F32) / HEAD_DIM)
    inv_full = jnp.concatenate([inv_freq, inv_freq]).reshape(1, HEAD_DIM)
    sign = jnp.concatenate([-jnp.ones((half,), F32), jnp.ones((half,), F32)]).reshape(1, HEAD_DIM)
    return positions.astype(F32).reshape(positions.size, 1), inv_full, sign


def _in_proj_kernel(tiles_per_seq, x_ref, g_ref, w_ref, pos_ref, inv_ref, sign_ref, wc_ref, bc_ref, gc_ref,
                    qkv_ref, conv_ref, uext_ref, cos_ref, sin_ref):
    tm = x_ref.shape[0]
    aw = qkv_ref.shape[1] // 3
    cw = conv_ref.shape[1]
    cg = CONV_COL_GROUP

    @pl.when(pl.program_id(0) % tiles_per_seq == 0)
    def _():
        uext_ref[0:SUBLANES, :] = jnp.zeros((SUBLANES, cw), F32)

    h = _rms_norm(x_ref[...], g_ref[...]).astype(BF16)

    def project_conv(c0):
        return tuple(_dot(h, w_ref[:, 3 * aw + part * cw + c0:3 * aw + part * cw + c0 + cg]) for part in range(3))

    def finish_conv(c0, c_gate, b_gate, x_in):
        cols = slice(c0, c0 + cg)
        u = c_gate * x_in
        uext_ref[SUBLANES:, cols] = u
        u1 = uext_ref[SUBLANES - 1:SUBLANES - 1 + tm, cols]
        u2 = uext_ref[SUBLANES - 2:SUBLANES - 2 + tm, cols]
        conv = wc_ref[0:1, cols] * u2 + wc_ref[1:2, cols] * u1 + wc_ref[2:3, cols] * u
        y = b_gate * (conv + bc_ref[:, cols])
        uext_ref[0:SUBLANES, cols] = u[tm - SUBLANES:, :]
        for lo in range(0, cg, HEAD_DIM):
            yg = _rms_norm(y[:, lo:lo + HEAD_DIM], gc_ref[:, c0 + lo:c0 + lo + HEAD_DIM])
            conv_ref[:, c0 + lo:c0 + lo + HEAD_DIM] = yg.astype(BF16)

    def project_attn(part):
        return (_dot(h, w_ref[:, part * aw:(part + 1) * aw]),)

    def finish_attn(part, r):
        scale = HEAD_DIM ** -0.5 * LOG2_E
        for lo in range(0, aw, HEAD_DIM):
            t = r[:, lo:lo + HEAD_DIM]
            if part < 2:
                t = t * cos_ref[...] + pltpu.roll(t, HEAD_DIM // 2, axis=1) * sin_ref[...]
            if part == 0:
                t = t * scale
            qkv_ref[:, part * aw + lo:part * aw + lo + HEAD_DIM] = t.astype(BF16)

    units = [(project_conv, finish_conv, c0) for c0 in range(0, cw, cg)]
    units += [(project_attn, finish_attn, part) for part in (0, 1, 2)]
    pending = units[0][0](units[0][2])
    ang = pos_ref[...] * inv_ref[...]
    cos_ref[...] = jnp.cos(ang)
    sin_ref[...] = jnp.sin(ang) * sign_ref[...]
    for n, (_, finish, arg) in enumerate(units):
        current = pending
        if n + 1 < len(units):
            pending = units[n + 1][0](units[n + 1][2])
        finish(arg, *current)


def _in_proj(x2d, g, w_in_bf, pos, inv_full, sign, w_conv, b_conv, g_conv, seq_len):
    n_tok, d_model = x2d.shape
    cw = w_conv.shape[1]
    aw = (w_in_bf.shape[1] - 3 * cw) // 3
    tm = ROW_TILE
    assert n_tok % tm == 0 and seq_len % tm == 0 and cw % CONV_COL_GROUP == 0 and aw % HEAD_DIM == 0
    const = lambda i: (0, 0)
    row = lambda i: (i, 0)
    return pl.pallas_call(
        functools.partial(_in_proj_kernel, seq_len // tm),
        grid=(n_tok // tm,),
        in_specs=[
            pl.BlockSpec((tm, d_model), row),
            pl.BlockSpec((1, d_model), const),
            pl.BlockSpec(w_in_bf.shape, const, pipeline_mode=pl.Buffered(1)),
            pl.BlockSpec((tm, 1), row),
            pl.BlockSpec((1, HEAD_DIM), const),
            pl.BlockSpec((1, HEAD_DIM), const),
            pl.BlockSpec((CONV_KSIZE, cw), const),
            pl.BlockSpec((1, cw), const),
            pl.BlockSpec((1, cw), const),
        ],
        out_specs=[pl.BlockSpec((tm, 3 * aw), row), pl.BlockSpec((tm, cw), row)],
        out_shape=[jax.ShapeDtypeStruct((n_tok, 3 * aw), BF16), jax.ShapeDtypeStruct((n_tok, cw), BF16)],
        scratch_shapes=[pltpu.VMEM((tm + SUBLANES, cw), F32),
                        pltpu.VMEM((tm, HEAD_DIM), F32), pltpu.VMEM((tm, HEAD_DIM), F32)],
        compiler_params=_params("arbitrary"),
        name="in_proj",
    )(x2d, g, w_in_bf, pos, inv_full, sign, w_conv, b_conv, g_conv)


def _split_bf16(a, parts):
    out = []
    for _ in range(parts):
        piece = a.astype(BF16)
        out.append(piece)
        a = a - piece.astype(F32)
    return out


def _moba_kernel(n_cast, q_ref, k_ref, v_ref, g_ref, *refs):
    cast_in, o_ref, cast_out = refs[:n_cast], refs[n_cast], refs[n_cast + 1:2 * n_cast + 1]
    qt_ref, vt_ref, m_ref, acc_ref = refs[2 * n_cast + 1:]
    blk = MOBA_BLOCK
    hd = HEAD_DIM
    seq = k_ref.shape[0]
    n_blk = seq // blk

    for c in range(n_blk):
        rows = slice(c * blk, (c + 1) * blk)
        qt_ref[0:hd, rows] = q_ref[rows, :].astype(F32).T.astype(BF16)
        vt_ref[0:hd, rows] = v_ref[rows, :].astype(F32).T.astype(BF16)
    qt_ref[hd + n_blk:, :] = jnp.zeros((qt_ref.shape[0] - hd - n_blk, seq), BF16)
    vt_ref[hd:, :] = jnp.ones((vt_ref.shape[0] - hd, seq), BF16)

    kf = k_ref[...].astype(F32).reshape(n_blk, blk, hd)
    kmean = jnp.sum(kf, axis=1) * (1.0 / blk)
    kmean_parts = _split_bf16(kmean, 3)

    key_id = lax.broadcasted_iota(jnp.int32, (blk, blk), 0)
    qry_id = lax.broadcasted_iota(jnp.int32, (blk, blk), 1)
    causal = key_id <= qry_id
    lane_id = lax.broadcasted_iota(jnp.int32, (blk, hd), 1)

    def mask_own_tile(s):
        own = jnp.where(causal, s[:, :blk], MASK_VALUE)
        return own if s.shape[1] == blk else jnp.concatenate([own, s[:, blk:]], axis=1)

    def finish_tile(c0):
        o = (acc_ref[0:hd, c0:c0 + blk] / acc_ref[hd:hd + 1, c0:c0 + blk]).T
        o_ref[c0:c0 + blk, :] = _rms_norm(o, g_ref[...]).astype(BF16)

    r = _dot(jnp.concatenate([k_ref[0:blk, :]] + kmean_parts, axis=0), qt_ref[0:hd, :])
    for src, dst in zip(cast_in, cast_out):
        dst[...] = src[...].astype(BF16)
    gate = (r[blk:blk + n_blk] + r[blk + n_blk:blk + 2 * n_blk]) + r[blk + 2 * n_blk:blk + 3 * n_blk]
    blk_id = lax.broadcasted_iota(jnp.int32, (n_blk, seq), 0)
    q_tile = lax.broadcasted_iota(jnp.int32, (n_blk, seq), 1) // blk
    rank = jnp.zeros((n_blk, seq), jnp.int32)
    for j in range(n_blk):
        gj = gate[j:j + 1, :]
        beats = (gj > gate) | ((gj == gate) & (j < blk_id))
        rank = rank + jnp.where(beats & (j < q_tile), 1, 0)
    visible = ((blk_id < q_tile) & (rank < MOBA_TOPK)) | (blk_id == q_tile)
    bias = jnp.where(visible, 0.0, MASK_VALUE).astype(F32)
    qt_ref[hd:hd + n_blk, :] = bias.astype(BF16)

    s = mask_own_tile(r[0:blk] + bias[0:1, :])
    m_new = jnp.max(s, axis=0, keepdims=True)
    acc_ref[...] = _dot(vt_ref[:, 0:blk], jnp.exp2(s - m_new).astype(BF16))
    m_ref[...] = m_new
    finish_tile(0)

    units = [(j, p0, min(p0 + MOBA_QUERY_PIECE, seq))
             for j in range(1, n_blk) for p0 in range(j * blk, seq, MOBA_QUERY_PIECE)]

    def scores(j, p0, p1):
        c0 = j * blk
        k_aug = jnp.concatenate([k_ref[c0:c0 + blk, :], jnp.where(lane_id == j, 1.0, 0.0).astype(BF16)], axis=1)
        s = _dot(k_aug, qt_ref[:, p0:p1])
        if p0 == c0:
            s = mask_own_tile(s)
        return s, jnp.max(s, axis=0, keepdims=True)

    pending = scores(*units[0])
    for n, (j, p0, p1) in enumerate(units):
        c0 = j * blk
        s, s_max = pending
        if n + 1 < len(units):
            pending = scores(*units[n + 1])
        m_old = m_ref[:, p0:p1]
        m_new = jnp.maximum(m_old, s_max)
        alpha = jnp.exp2(m_old - m_new)
        p = jnp.exp2(s - m_new).astype(BF16)
        acc_ref[:, p0:p1] = alpha * acc_ref[:, p0:p1] + _dot(vt_ref[:, c0:c0 + blk], p)
        m_ref[:, p0:p1] = m_new
        if p0 == c0:
            finish_tile(c0)


def _moba_attention(qkv, g_attn, bsz, seq_len, later_weights):
    n_tok = qkv.shape[0]
    nh = N_ATTN_HEADS
    n_blk = seq_len // MOBA_BLOCK
    steps = bsz * nh
    slab = lambda b, h: (b * nh + h, 0)
    assert seq_len % MOBA_BLOCK == 0 and n_blk % BF16_PACK == 0 and HEAD_DIM + n_blk <= MXU_DEPTH
    for w in later_weights:
        assert w.shape[0] % (steps * BF16_PACK) == 0, w.shape
    w_blocks = [(w.shape[0] // steps, w.shape[1]) for w in later_weights]
    outs = pl.pallas_call(
        functools.partial(_moba_kernel, len(later_weights)),
        grid=(bsz, nh),
        in_specs=[
            pl.BlockSpec((seq_len, HEAD_DIM), lambda b, h: (b, h)),
            pl.BlockSpec((seq_len, HEAD_DIM), lambda b, h: (b, nh + h)),
            pl.BlockSpec((seq_len, HEAD_DIM), lambda b, h: (b, 2 * nh + h)),
            pl.BlockSpec((1, HEAD_DIM), lambda b, h: (0, h)),
        ] + [pl.BlockSpec(blk, slab) for blk in w_blocks],
        out_specs=[pl.BlockSpec((seq_len, HEAD_DIM), lambda b, h: (b, h))]
        + [pl.BlockSpec(blk, slab) for blk in w_blocks],
        out_shape=[jax.ShapeDtypeStruct((n_tok, nh * HEAD_DIM), BF16)]
        + [jax.ShapeDtypeStruct(w.shape, BF16) for w in later_weights],
        scratch_shapes=[
            pltpu.VMEM((MXU_DEPTH, seq_len), BF16),
            pltpu.VMEM((HEAD_DIM + BF16_PACK, seq_len), BF16),
            pltpu.VMEM((1, seq_len), F32),
            pltpu.VMEM((HEAD_DIM + BF16_PACK, seq_len), F32),
        ],
        compiler_params=_params("parallel", "parallel"),
        name="moba_attention",
    )(qkv, qkv, qkv, g_attn, *later_weights)
    return outs[0], outs[1:]


def _mem_kv_kernel(m_ref, g_ref, w_ref, o_ref):
    h = _rms_norm(m_ref[0], g_ref[...]).astype(BF16)
    o_ref[0] = _dot(h, w_ref[...]).astype(BF16)


def _mem_kv(mem, g, w_xkv_bf):
    bsz, n_mem, d_model = mem.shape
    width = w_xkv_bf.shape[1]
    return pl.pallas_call(
        _mem_kv_kernel,
        grid=(bsz,),
        in_specs=[
            pl.BlockSpec((1, n_mem, d_model), lambda b: (b, 0, 0)),
            pl.BlockSpec((1, d_model), lambda b: (0, 0)),
            pl.BlockSpec(w_xkv_bf.shape, lambda b: (0, 0)),
        ],
        out_specs=pl.BlockSpec((1, n_mem, width), lambda b: (b, 0, 0)),
        out_shape=jax.ShapeDtypeStruct((bsz, n_mem, width), BF16),
        compiler_params=_params("parallel"),
        name="mem_kv",
    )(mem, g, w_xkv_bf)


def _mix_kernel(a_ref, c_ref, wout_ref, x_ref, g_ref, wq_ref, kv_ref, wo_ref, o_ref):
    half = x_ref.shape[0] // 2
    aw = a_ref.shape[1]
    halves = (slice(0, half), slice(half, 2 * half))

    def out_proj(rows):
        y = _dot(a_ref[rows, :], wout_ref[0:aw, :]) + _dot(c_ref[rows, :], wout_ref[aw:, :])
        return x_ref[rows, :] + y

    def queries(x1):
        h = _rms_norm(x1, g_ref[...]).astype(BF16)
        return (_dot(h, wq_ref[...]) * (XATTN_HEAD_DIM ** -0.5)).astype(BF16)

    def attend(q):
        heads = []
        for hh in range(N_XATTN_HEADS):
            lo = hh * XATTN_HEAD_DIM
            k = kv_ref[0, :, lo:lo + XATTN_HEAD_DIM]
            v = kv_ref[0, :, XATTN_WIDTH + lo:XATTN_WIDTH + lo + XATTN_HEAD_DIM]
            s = _dot_nt(q[:, lo:lo + XATTN_HEAD_DIM], k)
            p = jnp.exp(s - jnp.max(s, axis=-1, keepdims=True))
            o = _dot(p.astype(BF16), v) / jnp.sum(p, axis=-1, keepdims=True)
            heads.append(o.astype(BF16))
        return jnp.concatenate(heads, axis=1)

    x1 = [out_proj(rows) for rows in halves]
    q = [queries(t) for t in x1]
    for rows, t, qh in zip(halves, x1, q):
        o_ref[rows, :] = t + _dot(attend(qh), wo_ref[...])


def _mix(o_attn, y_conv, w_out_bf, x2d, g, w_xq_bf, kv, w_xo_bf, seq_len):
    n_tok, d_model = x2d.shape
    tm = ROW_TILE
    assert n_tok % tm == 0 and seq_len % tm == 0
    tiles_per_seq = seq_len // tm
    const = lambda i: (0, 0)
    row = lambda i: (i, 0)
    resident = pl.Buffered(1)
    return pl.pallas_call(
        _mix_kernel,
        grid=(n_tok // tm,),
        in_specs=[
            pl.BlockSpec((tm, o_attn.shape[1]), row),
            pl.BlockSpec((tm, y_conv.shape[1]), row),
            pl.BlockSpec(w_out_bf.shape, const, pipeline_mode=resident),
            pl.BlockSpec((tm, d_model), row),
            pl.BlockSpec((1, d_model), const),
            pl.BlockSpec(w_xq_bf.shape, const, pipeline_mode=resident),
            pl.BlockSpec((1,) + kv.shape[1:], lambda i: (i // tiles_per_seq, 0, 0)),
            pl.BlockSpec(w_xo_bf.shape, const, pipeline_mode=resident),
        ],
        out_specs=pl.BlockSpec((tm, d_model), row),
        out_shape=jax.ShapeDtypeStruct((n_tok, d_model), F32),
        compiler_params=_params("parallel"),
        name="mix_out_xattn",
    )(o_attn, y_conv, w_out_bf, x2d, g, w_xq_bf, kv, w_xo_bf)


def _ffn_kernel(final_norm, x_ref, g_ref, wg_ref, wu_ref, wd_ref, gf_ref, o_ref, h_ref):
    j = pl.program_id(1)

    @pl.when(j == 0)
    def _():
        x = x_ref[...]
        h_ref[...] = _rms_norm(x, g_ref[...]).astype(BF16)
        o_ref[...] = x

    h = h_ref[...]
    half = wg_ref.shape[1] // 2
    gu = [(_dot(h, wg_ref[:, c:c + half]), _dot(h, wu_ref[:, c:c + half])) for c in (0, half)]
    down = [_dot((gate * jax.nn.sigmoid(gate) * up).astype(BF16), wd_ref[c:c + half, :])
            for c, (gate, up) in zip((0, half), gu)]
    o_ref[...] += down[0] + down[1]

    if final_norm:
        @pl.when(j == pl.num_programs(1) - 1)
        def _():
            o_ref[...] = _rms_norm(o_ref[...], gf_ref[...])


def _ffn(x2d, g, w_gate_bf, w_up_bf, w_down_bf, g_final, final_norm):
    n_tok, d_model = x2d.shape
    d_ff = w_gate_bf.shape[1]
    tm, tf = FFN_ROW_TILE, FF_TILE
    assert n_tok % tm == 0 and d_ff % tf == 0
    return pl.pallas_call(
        functools.partial(_ffn_kernel, final_norm),
        grid=(n_tok // tm, d_ff // tf),
        in_specs=[
            pl.BlockSpec((tm, d_model), lambda i, j: (i, 0)),
            pl.BlockSpec((1, d_model), lambda i, j: (0, 0)),
            pl.BlockSpec((d_model, tf), lambda i, j: (0, j)),
            pl.BlockSpec((d_model, tf), lambda i, j: (0, j)),
            pl.BlockSpec((tf, d_model), lambda i, j: (j, 0)),
            pl.BlockSpec((1, d_model), lambda i, j: (0, 0)),
        ],
        out_specs=pl.BlockSpec((tm, d_model), lambda i, j: (i, 0)),
        out_shape=jax.ShapeDtypeStruct((n_tok, d_model), F32),
        scratch_shapes=[pltpu.VMEM((tm, d_model), BF16)],
        compiler_params=_params("parallel", "arbitrary", vmem_limit=FFN_VMEM_LIMIT),
        name="ffn",
    )(x2d, g, w_gate_bf, w_up_bf, w_down_bf, g_final)


def kernel(x, mem, positions, g_mix, w_in, w_conv, b_conv, g_attn_out, g_conv_out, w_out, g_xattn, g_mem,
           w_xq, w_xkv, w_xo, g_ffn, w_gate, w_up, w_down, g_final):
    bsz, seq_len, d_model = x.shape
    depth = g_mix.shape[0]
    row = lambda a: a.reshape(1, -1)
    pos, inv_full, sign = _rope_inputs(positions)
    x2d = x.reshape(bsz * seq_len, d_model)
    for l in range(depth):
        qkv, y_conv = _in_proj(x2d, row(g_mix[l]), w_in[l].astype(BF16), pos, inv_full, sign, w_conv[l],
                               row(b_conv[l]), row(g_conv_out[l]), seq_len)
        o_attn, (w_out_bf, w_xq_bf, w_xkv_bf, w_xo_bf, w_gate_bf, w_up_bf, w_down_bf) = _moba_attention(
            qkv, row(g_attn_out[l]), bsz, seq_len,
            [w_out[l], w_xq[l], w_xkv[l], w_xo[l], w_gate[l], w_up[l], w_down[l]])
        kv = _mem_kv(mem, row(g_mem[l]), w_xkv_bf)
        x2d = _mix(o_attn, y_conv, w_out_bf, x2d, row(g_xattn[l]), w_xq_bf, kv, w_xo_bf, seq_len)
        x2d = _ffn(x2d, row(g_ffn[l]), w_gate_bf, w_up_bf, w_down_bf, row(g_final), final_norm=l == depth - 1)
    return x2d.reshape(bsz, seq_len, d_model)
```

```python
import functools

import jax
import jax.numpy as jnp
from jax import lax
from jax.experimental import pallas as pl
from jax.experimental.pallas import tpu as pltpu

HEAD_DIM = 128
N_ATTN_HEADS = 8
CONV_KSIZE = 3
MOBA_BLOCK = 256
MOBA_TOPK = 3
ROPE_THETA = 10000.0
N_XATTN_HEADS = 4
XATTN_HEAD_DIM = 128
XATTN_WIDTH = N_XATTN_HEADS * XATTN_HEAD_DIM
EPS = 1e-6

LOG2_E = 1.4426950408889634
MASK_VALUE = -1e30
SUBLANES = 8
VMEM_LIMIT = 56 * 1024 * 1024

ROW_TILE = 512
FF_TILE = 512
CONV_COL_GROUP = 256
FFN_ROW_TILE = 1024
FFN_VMEM_LIMIT = 60 * 1024 * 1024
BF16_PACK = 16
MXU_DEPTH = 256

BF16 = jnp.bfloat16
F32 = jnp.float32


def _rms_norm(xf, g):
    y = xf * lax.rsqrt(jnp.mean(xf * xf, axis=-1, keepdims=True) + EPS)
    return y * g


def _dot(a, b):
    return jnp.dot(a, b, preferred_element_type=F32)


def _dot_nt(a, b):
    return lax.dot_general(a, b, (((1,), (1,)), ((), ())), preferred_element_type=F32)


def _params(*semantics, vmem_limit=VMEM_LIMIT):
    return pltpu.CompilerParams(dimension_semantics=semantics, vmem_limit_bytes=vmem_limit)


def _rope_inputs(positions):
    half = HEAD_DIM // 2
    inv_freq = ROPE_THETA ** (-jnp.arange(0, HEAD_DIM, 2, dtype=F32) / HEAD_DIM)
    inv_full = jnp.concatenate([inv_freq, inv_freq]).reshape(1, HEAD_DIM)
    sign = jnp.concatenate([-jnp.ones((half,), F32), jnp.ones((half,), F32)]).reshape(1, HEAD_DIM)
    return positions.astype(F32).reshape(positions.size, 1), inv_full, sign


def _in_proj_kernel(tiles_per_seq, x_ref, g_ref, w_ref, pos_ref, inv_ref, sign_ref, wc_ref, bc_ref, gc_ref,
                    qkv_ref, conv_ref, uext_ref, cos_ref, sin_ref):
    tm = x_ref.shape[0]
    aw = qkv_ref.shape[1] // 3
    cw = conv_ref.shape[1]
    cg = CONV_COL_GROUP

    @pl.when(pl.program_id(0) % tiles_per_seq == 0)
    def _():
        uext_ref[0:SUBLANES, :] = jnp.zeros((SUBLANES, cw), F32)

    h = _rms_norm(x_ref[...], g_ref[...]).astype(BF16)

    def project_conv(c0):
        return tuple(_dot(h, w_ref[:, 3 * aw + part * cw + c0:3 * aw + part * cw + c0 + cg]) for part in range(3))

    def finish_conv(c0, c_gate, b_gate, x_in):
        cols = slice(c0, c0 + cg)
        u = c_gate * x_in
        uext_ref[SUBLANES:, cols] = u
        u1 = uext_ref[SUBLANES - 1:SUBLANES - 1 + tm, cols]
        u2 = uext_ref[SUBLANES - 2:SUBLANES - 2 + tm, cols]
        conv = wc_ref[0:1, cols] * u2 + wc_ref[1:2, cols] * u1 + wc_ref[2:3, cols] * u
        y = b_gate * (conv + bc_ref[:, cols])
        uext_ref[0:SUBLANES, cols] = u[tm - SUBLANES:, :]
        for lo in range(0, cg, HEAD_DIM):
            yg = _rms_norm(y[:, lo:lo + HEAD_DIM], gc_ref[:, c0 + lo:c0 + lo + HEAD_DIM])
            conv_ref[:, c0 + lo:c0 + lo + HEAD_DIM] = yg.astype(BF16)

    def project_attn(part):
        return (_dot(h, w_ref[:, part * aw:(part + 1) * aw]),)

    def finish_attn(part, r):
        scale = HEAD_DIM ** -0.5 * LOG2_E
        for lo in range(0, aw, HEAD_DIM):
            t = r[:, lo:lo + HEAD_DIM]
            if part < 2:
                t = t * cos_ref[...] + pltpu.roll(t, HEAD_DIM // 2, axis=1) * sin_ref[...]
            if part == 0:
                t = t * scale
            qkv_ref[:, part * aw + lo:part * aw + lo + HEAD_DIM] = t.astype(BF16)

    units = [(project_conv, finish_conv, c0) for c0 in range(0, cw, cg)]
    units += [(project_attn, finish_attn, part) for part in (0, 1, 2)]
    pending = units[0][0](units[0][2])
    ang = pos_ref[...] * inv_ref[...]
    cos_ref[...] = jnp.cos(ang)
    sin_ref[...] = jnp.sin(ang) * sign_ref[...]
    for n, (_, finish, arg) in enumerate(units):
        current = pending
        if n + 1 < len(units):
            pending = units[n + 1][0](units[n + 1][2])
        finish(arg, *current)


def _in_proj(x2d, g, w_in_bf, pos, inv_full, sign, w_conv, b_conv, g_conv, seq_len):
    n_tok, d_model = x2d.shape
    cw = w_conv.shape[1]
    aw = (w_in_bf.shape[1] - 3 * cw) // 3
    tm = ROW_TILE
    assert n_tok % tm == 0 and seq_len % tm == 0 and cw % CONV_COL_GROUP == 0 and aw % HEAD_DIM == 0
    const = lambda i: (0, 0)
    row = lambda i: (i, 0)
    return pl.pallas_call(
        functools.partial(_in_proj_kernel, seq_len // tm),
        grid=(n_tok // tm,),
        in_specs=[
            pl.BlockSpec((tm, d_model), row),
            pl.BlockSpec((1, d_model), const),
            pl.BlockSpec(w_in_bf.shape, const, pipeline_mode=pl.Buffered(1)),
            pl.BlockSpec((tm, 1), row),
            pl.BlockSpec((1, HEAD_DIM), const),
            pl.BlockSpec((1, HEAD_DIM), const),
            pl.BlockSpec((CONV_KSIZE, cw), const),
            pl.BlockSpec((1, cw), const),
            pl.BlockSpec((1, cw), const),
        ],
        out_specs=[pl.BlockSpec((tm, 3 * aw), row), pl.BlockSpec((tm, cw), row)],
        out_shape=[jax.ShapeDtypeStruct((n_tok, 3 * aw), BF16), jax.ShapeDtypeStruct((n_tok, cw), BF16)],
        scratch_shapes=[pltpu.VMEM((tm + SUBLANES, cw), F32),
                        pltpu.VMEM((tm, HEAD_DIM), F32), pltpu.VMEM((tm, HEAD_DIM), F32)],
        compiler_params=_params("arbitrary"),
        name="in_proj",
    )(x2d, g, w_in_bf, pos, inv_full, sign, w_conv, b_conv, g_conv)


def _split_bf16(a, parts):
    out = []
    for _ in range(parts):
        piece = a.astype(BF16)
        out.append(piece)
        a = a - piece.astype(F32)
    return out


def _moba_kernel(n_cast, q_ref, k_ref, v_ref, g_ref, *refs):
    cast_in, o_ref, cast_out = refs[:n_cast], refs[n_cast], refs[n_cast + 1:2 * n_cast + 1]
    qt_ref, vt_ref, m_ref, acc_ref = refs[2 * n_cast + 1:]
    blk = MOBA_BLOCK
    hd = HEAD_DIM
    seq = k_ref.shape[0]
    n_blk = seq // blk

    for c in range(n_blk):
        rows = slice(c * blk, (c + 1) * blk)
        qt_ref[0:hd, rows] = q_ref[rows, :].astype(F32).T.astype(BF16)
        vt_ref[0:hd, rows] = v_ref[rows, :].astype(F32).T.astype(BF16)
    qt_ref[hd + n_blk:, :] = jnp.zeros((qt_ref.shape[0] - hd - n_blk, seq), BF16)
    vt_ref[hd:, :] = jnp.ones((vt_ref.shape[0] - hd, seq), BF16)

    kf = k_ref[...].astype(F32).reshape(n_blk, blk, hd)
    kmean = jnp.sum(kf, axis=1) * (1.0 / blk)
    kmean_parts = _split_bf16(kmean, 3)

    key_id = lax.broadcasted_iota(jnp.int32, (blk, blk), 0)
    qry_id = lax.broadcasted_iota(jnp.int32, (blk, blk), 1)
    causal = key_id <= qry_id
    lane_id = lax.broadcasted_iota(jnp.int32, (blk, hd), 1)

    def mask_own_tile(s):
        own = jnp.where(causal, s[:, :blk], MASK_VALUE)
        return own if s.shape[1] == blk else jnp.concatenate([own, s[:, blk:]], axis=1)

    def finish_tile(c0):
        o = (acc_ref[0:hd, c0:c0 + blk] / acc_ref[hd:hd + 1, c0:c0 + blk]).T
        o_ref[c0:c0 + blk, :] = _rms_norm(o, g_ref[...]).astype(BF16)

    r = _dot(jnp.concatenate([k_ref[0:blk, :]] + kmean_parts, axis=0), qt_ref[0:hd, :])
    for src, dst in zip(cast_in, cast_out):
        dst[...] = src[...].astype(BF16)
    gate = (r[blk:blk + n_blk] + r[blk + n_blk:blk + 2 * n_blk]) + r[blk + 2 * n_blk:blk + 3 * n_blk]
    blk_id = lax.broadcasted_iota(jnp.int32, (n_blk, seq), 0)
    q_tile = lax.broadcasted_iota(jnp.int32, (n_blk, seq), 1) // blk
    rank = jnp.zeros((n_blk, seq), jnp.int32)
    for j in range(n_blk):
        gj = gate[j:j + 1, :]
        beats = (gj > gate) | ((gj == gate) & (j < blk_id))
        rank = rank + jnp.where(beats & (j < q_tile), 1, 0)
    visible = ((blk_id < q_tile) & (rank < MOBA_TOPK)) | (blk_id == q_tile)
    bias = jnp.where(visible, 0.0, MASK_VALUE).astype(F32)
    qt_ref[hd:hd + n_blk, :] = bias.astype(BF16)

    s = mask_own_tile(r[0:blk] + bias[0:1, :])
    m_new = jnp.max(s, axis=0, keepdims=True)
    acc_ref[...] = _dot(vt_ref[:, 0:blk], jnp.exp2(s - m_new).astype(BF16))
    m_ref[...] = m_new
    finish_tile(0)

    def scores(j):
        c0 = j * blk
        k_aug = jnp.concatenate([k_ref[c0:c0 + blk, :], jnp.where(lane_id == j, 1.0, 0.0).astype(BF16)], axis=1)
        s = mask_own_tile(_dot(k_aug, qt_ref[:, c0:]))
        return s, jnp.max(s, axis=0, keepdims=True)

    pending = scores(1)
    for j in range(1, n_blk):
        c0 = j * blk
        s, s_max = pending
        if j + 1 < n_blk:
            pending = scores(j + 1)
        m_old = m_ref[:, c0:]
        m_new = jnp.maximum(m_old, s_max)
        alpha = jnp.exp2(m_old - m_new)
        p = jnp.exp2(s - m_new).astype(BF16)
        acc_ref[:, c0:] = alpha * acc_ref[:, c0:] + _dot(vt_ref[:, c0:c0 + blk], p)
        m_ref[:, c0:] = m_new
        finish_tile(c0)


def _moba_attention(qkv, g_attn, bsz, seq_len, later_weights):
    n_tok = qkv.shape[0]
    nh = N_ATTN_HEADS
    n_blk = seq_len // MOBA_BLOCK
    steps = bsz * nh
    slab = lambda b, h: (b * nh + h, 0)
    assert seq_len % MOBA_BLOCK == 0 and n_blk % BF16_PACK == 0 and HEAD_DIM + n_blk <= MXU_DEPTH
    for w in later_weights:
        assert w.shape[0] % (steps * BF16_PACK) == 0, w.shape
    w_blocks = [(w.shape[0] // steps, w.shape[1]) for w in later_weights]
    outs = pl.pallas_call(
        functools.partial(_moba_kernel, len(later_weights)),
        grid=(bsz, nh),
        in_specs=[
            pl.BlockSpec((seq_len, HEAD_DIM), lambda b, h: (b, h)),
            pl.BlockSpec((seq_len, HEAD_DIM), lambda b, h: (b, nh + h)),
            pl.BlockSpec((seq_len, HEAD_DIM), lambda b, h: (b, 2 * nh + h)),
            pl.BlockSpec((1, HEAD_DIM), lambda b, h: (0, h)),
        ] + [pl.BlockSpec(blk, slab) for blk in w_blocks],
        out_specs=[pl.BlockSpec((seq_len, HEAD_DIM), lambda b, h: (b, h))]
        + [pl.BlockSpec(blk, slab) for blk in w_blocks],
        out_shape=[jax.ShapeDtypeStruct((n_tok, nh * HEAD_DIM), BF16)]
        + [jax.ShapeDtypeStruct(w.shape, BF16) for w in later_weights],
        scratch_shapes=[
            pltpu.VMEM((MXU_DEPTH, seq_len), BF16),
            pltpu.VMEM((HEAD_DIM + BF16_PACK, seq_len), BF16),
            pltpu.VMEM((1, seq_len), F32),
            pltpu.VMEM((HEAD_DIM + BF16_PACK, seq_len), F32),
        ],
        compiler_params=_params("parallel", "parallel"),
        name="moba_attention",
    )(qkv, qkv, qkv, g_attn, *later_weights)
    return outs[0], outs[1:]


def _mix_kernel(tiles_per_seq, a_ref, c_ref, wout_ref, x_ref, g_ref, wq_ref, mem_ref, gm_ref, wkv_ref, wo_ref,
                o_ref, kv_ref):
    @pl.when(pl.program_id(0) % tiles_per_seq == 0)
    def _():
        m = _rms_norm(mem_ref[0], gm_ref[...]).astype(BF16)
        kv_ref[...] = _dot(m, wkv_ref[...]).astype(BF16)

    half = x_ref.shape[0] // 2
    aw = a_ref.shape[1]
    halves = (slice(0, half), slice(half, 2 * half))

    def out_proj(rows):
        y = _dot(a_ref[rows, :], wout_ref[0:aw, :]) + _dot(c_ref[rows, :], wout_ref[aw:, :])
        return x_ref[rows, :] + y

    def queries(x1):
        h = _rms_norm(x1, g_ref[...]).astype(BF16)
        return (_dot(h, wq_ref[...]) * (XATTN_HEAD_DIM ** -0.5)).astype(BF16)

    def attend(q):
        heads = []
        for hh in range(N_XATTN_HEADS):
            lo = hh * XATTN_HEAD_DIM
            k = kv_ref[:, lo:lo + XATTN_HEAD_DIM]
            v = kv_ref[:, XATTN_WIDTH + lo:XATTN_WIDTH + lo + XATTN_HEAD_DIM]
            s = _dot_nt(q[:, lo:lo + XATTN_HEAD_DIM], k)
            p = jnp.exp(s - jnp.max(s, axis=-1, keepdims=True))
            o = _dot(p.astype(BF16), v) / jnp.sum(p, axis=-1, keepdims=True)
            heads.append(o.astype(BF16))
        return jnp.concatenate(heads, axis=1)

    x1 = [out_proj(rows) for rows in halves]
    q = [queries(t) for t in x1]
    for rows, t, qh in zip(halves, x1, q):
        o_ref[rows, :] = t + _dot(attend(qh), wo_ref[...])


def _mix(o_attn, y_conv, w_out_bf, x2d, g, w_xq_bf, mem, g_mem, w_xkv_bf, w_xo_bf, seq_len):
    n_tok, d_model = x2d.shape
    n_mem = mem.shape[1]
    tm = ROW_TILE
    assert n_tok % tm == 0 and seq_len % tm == 0
    tiles_per_seq = seq_len // tm
    const = lambda i: (0, 0)
    row = lambda i: (i, 0)
    resident = pl.Buffered(1)
    return pl.pallas_call(
        functools.partial(_mix_kernel, tiles_per_seq),
        grid=(n_tok // tm,),
        in_specs=[
            pl.BlockSpec((tm, o_attn.shape[1]), row),
            pl.BlockSpec((tm, y_conv.shape[1]), row),
            pl.BlockSpec(w_out_bf.shape, const, pipeline_mode=resident),
            pl.BlockSpec((tm, d_model), row),
            pl.BlockSpec((1, d_model), const),
            pl.BlockSpec(w_xq_bf.shape, const, pipeline_mode=resident),
            pl.BlockSpec((1, n_mem, d_model), lambda i: (i // tiles_per_seq, 0, 0)),
            pl.BlockSpec((1, d_model), const),
            pl.BlockSpec(w_xkv_bf.shape, const, pipeline_mode=resident),
            pl.BlockSpec(w_xo_bf.shape, const, pipeline_mode=resident),
        ],
        out_specs=pl.BlockSpec((tm, d_model), row),
        out_shape=jax.ShapeDtypeStruct((n_tok, d_model), F32),
        scratch_shapes=[pltpu.VMEM((n_mem, w_xkv_bf.shape[1]), BF16)],
        compiler_params=_params("arbitrary"),
        name="mix_out_xattn",
    )(o_attn, y_conv, w_out_bf, x2d, g, w_xq_bf, mem, g_mem, w_xkv_bf, w_xo_bf)


def _ffn_kernel(final_norm, x_ref, g_ref, wg_ref, wu_ref, wd_ref, gf_ref, o_ref, h_ref):
    j = pl.program_id(1)

    @pl.when(j == 0)
    def _():
        x = x_ref[...]
        h_ref[...] = _rms_norm(x, g_ref[...]).astype(BF16)
        o_ref[...] = x

    h = h_ref[...]
    half = wg_ref.shape[1] // 2
    gu = [(_dot(h, wg_ref[:, c:c + half]), _dot(h, wu_ref[:, c:c + half])) for c in (0, half)]
    down = [_dot((gate * jax.nn.sigmoid(gate) * up).astype(BF16), wd_ref[c:c + half, :])
            for c, (gate, up) in zip((0, half), gu)]
    o_ref[...] += down[0] + down[1]

    if final_norm:
        @pl.when(j == pl.num_programs(1) - 1)
        def _():
            o_ref[...] = _rms_norm(o_ref[...], gf_ref[...])


def _ffn(x2d, g, w_gate_bf, w_up_bf, w_down_bf, g_final, final_norm):
    n_tok, d_model = x2d.shape
    d_ff = w_gate_bf.shape[1]
    tm, tf = FFN_ROW_TILE, FF_TILE
    assert n_tok % tm == 0 and d_ff % tf == 0
    return pl.pallas_call(
        functools.partial(_ffn_kernel, final_norm),
        grid=(n_tok // tm, d_ff // tf),
        in_specs=[
            pl.BlockSpec((tm, d_model), lambda i, j: (i, 0)),
            pl.BlockSpec((1, d_model), lambda i, j: (0, 0)),
            pl.BlockSpec((d_model, tf), lambda i, j: (0, j)),
            pl.BlockSpec((d_model, tf), lambda i, j: (0, j)),
            pl.BlockSpec((tf, d_model), lambda i, j: (j, 0)),
            pl.BlockSpec((1, d_model), lambda i, j: (0, 0)),
        ],
        out_specs=pl.BlockSpec((tm, d_model), lambda i, j: (i, 0)),
        out_shape=jax.ShapeDtypeStruct((n_tok, d_model), F32),
        scratch_shapes=[pltpu.VMEM((tm, d_model), BF16)],
        compiler_params=_params("parallel", "arbitrary", vmem_limit=FFN_VMEM_LIMIT),
        name="ffn",
    )(x2d, g, w_gate_bf, w_up_bf, w_down_bf, g_final)


def kernel(x, mem, positions, g_mix, w_in, w_conv, b_conv, g_attn_out, g_conv_out, w_out, g_xattn, g_mem,
           w_xq, w_xkv, w_xo, g_ffn, w_gate, w_up, w_down, g_final):
    bsz, seq_len, d_model = x.shape
    depth = g_mix.shape[0]
    row = lambda a: a.reshape(1, -1)
    pos, inv_full, sign = _rope_inputs(positions)
    x2d = x.reshape(bsz * seq_len, d_model)
    for l in range(depth):
        qkv, y_conv = _in_proj(x2d, row(g_mix[l]), w_in[l].astype(BF16), pos, inv_full, sign, w_conv[l],
                               row(b_conv[l]), row(g_conv_out[l]), seq_len)
        o_attn, (w_out_bf, w_xq_bf, w_xkv_bf, w_xo_bf, w_gate_bf, w_up_bf, w_down_bf) = _moba_attention(
            qkv, row(g_attn_out[l]), bsz, seq_len,
            [w_out[l], w_xq[l], w_xkv[l], w_xo[l], w_gate[l], w_up[l], w_down[l]])
        x2d = _mix(o_attn, y_conv, w_out_bf, x2d, row(g_xattn[l]), w_xq_bf, mem, row(g_mem[l]), w_xkv_bf, w_xo_bf,
                   seq_len)
        x2d = _ffn(x2d, row(g_ffn[l]), w_gate_bf, w_up_bf, w_down_bf, row(g_final), final_norm=l == depth - 1)
    return x2d.reshape(bsz, seq_len, d_model)
```

```python
import functools

import jax
import jax.numpy as jnp
from jax import lax
from jax.experimental import pallas as pl
from jax.experimental.pallas import tpu as pltpu

HEAD_DIM = 128
N_ATTN_HEADS = 8
CONV_KSIZE = 3
MOBA_BLOCK = 256
MOBA_TOPK = 3
ROPE_THETA = 10000.0
N_XATTN_HEADS = 4
XATTN_HEAD_DIM = 128
XATTN_WIDTH = N_XATTN_HEADS * XATTN_HEAD_DIM
EPS = 1e-6

LOG2_E = 1.4426950408889634
MASK_VALUE = -1e30
SUBLANES = 8
VMEM_LIMIT = 56 * 1024 * 1024

ROW_TILE = 512
FF_TILE = 512
CONV_COL_GROUP = 256
FFN_ROW_TILE = 1024
FFN_VMEM_LIMIT = 60 * 1024 * 1024
BF16_PACK = 16
MXU_DEPTH = 256

BF16 = jnp.bfloat16
F32 = jnp.float32


def _rms_norm(xf, g):
    y = xf * lax.rsqrt(jnp.mean(xf * xf, axis=-1, keepdims=True) + EPS)
    return y * g


def _dot(a, b):
    return jnp.dot(a, b, preferred_element_type=F32)


def _dot_nt(a, b):
    return lax.dot_general(a, b, (((1,), (1,)), ((), ())), preferred_element_type=F32)


def _params(*semantics, vmem_limit=VMEM_LIMIT):
    return pltpu.CompilerParams(dimension_semantics=semantics, vmem_limit_bytes=vmem_limit)


def _rope_inputs(positions):
    half = HEAD_DIM // 2
    inv_freq = ROPE_THETA ** (-jnp.arange(0, HEAD_DIM, 2, dtype=F32) / HEAD_DIM)
    inv_full = jnp.concatenate([inv_freq, inv_freq]).reshape(1, HEAD_DIM)
    sign = jnp.concatenate([-jnp.ones((half,), F32), jnp.ones((half,), F32)]).reshape(1, HEAD_DIM)
    return positions.astype(F32).reshape(positions.size, 1), inv_full, sign


def _in_proj_kernel(tiles_per_seq, x_ref, g_ref, w_ref, pos_ref, inv_ref, sign_ref, wc_ref, bc_ref, gc_ref,
                    qkv_ref, conv_ref, uext_ref, cos_ref, sin_ref):
    tm = x_ref.shape[0]
    aw = qkv_ref.shape[1] // 3
    cw = conv_ref.shape[1]
    cg = CONV_COL_GROUP

    @pl.when(pl.program_id(0) % tiles_per_seq == 0)
    def _():
        uext_ref[0:SUBLANES, :] = jnp.zeros((SUBLANES, cw), F32)

    h = _rms_norm(x_ref[...], g_ref[...]).astype(BF16)

    def project_conv(c0):
        return tuple(_dot(h, w_ref[:, 3 * aw + part * cw + c0:3 * aw + part * cw + c0 + cg]) for part in range(3))

    def finish_conv(c0, c_gate, b_gate, x_in):
        cols = slice(c0, c0 + cg)
        u = c_gate * x_in
        uext_ref[SUBLANES:, cols] = u
        u1 = uext_ref[SUBLANES - 1:SUBLANES - 1 + tm, cols]
        u2 = uext_ref[SUBLANES - 2:SUBLANES - 2 + tm, cols]
        conv = wc_ref[0:1, cols] * u2 + wc_ref[1:2, cols] * u1 + wc_ref[2:3, cols] * u
        y = b_gate * (conv + bc_ref[:, cols])
        uext_ref[0:SUBLANES, cols] = u[tm - SUBLANES:, :]
        for lo in range(0, cg, HEAD_DIM):
            yg = _rms_norm(y[:, lo:lo + HEAD_DIM], gc_ref[:, c0 + lo:c0 + lo + HEAD_DIM])
            conv_ref[:, c0 + lo:c0 + lo + HEAD_DIM] = yg.astype(BF16)

    def project_attn(part):
        return (_dot(h, w_ref[:, part * aw:(part + 1) * aw]),)

    def finish_attn(part, r):
        scale = HEAD_DIM ** -0.5 * LOG2_E
        for lo in range(0, aw, HEAD_DIM):
            t = r[:, lo:lo + HEAD_DIM]
            if part < 2:
                t = t * cos_ref[...] + pltpu.roll(t, HEAD_DIM // 2, axis=1) * sin_ref[...]
            if part == 0:
                t = t * scale
            qkv_ref[:, part * aw + lo:part * aw + lo + HEAD_DIM] = t.astype(BF16)

    units = [(project_conv, finish_conv, c0) for c0 in range(0, cw, cg)]
    units += [(project_attn, finish_attn, part) for part in (0, 1, 2)]
    pending = units[0][0](units[0][2])
    ang = pos_ref[...] * inv_ref[...]
    cos_ref[...] = jnp.cos(ang)
    sin_ref[...] = jnp.sin(ang) * sign_ref[...]
    for n, (_, finish, arg) in enumerate(units):
        current = pending
        if n + 1 < len(units):
            pending = units[n + 1][0](units[n + 1][2])
        finish(arg, *current)


def _in_proj(x2d, g, w_in_bf, pos, inv_full, sign, w_conv, b_conv, g_conv, seq_len):
    n_tok, d_model = x2d.shape
    cw = w_conv.shape[1]
    aw = (w_in_bf.shape[1] - 3 * cw) // 3
    tm = ROW_TILE
    assert n_tok % tm == 0 and seq_len % tm == 0 and cw % CONV_COL_GROUP == 0 and aw % HEAD_DIM == 0
    const = lambda i: (0, 0)
    row = lambda i: (i, 0)
    return pl.pallas_call(
        functools.partial(_in_proj_kernel, seq_len // tm),
        grid=(n_tok // tm,),
        in_specs=[
            pl.BlockSpec((tm, d_model), row),
            pl.BlockSpec((1, d_model), const),
            pl.BlockSpec(w_in_bf.shape, const, pipeline_mode=pl.Buffered(1)),
            pl.BlockSpec((tm, 1), row),
            pl.BlockSpec((1, HEAD_DIM), const),
            pl.BlockSpec((1, HEAD_DIM), const),
            pl.BlockSpec((CONV_KSIZE, cw), const),
            pl.BlockSpec((1, cw), const),
            pl.BlockSpec((1, cw), const),
        ],
        out_specs=[pl.BlockSpec((tm, 3 * aw), row), pl.BlockSpec((tm, cw), row)],
        out_shape=[jax.ShapeDtypeStruct((n_tok, 3 * aw), BF16), jax.ShapeDtypeStruct((n_tok, cw), BF16)],
        scratch_shapes=[pltpu.VMEM((tm + SUBLANES, cw), F32),
                        pltpu.VMEM((tm, HEAD_DIM), F32), pltpu.VMEM((tm, HEAD_DIM), F32)],
        compiler_params=_params("arbitrary"),
        name="in_proj",
    )(x2d, g, w_in_bf, pos, inv_full, sign, w_conv, b_conv, g_conv)


def _split_bf16(a, parts):
    out = []
    for _ in range(parts):
        piece = a.astype(BF16)
        out.append(piece)
        a = a - piece.astype(F32)
    return out


def _moba_kernel(n_cast, q_ref, k_ref, v_ref, g_ref, *refs):
    cast_in, o_ref, cast_out = refs[:n_cast], refs[n_cast], refs[n_cast + 1:2 * n_cast + 1]
    qt_ref, vt_ref, m_ref, acc_ref = refs[2 * n_cast + 1:]
    blk = MOBA_BLOCK
    hd = HEAD_DIM
    seq = k_ref.shape[0]
    n_blk = seq // blk

    for c in range(n_blk):
        rows = slice(c * blk, (c + 1) * blk)
        qt_ref[0:hd, rows] = q_ref[rows, :].astype(F32).T.astype(BF16)
        vt_ref[0:hd, rows] = v_ref[rows, :].astype(F32).T.astype(BF16)
    qt_ref[hd + n_blk:, :] = jnp.zeros((qt_ref.shape[0] - hd - n_blk, seq), BF16)
    vt_ref[hd:, :] = jnp.ones((vt_ref.shape[0] - hd, seq), BF16)

    kf = k_ref[...].astype(F32).reshape(n_blk, blk, hd)
    kmean = jnp.sum(kf, axis=1) * (1.0 / blk)
    kmean_parts = _split_bf16(kmean, 3)

    key_id = lax.broadcasted_iota(jnp.int32, (blk, blk), 0)
    qry_id = lax.broadcasted_iota(jnp.int32, (blk, blk), 1)
    causal = key_id <= qry_id
    lane_id = lax.broadcasted_iota(jnp.int32, (blk, hd), 1)

    def mask_own_tile(s):
        own = jnp.where(causal, s[:, :blk], MASK_VALUE)
        return own if s.shape[1] == blk else jnp.concatenate([own, s[:, blk:]], axis=1)

    def finish_tile(c0):
        o = (acc_ref[0:hd, c0:c0 + blk] / acc_ref[hd:hd + 1, c0:c0 + blk]).T
        o_ref[c0:c0 + blk, :] = _rms_norm(o, g_ref[...]).astype(BF16)

    r = _dot(jnp.concatenate([k_ref[0:blk, :]] + kmean_parts, axis=0), qt_ref[0:hd, :])
    for src, dst in zip(cast_in, cast_out):
        if len(dst.shape) == 2:
            dst[...] = src[...].astype(BF16)
        else:
            for t in range(dst.shape[0]):
                dst[t] = src[:, t * dst.shape[2]:(t + 1) * dst.shape[2]].astype(BF16)
    gate = (r[blk:blk + n_blk] + r[blk + n_blk:blk + 2 * n_blk]) + r[blk + 2 * n_blk:blk + 3 * n_blk]
    blk_id = lax.broadcasted_iota(jnp.int32, (n_blk, seq), 0)
    q_tile = lax.broadcasted_iota(jnp.int32, (n_blk, seq), 1) // blk
    rank = jnp.zeros((n_blk, seq), jnp.int32)
    for j in range(n_blk):
        gj = gate[j:j + 1, :]
        beats = (gj > gate) | ((gj == gate) & (j < blk_id))
        rank = rank + jnp.where(beats & (j < q_tile), 1, 0)
    visible = ((blk_id < q_tile) & (rank < MOBA_TOPK)) | (blk_id == q_tile)
    bias = jnp.where(visible, 0.0, MASK_VALUE).astype(F32)
    qt_ref[hd:hd + n_blk, :] = bias.astype(BF16)

    s = mask_own_tile(r[0:blk] + bias[0:1, :])
    m_new = jnp.max(s, axis=0, keepdims=True)
    acc_ref[...] = _dot(vt_ref[:, 0:blk], jnp.exp2(s - m_new).astype(BF16))
    m_ref[...] = m_new
    finish_tile(0)

    def scores(j):
        c0 = j * blk
        k_aug = jnp.concatenate([k_ref[c0:c0 + blk, :], jnp.where(lane_id == j, 1.0, 0.0).astype(BF16)], axis=1)
        s = mask_own_tile(_dot(k_aug, qt_ref[:, c0:]))
        return s, jnp.max(s, axis=0, keepdims=True)

    pending = scores(1)
    for j in range(1, n_blk):
        c0 = j * blk
        s, s_max = pending
        if j + 1 < n_blk:
            pending = scores(j + 1)
        m_old = m_ref[:, c0:]
        m_new = jnp.maximum(m_old, s_max)
        alpha = jnp.exp2(m_old - m_new)
        p = jnp.exp2(s - m_new).astype(BF16)
        acc_ref[:, c0:] = alpha * acc_ref[:, c0:] + _dot(vt_ref[:, c0:c0 + blk], p)
        m_ref[:, c0:] = m_new
        finish_tile(c0)


def _moba_attention(qkv, g_attn, bsz, seq_len, later_weights):
    n_tok = qkv.shape[0]
    nh = N_ATTN_HEADS
    n_blk = seq_len // MOBA_BLOCK
    steps = bsz * nh
    slab = lambda b, h: (b * nh + h, 0)
    assert seq_len % MOBA_BLOCK == 0 and n_blk % BF16_PACK == 0 and HEAD_DIM + n_blk <= MXU_DEPTH
    weights = [w for w, _ in later_weights]
    for w, col_tile in later_weights:
        assert w.shape[0] % (steps * BF16_PACK) == 0 and (col_tile is None or w.shape[1] % col_tile == 0), w.shape
    w_blocks = [(w.shape[0] // steps, w.shape[1]) for w in weights]
    copy_specs, copy_shapes = [], []
    for (w, col_tile), (rows, cols) in zip(later_weights, w_blocks):
        if col_tile is None:
            copy_specs.append(pl.BlockSpec((rows, cols), slab))
            copy_shapes.append(jax.ShapeDtypeStruct(w.shape, BF16))
        else:
            copy_specs.append(pl.BlockSpec((cols // col_tile, rows, col_tile), lambda b, h: (0, b * nh + h, 0)))
            copy_shapes.append(jax.ShapeDtypeStruct((cols // col_tile, w.shape[0], col_tile), BF16))
    outs = pl.pallas_call(
        functools.partial(_moba_kernel, len(later_weights)),
        grid=(bsz, nh),
        in_specs=[
            pl.BlockSpec((seq_len, HEAD_DIM), lambda b, h: (b, h)),
            pl.BlockSpec((seq_len, HEAD_DIM), lambda b, h: (b, nh + h)),
            pl.BlockSpec((seq_len, HEAD_DIM), lambda b, h: (b, 2 * nh + h)),
            pl.BlockSpec((1, HEAD_DIM), lambda b, h: (0, h)),
        ] + [pl.BlockSpec(blk, slab) for blk in w_blocks],
        out_specs=[pl.BlockSpec((seq_len, HEAD_DIM), lambda b, h: (b, h))] + copy_specs,
        out_shape=[jax.ShapeDtypeStruct((n_tok, nh * HEAD_DIM), BF16)] + copy_shapes,
        scratch_shapes=[
            pltpu.VMEM((MXU_DEPTH, seq_len), BF16),
            pltpu.VMEM((HEAD_DIM + BF16_PACK, seq_len), BF16),
            pltpu.VMEM((1, seq_len), F32),
            pltpu.VMEM((HEAD_DIM + BF16_PACK, seq_len), F32),
        ],
        compiler_params=_params("parallel", "parallel"),
        name="moba_attention",
    )(qkv, qkv, qkv, g_attn, *weights)
    return outs[0], outs[1:]


def _mix_kernel(tiles_per_seq, a_ref, c_ref, wout_ref, x_ref, g_ref, wq_ref, mem_ref, gm_ref, wkv_ref, wo_ref,
                o_ref, kv_ref):
    @pl.when(pl.program_id(0) % tiles_per_seq == 0)
    def _():
        m = _rms_norm(mem_ref[0], gm_ref[...]).astype(BF16)
        kv_ref[...] = _dot(m, wkv_ref[...]).astype(BF16)

    half = x_ref.shape[0] // 2
    aw = a_ref.shape[1]
    halves = (slice(0, half), slice(half, 2 * half))

    def out_proj(rows):
        y = _dot(a_ref[rows, :], wout_ref[0:aw, :]) + _dot(c_ref[rows, :], wout_ref[aw:, :])
        return x_ref[rows, :] + y

    def queries(x1):
        h = _rms_norm(x1, g_ref[...]).astype(BF16)
        return (_dot(h, wq_ref[...]) * (XATTN_HEAD_DIM ** -0.5)).astype(BF16)

    def attend(q):
        heads = []
        for hh in range(N_XATTN_HEADS):
            lo = hh * XATTN_HEAD_DIM
            k = kv_ref[:, lo:lo + XATTN_HEAD_DIM]
            v = kv_ref[:, XATTN_WIDTH + lo:XATTN_WIDTH + lo + XATTN_HEAD_DIM]
            s = _dot_nt(q[:, lo:lo + XATTN_HEAD_DIM], k)
            p = jnp.exp(s - jnp.max(s, axis=-1, keepdims=True))
            o = _dot(p.astype(BF16), v) / jnp.sum(p, axis=-1, keepdims=True)
            heads.append(o.astype(BF16))
        return jnp.concatenate(heads, axis=1)

    x1 = [out_proj(rows) for rows in halves]
    q = [queries(t) for t in x1]
    for rows, t, qh in zip(halves, x1, q):
        o_ref[rows, :] = t + _dot(attend(qh), wo_ref[...])


def _mix(o_attn, y_conv, w_out_bf, x2d, g, w_xq_bf, mem, g_mem, w_xkv_bf, w_xo_bf, seq_len):
    n_tok, d_model = x2d.shape
    n_mem = mem.shape[1]
    tm = ROW_TILE
    assert n_tok % tm == 0 and seq_len % tm == 0
    tiles_per_seq = seq_len // tm
    const = lambda i: (0, 0)
    row = lambda i: (i, 0)
    resident = pl.Buffered(1)
    return pl.pallas_call(
        functools.partial(_mix_kernel, tiles_per_seq),
        grid=(n_tok // tm,),
        in_specs=[
            pl.BlockSpec((tm, o_attn.shape[1]), row),
            pl.BlockSpec((tm, y_conv.shape[1]), row),
            pl.BlockSpec(w_out_bf.shape, const, pipeline_mode=resident),
            pl.BlockSpec((tm, d_model), row),
            pl.BlockSpec((1, d_model), const),
            pl.BlockSpec(w_xq_bf.shape, const, pipeline_mode=resident),
            pl.BlockSpec((1, n_mem, d_model), lambda i: (i // tiles_per_seq, 0, 0)),
            pl.BlockSpec((1, d_model), const),
            pl.BlockSpec(w_xkv_bf.shape, const, pipeline_mode=resident),
            pl.BlockSpec(w_xo_bf.shape, const, pipeline_mode=resident),
        ],
        out_specs=pl.BlockSpec((tm, d_model), row),
        out_shape=jax.ShapeDtypeStruct((n_tok, d_model), F32),
        scratch_shapes=[pltpu.VMEM((n_mem, w_xkv_bf.shape[1]), BF16)],
        compiler_params=_params("arbitrary"),
        name="mix_out_xattn",
    )(o_attn, y_conv, w_out_bf, x2d, g, w_xq_bf, mem, g_mem, w_xkv_bf, w_xo_bf)


def _ffn_kernel(final_norm, x_ref, g_ref, wg_ref, wu_ref, wd_ref, gf_ref, o_ref, h_ref):
    j = pl.program_id(1)

    @pl.when(j == 0)
    def _():
        x = x_ref[...]
        h_ref[...] = _rms_norm(x, g_ref[...]).astype(BF16)
        o_ref[...] = x

    h = h_ref[...]
    half = wg_ref.shape[1] // 2
    gu = [(_dot(h, wg_ref[:, c:c + half]), _dot(h, wu_ref[:, c:c + half])) for c in (0, half)]
    down = [_dot((gate * jax.nn.sigmoid(gate) * up).astype(BF16), wd_ref[c:c + half, :])
            for c, (gate, up) in zip((0, half), gu)]
    o_ref[...] += down[0] + down[1]

    if final_norm:
        @pl.when(j == pl.num_programs(1) - 1)
        def _():
            o_ref[...] = _rms_norm(o_ref[...], gf_ref[...])


def _ffn(x2d, g, w_gate_bf, w_up_bf, w_down_bf, g_final, final_norm):
    n_tok, d_model = x2d.shape
    n_ff_tiles, _, tf = w_gate_bf.shape
    tm = FFN_ROW_TILE
    assert n_tok % tm == 0 and w_down_bf.shape[0] == n_ff_tiles * tf
    return pl.pallas_call(
        functools.partial(_ffn_kernel, final_norm),
        grid=(n_tok // tm, n_ff_tiles),
        in_specs=[
            pl.BlockSpec((tm, d_model), lambda i, j: (i, 0)),
            pl.BlockSpec((1, d_model), lambda i, j: (0, 0)),
            pl.BlockSpec((None, d_model, tf), lambda i, j: (j, 0, 0)),
            pl.BlockSpec((None, d_model, tf), lambda i, j: (j, 0, 0)),
            pl.BlockSpec((tf, d_model), lambda i, j: (j, 0)),
            pl.BlockSpec((1, d_model), lambda i, j: (0, 0)),
        ],
        out_specs=pl.BlockSpec((tm, d_model), lambda i, j: (i, 0)),
        out_shape=jax.ShapeDtypeStruct((n_tok, d_model), F32),
        scratch_shapes=[pltpu.VMEM((tm, d_model), BF16)],
        compiler_params=_params("parallel", "arbitrary", vmem_limit=FFN_VMEM_LIMIT),
        name="ffn",
    )(x2d, g, w_gate_bf, w_up_bf, w_down_bf, g_final)


def kernel(x, mem, positions, g_mix, w_in, w_conv, b_conv, g_attn_out, g_conv_out, w_out, g_xattn, g_mem,
           w_xq, w_xkv, w_xo, g_ffn, w_gate, w_up, w_down, g_final):
    bsz, seq_len, d_model = x.shape
    depth = g_mix.shape[0]
    row = lambda a: a.reshape(1, -1)
    pos, inv_full, sign = _rope_inputs(positions)
    x2d = x.reshape(bsz * seq_len, d_model)
    for l in range(depth):
        qkv, y_conv = _in_proj(x2d, row(g_mix[l]), w_in[l].astype(BF16), pos, inv_full, sign, w_conv[l],
                               row(b_conv[l]), row(g_conv_out[l]), seq_len)
        o_attn, (w_out_bf, w_xq_bf, w_xkv_bf, w_xo_bf, w_gate_bf, w_up_bf, w_down_bf) = _moba_attention(
            qkv, row(g_attn_out[l]), bsz, seq_len,
            [(w_out[l], None), (w_xq[l], None), (w_xkv[l], None), (w_xo[l], None),
             (w_gate[l], FF_TILE), (w_up[l], FF_TILE), (w_down[l], None)])
        x2d = _mix(o_attn, y_conv, w_out_bf, x2d, row(g_xattn[l]), w_xq_bf, mem, row(g_mem[l]), w_xkv_bf, w_xo_bf,
                   seq_len)
        x2d = _ffn(x2d, row(g_ffn[l]), w_gate_bf, w_up_bf, w_down_bf, row(g_final), final_norm=l == depth - 1)
    return x2d.reshape(bsz, seq_len, d_model)
```

```python
import functools

import jax
import jax.numpy as jnp
from jax import lax
from jax.experimental import pallas as pl
from jax.experimental.pallas import tpu as pltpu

HEAD_DIM = 128
N_ATTN_HEADS = 8
CONV_KSIZE = 3
MOBA_BLOCK = 256
MOBA_TOPK = 3
ROPE_THETA = 10000.0
N_XATTN_HEADS = 4
XATTN_HEAD_DIM = 128
XATTN_WIDTH = N_XATTN_HEADS * XATTN_HEAD_DIM
EPS = 1e-6

LOG2_E = 1.4426950408889634
MASK_VALUE = -1e30
SUBLANES = 8
MIB = 1024 * 1024
V7X_VMEM_BYTES = 64 * MIB
VMEM_LIMIT = V7X_VMEM_BYTES - 8 * MIB

ROW_TILE = 512
FF_TILE = 512
CONV_COL_GROUP = 256
FFN_ROW_TILE = 1024
FFN_VMEM_LIMIT = V7X_VMEM_BYTES - 4 * MIB
BF16_PACK = 16
MXU_DEPTH = 256

BF16 = jnp.bfloat16
F32 = jnp.float32


def _rms_norm(xf, g):
    y = xf * lax.rsqrt(jnp.mean(xf * xf, axis=-1, keepdims=True) + EPS)
    return y * g


def _dot(a, b):
    return jnp.dot(a, b, preferred_element_type=F32)


def _dot_nt(a, b):
    return lax.dot_general(a, b, (((1,), (1,)), ((), ())), preferred_element_type=F32)


def _params(*semantics, vmem_limit=VMEM_LIMIT):
    return pltpu.CompilerParams(dimension_semantics=semantics, vmem_limit_bytes=vmem_limit)


def _rope_inputs(positions):
    half = HEAD_DIM // 2
    inv_freq = ROPE_THETA ** (-jnp.arange(0, HEAD_DIM, 2, dtype=F32) / HEAD_DIM)
    inv_full = jnp.concatenate([inv_freq, inv_freq]).reshape(1, HEAD_DIM)
    sign = jnp.concatenate([-jnp.ones((half,), F32), jnp.ones((half,), F32)]).reshape(1, HEAD_DIM)
    return positions.astype(F32).reshape(positions.size, 1), inv_full, sign


def _in_proj_kernel(tiles_per_seq, x_ref, g_ref, w_ref, pos_ref, inv_ref, sign_ref, wc_ref, bc_ref, gc_ref,
                    qkv_ref, conv_ref, uext_ref, cos_ref, sin_ref):
    tm = x_ref.shape[0]
    aw = qkv_ref.shape[1] // 3
    cw = conv_ref.shape[1]
    cg = CONV_COL_GROUP

    @pl.when(pl.program_id(0) % tiles_per_seq == 0)
    def _():
        uext_ref[0:SUBLANES, :] = jnp.zeros((SUBLANES, cw), F32)

    h = _rms_norm(x_ref[...], g_ref[...]).astype(BF16)

    def project_conv(c0):
        return tuple(_dot(h, w_ref[:, 3 * aw + part * cw + c0:3 * aw + part * cw + c0 + cg]) for part in range(3))

    def finish_conv(c0, c_gate, b_gate, x_in):
        cols = slice(c0, c0 + cg)
        u = c_gate * x_in
        uext_ref[SUBLANES:, cols] = u
        u1 = uext_ref[SUBLANES - 1:SUBLANES - 1 + tm, cols]
        u2 = uext_ref[SUBLANES - 2:SUBLANES - 2 + tm, cols]
        conv = wc_ref[0:1, cols] * u2 + wc_ref[1:2, cols] * u1 + wc_ref[2:3, cols] * u
        y = b_gate * (conv + bc_ref[:, cols])
        uext_ref[0:SUBLANES, cols] = u[tm - SUBLANES:, :]
        for lo in range(0, cg, HEAD_DIM):
            yg = _rms_norm(y[:, lo:lo + HEAD_DIM], gc_ref[:, c0 + lo:c0 + lo + HEAD_DIM])
            conv_ref[:, c0 + lo:c0 + lo + HEAD_DIM] = yg.astype(BF16)

    def project_attn(part):
        return (_dot(h, w_ref[:, part * aw:(part + 1) * aw]),)

    def finish_attn(part, r):
        scale = HEAD_DIM ** -0.5 * LOG2_E
        for lo in range(0, aw, HEAD_DIM):
            t = r[:, lo:lo + HEAD_DIM]
            if part < 2:
                t = t * cos_ref[...] + pltpu.roll(t, HEAD_DIM // 2, axis=1) * sin_ref[...]
            if part == 0:
                t = t * scale
            qkv_ref[:, part * aw + lo:part * aw + lo + HEAD_DIM] = t.astype(BF16)

    units = [(project_conv, finish_conv, c0) for c0 in range(0, cw, cg)]
    units += [(project_attn, finish_attn, part) for part in (0, 1, 2)]
    pending = units[0][0](units[0][2])
    ang = pos_ref[...] * inv_ref[...]
    cos_ref[...] = jnp.cos(ang)
    sin_ref[...] = jnp.sin(ang) * sign_ref[...]
    for n, (_, finish, arg) in enumerate(units):
        current = pending
        if n + 1 < len(units):
            pending = units[n + 1][0](units[n + 1][2])
        finish(arg, *current)


def _in_proj(x2d, g, w_in_bf, pos, inv_full, sign, w_conv, b_conv, g_conv, seq_len):
    n_tok, d_model = x2d.shape
    cw = w_conv.shape[1]
    aw = (w_in_bf.shape[1] - 3 * cw) // 3
    tm = ROW_TILE
    assert n_tok % tm == 0 and seq_len % tm == 0 and cw % CONV_COL_GROUP == 0 and aw % HEAD_DIM == 0
    const = lambda i: (0, 0)
    row = lambda i: (i, 0)
    return pl.pallas_call(
        functools.partial(_in_proj_kernel, seq_len // tm),
        grid=(n_tok // tm,),
        in_specs=[
            pl.BlockSpec((tm, d_model), row),
            pl.BlockSpec((1, d_model), const),
            pl.BlockSpec(w_in_bf.shape, const, pipeline_mode=pl.Buffered(1)),
            pl.BlockSpec((tm, 1), row),
            pl.BlockSpec((1, HEAD_DIM), const),
            pl.BlockSpec((1, HEAD_DIM), const),
            pl.BlockSpec((CONV_KSIZE, cw), const),
            pl.BlockSpec((1, cw), const),
            pl.BlockSpec((1, cw), const),
        ],
        out_specs=[pl.BlockSpec((tm, 3 * aw), row), pl.BlockSpec((tm, cw), row)],
        out_shape=[jax.ShapeDtypeStruct((n_tok, 3 * aw), BF16), jax.ShapeDtypeStruct((n_tok, cw), BF16)],
        scratch_shapes=[pltpu.VMEM((tm + SUBLANES, cw), F32),
                        pltpu.VMEM((tm, HEAD_DIM), F32), pltpu.VMEM((tm, HEAD_DIM), F32)],
        compiler_params=_params("arbitrary"),
        name="in_proj",
    )(x2d, g, w_in_bf, pos, inv_full, sign, w_conv, b_conv, g_conv)


def _split_bf16(a, parts):
    out = []
    for _ in range(parts):
        piece = a.astype(BF16)
        out.append(piece)
        a = a - piece.astype(F32)
    return out


def _moba_kernel(n_cast, q_ref, k_ref, v_ref, g_ref, *refs):
    cast_in, o_ref, cast_out = refs[:n_cast], refs[n_cast], refs[n_cast + 1:2 * n_cast + 1]
    qt_ref, vt_ref, m_ref, acc_ref = refs[2 * n_cast + 1:]
    blk = MOBA_BLOCK
    hd = HEAD_DIM
    seq = k_ref.shape[0]
    n_blk = seq // blk

    for c in range(n_blk):
        rows = slice(c * blk, (c + 1) * blk)
        qt_ref[0:hd, rows] = q_ref[rows, :].astype(F32).T.astype(BF16)
        vt_ref[0:hd, rows] = v_ref[rows, :].astype(F32).T.astype(BF16)
    qt_ref[hd + n_blk:, :] = jnp.zeros((qt_ref.shape[0] - hd - n_blk, seq), BF16)
    vt_ref[hd:, :] = jnp.ones((vt_ref.shape[0] - hd, seq), BF16)

    kf = k_ref[...].astype(F32).reshape(n_blk, blk, hd)
    kmean = jnp.sum(kf, axis=1) * (1.0 / blk)
    kmean_parts = _split_bf16(kmean, 3)

    key_id = lax.broadcasted_iota(jnp.int32, (blk, blk), 0)
    qry_id = lax.broadcasted_iota(jnp.int32, (blk, blk), 1)
    causal = key_id <= qry_id
    lane_id = lax.broadcasted_iota(jnp.int32, (blk, hd), 1)

    def mask_own_tile(s):
        own = jnp.where(causal, s[:, :blk], MASK_VALUE)
        return own if s.shape[1] == blk else jnp.concatenate([own, s[:, blk:]], axis=1)

    def finish_tile(c0):
        o = (acc_ref[0:hd, c0:c0 + blk] / acc_ref[hd:hd + 1, c0:c0 + blk]).T
        o_ref[c0:c0 + blk, :] = _rms_norm(o, g_ref[...]).astype(BF16)

    r = _dot(jnp.concatenate([k_ref[0:blk, :]] + kmean_parts, axis=0), qt_ref[0:hd, :])
    for src, dst in zip(cast_in, cast_out):
        dst[...] = src[...].astype(BF16)
    gate = (r[blk:blk + n_blk] + r[blk + n_blk:blk + 2 * n_blk]) + r[blk + 2 * n_blk:blk + 3 * n_blk]
    blk_id = lax.broadcasted_iota(jnp.int32, (n_blk, seq), 0)
    q_tile = lax.broadcasted_iota(jnp.int32, (n_blk, seq), 1) // blk
    rank = jnp.zeros((n_blk, seq), jnp.int32)
    for j in range(n_blk):
        gj = gate[j:j + 1, :]
        beats = (gj > gate) | ((gj == gate) & (j < blk_id))
        rank = rank + jnp.where(beats & (j < q_tile), 1, 0)
    visible = ((blk_id < q_tile) & (rank < MOBA_TOPK)) | (blk_id == q_tile)
    bias = jnp.where(visible, 0.0, MASK_VALUE).astype(F32)
    qt_ref[hd:hd + n_blk, :] = bias.astype(BF16)

    s = mask_own_tile(r[0:blk] + bias[0:1, :])
    m_new = jnp.max(s, axis=0, keepdims=True)
    acc_ref[...] = _dot(vt_ref[:, 0:blk], jnp.exp2(s - m_new).astype(BF16))
    m_ref[...] = m_new
    finish_tile(0)

    def scores(j):
        c0 = j * blk
        k_aug = jnp.concatenate([k_ref[c0:c0 + blk, :], jnp.where(lane_id == j, 1.0, 0.0).astype(BF16)], axis=1)
        s = mask_own_tile(_dot(k_aug, qt_ref[:, c0:]))
        return s, jnp.max(s, axis=0, keepdims=True)

    pending = scores(1)
    for j in range(1, n_blk):
        c0 = j * blk
        s, s_max = pending
        if j + 1 < n_blk:
            pending = scores(j + 1)
        m_old = m_ref[:, c0:]
        m_new = jnp.maximum(m_old, s_max)
        alpha = jnp.exp2(m_old - m_new)
        p = jnp.exp2(s - m_new).astype(BF16)
        acc_ref[:, c0:] = alpha * acc_ref[:, c0:] + _dot(vt_ref[:, c0:c0 + blk], p)
        m_ref[:, c0:] = m_new
        finish_tile(c0)


def _moba_attention(qkv, g_attn, bsz, seq_len, later_weights):
    n_tok = qkv.shape[0]
    nh = N_ATTN_HEADS
    n_blk = seq_len // MOBA_BLOCK
    steps = bsz * nh
    slab = lambda b, h: (b * nh + h, 0)
    assert seq_len % MOBA_BLOCK == 0 and n_blk % BF16_PACK == 0 and HEAD_DIM + n_blk <= MXU_DEPTH
    for w in later_weights:
        assert w.shape[0] % (steps * BF16_PACK) == 0, w.shape
    w_blocks = [(w.shape[0] // steps, w.shape[1]) for w in later_weights]
    outs = pl.pallas_call(
        functools.partial(_moba_kernel, len(later_weights)),
        grid=(bsz, nh),
        in_specs=[
            pl.BlockSpec((seq_len, HEAD_DIM), lambda b, h: (b, h)),
            pl.BlockSpec((seq_len, HEAD_DIM), lambda b, h: (b, nh + h)),
            pl.BlockSpec((seq_len, HEAD_DIM), lambda b, h: (b, 2 * nh + h)),
            pl.BlockSpec((1, HEAD_DIM), lambda b, h: (0, h)),
        ] + [pl.BlockSpec(blk, slab) for blk in w_blocks],
        out_specs=[pl.BlockSpec((seq_len, HEAD_DIM), lambda b, h: (b, h))]
        + [pl.BlockSpec(blk, slab) for blk in w_blocks],
        out_shape=[jax.ShapeDtypeStruct((n_tok, nh * HEAD_DIM), BF16)]
        + [jax.ShapeDtypeStruct(w.shape, BF16) for w in later_weights],
        scratch_shapes=[
            pltpu.VMEM((MXU_DEPTH, seq_len), BF16),
            pltpu.VMEM((HEAD_DIM + BF16_PACK, seq_len), BF16),
            pltpu.VMEM((1, seq_len), F32),
            pltpu.VMEM((HEAD_DIM + BF16_PACK, seq_len), F32),
        ],
        compiler_params=_params("parallel", "parallel"),
        name="moba_attention",
    )(qkv, qkv, qkv, g_attn, *later_weights)
    return outs[0], outs[1:]


def _mix_kernel(tiles_per_seq, a_ref, c_ref, wout_ref, x_ref, g_ref, wq_ref, mem_ref, gm_ref, wkv_ref, wo_ref,
                o_ref, kv_ref):
    @pl.when(pl.program_id(0) % tiles_per_seq == 0)
    def _():
        m = _rms_norm(mem_ref[0], gm_ref[...]).astype(BF16)
        kv_ref[...] = _dot(m, wkv_ref[...]).astype(BF16)

    half = x_ref.shape[0] // 2
    aw = a_ref.shape[1]
    halves = (slice(0, half), slice(half, 2 * half))

    def out_proj(rows):
        y = _dot(a_ref[rows, :], wout_ref[0:aw, :]) + _dot(c_ref[rows, :], wout_ref[aw:, :])
        return x_ref[rows, :] + y

    def queries(x1):
        h = _rms_norm(x1, g_ref[...]).astype(BF16)
        return (_dot(h, wq_ref[...]) * (XATTN_HEAD_DIM ** -0.5)).astype(BF16)

    def attend(q):
        heads = []
        for hh in range(N_XATTN_HEADS):
            lo = hh * XATTN_HEAD_DIM
            k = kv_ref[:, lo:lo + XATTN_HEAD_DIM]
            v = kv_ref[:, XATTN_WIDTH + lo:XATTN_WIDTH + lo + XATTN_HEAD_DIM]
            s = _dot_nt(q[:, lo:lo + XATTN_HEAD_DIM], k)
            p = jnp.exp(s - jnp.max(s, axis=-1, keepdims=True))
            o = _dot(p.astype(BF16), v) / jnp.sum(p, axis=-1, keepdims=True)
            heads.append(o.astype(BF16))
        return jnp.concatenate(heads, axis=1)

    x1 = [out_proj(rows) for rows in halves]
    q = [queries(t) for t in x1]
    for rows, t, qh in zip(halves, x1, q):
        o_ref[rows, :] = t + _dot(attend(qh), wo_ref[...])


def _mix(o_attn, y_conv, w_out_bf, x2d, g, w_xq_bf, mem, g_mem, w_xkv_bf, w_xo_bf, seq_len):
    n_tok, d_model = x2d.shape
    n_mem = mem.shape[1]
    tm = ROW_TILE
    assert n_tok % tm == 0 and seq_len % tm == 0
    tiles_per_seq = seq_len // tm
    const = lambda i: (0, 0)
    row = lambda i: (i, 0)
    resident = pl.Buffered(1)
    return pl.pallas_call(
        functools.partial(_mix_kernel, tiles_per_seq),
        grid=(n_tok // tm,),
        in_specs=[
            pl.BlockSpec((tm, o_attn.shape[1]), row),
            pl.BlockSpec((tm, y_conv.shape[1]), row),
            pl.BlockSpec(w_out_bf.shape, const, pipeline_mode=resident),
            pl.BlockSpec((tm, d_model), row),
            pl.BlockSpec((1, d_model), const),
            pl.BlockSpec(w_xq_bf.shape, const, pipeline_mode=resident),
            pl.BlockSpec((1, n_mem, d_model), lambda i: (i // tiles_per_seq, 0, 0)),
            pl.BlockSpec((1, d_model), const),
            pl.BlockSpec(w_xkv_bf.shape, const, pipeline_mode=resident),
            pl.BlockSpec(w_xo_bf.shape, const, pipeline_mode=resident),
        ],
        out_specs=pl.BlockSpec((tm, d_model), row),
        out_shape=jax.ShapeDtypeStruct((n_tok, d_model), F32),
        scratch_shapes=[pltpu.VMEM((n_mem, w_xkv_bf.shape[1]), BF16)],
        compiler_params=_params("arbitrary"),
        name="mix_out_xattn",
    )(o_attn, y_conv, w_out_bf, x2d, g, w_xq_bf, mem, g_mem, w_xkv_bf, w_xo_bf)


def _ffn_kernel(final_norm, x_ref, g_ref, wg_ref, wu_ref, wd_ref, gf_ref, o_ref, h_ref):
    j = pl.program_id(1)

    @pl.when(j == 0)
    def _():
        x = x_ref[...]
        h_ref[...] = _rms_norm(x, g_ref[...]).astype(BF16)
        o_ref[...] = x

    h = h_ref[...]
    half = wg_ref.shape[1] // 2
    gu = [(_dot(h, wg_ref[:, c:c + half]), _dot(h, wu_ref[:, c:c + half])) for c in (0, half)]
    down = [_dot((gate * jax.nn.sigmoid(gate) * up).astype(BF16), wd_ref[c:c + half, :])
            for c, (gate, up) in zip((0, half), gu)]
    o_ref[...] += down[0] + down[1]

    if final_norm:
        @pl.when(j == pl.num_programs(1) - 1)
        def _():
            o_ref[...] = _rms_norm(o_ref[...], gf_ref[...])


def _ffn(x2d, g, w_gate_bf, w_up_bf, w_down_bf, g_final, final_norm):
    n_tok, d_model = x2d.shape
    d_ff = w_gate_bf.shape[1]
    tm, tf = FFN_ROW_TILE, FF_TILE
    assert n_tok % tm == 0 and d_ff % tf == 0
    return pl.pallas_call(
        functools.partial(_ffn_kernel, final_norm),
        grid=(n_tok // tm, d_ff // tf),
        in_specs=[
            pl.BlockSpec((tm, d_model), lambda i, j: (i, 0)),
            pl.BlockSpec((1, d_model), lambda i, j: (0, 0)),
            pl.BlockSpec((d_model, tf), lambda i, j: (0, j)),
            pl.BlockSpec((d_model, tf), lambda i, j: (0, j)),
            pl.BlockSpec((tf, d_model), lambda i, j: (j, 0)),
            pl.BlockSpec((1, d_model), lambda i, j: (0, 0)),
        ],
        out_specs=pl.BlockSpec((tm, d_model), lambda i, j: (i, 0)),
        out_shape=jax.ShapeDtypeStruct((n_tok, d_model), F32),
        scratch_shapes=[pltpu.VMEM((tm, d_model), BF16)],
        compiler_params=_params("parallel", "arbitrary", vmem_limit=FFN_VMEM_LIMIT),
        name="ffn",
    )(x2d, g, w_gate_bf, w_up_bf, w_down_bf, g_final)


def kernel(x, mem, positions, g_mix, w_in, w_conv, b_conv, g_attn_out, g_conv_out, w_out, g_xattn, g_mem,
           w_xq, w_xkv, w_xo, g_ffn, w_gate, w_up, w_down, g_final):
    bsz, seq_len, d_model = x.shape
    depth = g_mix.shape[0]
    row = lambda a: a.reshape(1, -1)
    pos, inv_full, sign = _rope_inputs(positions)
    x2d = x.reshape(bsz * seq_len, d_model)
    for l in range(depth):
        qkv, y_conv = _in_proj(x2d, row(g_mix[l]), w_in[l].astype(BF16), pos, inv_full, sign, w_conv[l],
                               row(b_conv[l]), row(g_conv_out[l]), seq_len)
        o_attn, (w_out_bf, w_xq_bf, w_xkv_bf, w_xo_bf, w_gate_bf, w_up_bf, w_down_bf) = _moba_attention(
            qkv, row(g_attn_out[l]), bsz, seq_len,
            [w_out[l], w_xq[l], w_xkv[l], w_xo[l], w_gate[l], w_up[l], w_down[l]])
        x2d = _mix(o_attn, y_conv, w_out_bf, x2d, row(g_xattn[l]), w_xq_bf, mem, row(g_mem[l]), w_xkv_bf, w_xo_bf,
                   seq_len)
        x2d = _ffn(x2d, row(g_ffn[l]), w_gate_bf, w_up_bf, w_down_bf, row(g_final), final_norm=l == depth - 1)
    return x2d.reshape(bsz, seq_len, d_model)
```

```python
import functools

import jax
import jax.numpy as jnp
from jax import lax
from jax.experimental import pallas as pl
from jax.experimental.pallas import tpu as pltpu

HEAD_DIM = 128
N_ATTN_HEADS = 8
CONV_KSIZE = 3
MOBA_BLOCK = 256
MOBA_TOPK = 3
ROPE_THETA = 10000.0
N_XATTN_HEADS = 4
XATTN_HEAD_DIM = 128
XATTN_WIDTH = N_XATTN_HEADS * XATTN_HEAD_DIM
EPS = 1e-6

LOG2_E = 1.4426950408889634
MASK_VALUE = -1e30
SUBLANES = 8
VMEM_LIMIT = 56 * 1024 * 1024

ROW_TILE = 512
FF_TILE = 512
CONV_COL_GROUP = 256
FFN_ROW_TILE = 1024
FFN_VMEM_LIMIT = 60 * 1024 * 1024
FFN_WEIGHT_BUFFERS = 2
BF16_PACK = 16
MXU_DEPTH = 256

BF16 = jnp.bfloat16
F32 = jnp.float32


def _rms_norm(xf, g):
    y = xf * lax.rsqrt(jnp.mean(xf * xf, axis=-1, keepdims=True) + EPS)
    return y * g


def _dot(a, b):
    return jnp.dot(a, b, preferred_element_type=F32)


def _dot_nt(a, b):
    return lax.dot_general(a, b, (((1,), (1,)), ((), ())), preferred_element_type=F32)


def _params(*semantics, vmem_limit=VMEM_LIMIT):
    return pltpu.CompilerParams(dimension_semantics=semantics, vmem_limit_bytes=vmem_limit)


def _rope_inputs(positions):
    half = HEAD_DIM // 2
    inv_freq = ROPE_THETA ** (-jnp.arange(0, HEAD_DIM, 2, dtype=F32) / HEAD_DIM)
    inv_full = jnp.concatenate([inv_freq, inv_freq]).reshape(1, HEAD_DIM)
    sign = jnp.concatenate([-jnp.ones((half,), F32), jnp.ones((half,), F32)]).reshape(1, HEAD_DIM)
    return positions.astype(F32).reshape(positions.size, 1), inv_full, sign


def _in_proj_kernel(tiles_per_seq, x_ref, g_ref, w_ref, pos_ref, inv_ref, sign_ref, wc_ref, bc_ref, gc_ref,
                    qkv_ref, conv_ref, uext_ref, cos_ref, sin_ref):
    tm = x_ref.shape[0]
    aw = qkv_ref.shape[1] // 3
    cw = conv_ref.shape[1]
    cg = CONV_COL_GROUP

    @pl.when(pl.program_id(0) % tiles_per_seq == 0)
    def _():
        uext_ref[0:SUBLANES, :] = jnp.zeros((SUBLANES, cw), F32)

    h = _rms_norm(x_ref[...], g_ref[...]).astype(BF16)

    def project_conv(c0):
        return tuple(_dot(h, w_ref[:, 3 * aw + part * cw + c0:3 * aw + part * cw + c0 + cg]) for part in range(3))

    def finish_conv(c0, c_gate, b_gate, x_in):
        cols = slice(c0, c0 + cg)
        u = c_gate * x_in
        uext_ref[SUBLANES:, cols] = u
        u1 = uext_ref[SUBLANES - 1:SUBLANES - 1 + tm, cols]
        u2 = uext_ref[SUBLANES - 2:SUBLANES - 2 + tm, cols]
        conv = wc_ref[0:1, cols] * u2 + wc_ref[1:2, cols] * u1 + wc_ref[2:3, cols] * u
        y = b_gate * (conv + bc_ref[:, cols])
        uext_ref[0:SUBLANES, cols] = u[tm - SUBLANES:, :]
        for lo in range(0, cg, HEAD_DIM):
            yg = _rms_norm(y[:, lo:lo + HEAD_DIM], gc_ref[:, c0 + lo:c0 + lo + HEAD_DIM])
            conv_ref[:, c0 + lo:c0 + lo + HEAD_DIM] = yg.astype(BF16)

    def project_attn(part):
        return (_dot(h, w_ref[:, part * aw:(part + 1) * aw]),)

    def finish_attn(part, r):
        scale = HEAD_DIM ** -0.5 * LOG2_E
        for lo in range(0, aw, HEAD_DIM):
            t = r[:, lo:lo + HEAD_DIM]
            if part < 2:
                t = t * cos_ref[...] + pltpu.roll(t, HEAD_DIM // 2, axis=1) * sin_ref[...]
            if part == 0:
                t = t * scale
            qkv_ref[:, part * aw + lo:part * aw + lo + HEAD_DIM] = t.astype(BF16)

    units = [(project_conv, finish_conv, c0) for c0 in range(0, cw, cg)]
    units += [(project_attn, finish_attn, part) for part in (0, 1, 2)]
    pending = units[0][0](units[0][2])
    ang = pos_ref[...] * inv_ref[...]
    cos_ref[...] = jnp.cos(ang)
    sin_ref[...] = jnp.sin(ang) * sign_ref[...]
    for n, (_, finish, arg) in enumerate(units):
        current = pending
        if n + 1 < len(units):
            pending = units[n + 1][0](units[n + 1][2])
        finish(arg, *current)


def _in_proj(x2d, g, w_in_bf, pos, inv_full, sign, w_conv, b_conv, g_conv, seq_len):
    n_tok, d_model = x2d.shape
    cw = w_conv.shape[1]
    aw = (w_in_bf.shape[1] - 3 * cw) // 3
    tm = ROW_TILE
    assert n_tok % tm == 0 and seq_len % tm == 0 and cw % CONV_COL_GROUP == 0 and aw % HEAD_DIM == 0
    const = lambda i: (0, 0)
    row = lambda i: (i, 0)
    return pl.pallas_call(
        functools.partial(_in_proj_kernel, seq_len // tm),
        grid=(n_tok // tm,),
        in_specs=[
            pl.BlockSpec((tm, d_model), row),
            pl.BlockSpec((1, d_model), const),
            pl.BlockSpec(w_in_bf.shape, const, pipeline_mode=pl.Buffered(1)),
            pl.BlockSpec((tm, 1), row),
            pl.BlockSpec((1, HEAD_DIM), const),
            pl.BlockSpec((1, HEAD_DIM), const),
            pl.BlockSpec((CONV_KSIZE, cw), const),
            pl.BlockSpec((1, cw), const),
            pl.BlockSpec((1, cw), const),
        ],
        out_specs=[pl.BlockSpec((tm, 3 * aw), row), pl.BlockSpec((tm, cw), row)],
        out_shape=[jax.ShapeDtypeStruct((n_tok, 3 * aw), BF16), jax.ShapeDtypeStruct((n_tok, cw), BF16)],
        scratch_shapes=[pltpu.VMEM((tm + SUBLANES, cw), F32),
                        pltpu.VMEM((tm, HEAD_DIM), F32), pltpu.VMEM((tm, HEAD_DIM), F32)],
        compiler_params=_params("arbitrary"),
        name="in_proj",
    )(x2d, g, w_in_bf, pos, inv_full, sign, w_conv, b_conv, g_conv)


def _split_bf16(a, parts):
    out = []
    for _ in range(parts):
        piece = a.astype(BF16)
        out.append(piece)
        a = a - piece.astype(F32)
    return out


def _moba_kernel(n_cast, q_ref, k_ref, v_ref, g_ref, *refs):
    cast_in, o_ref, cast_out = refs[:n_cast], refs[n_cast], refs[n_cast + 1:2 * n_cast + 1]
    qt_ref, vt_ref, m_ref, acc_ref = refs[2 * n_cast + 1:]
    blk = MOBA_BLOCK
    hd = HEAD_DIM
    seq = k_ref.shape[0]
    n_blk = seq // blk

    for c in range(n_blk):
        rows = slice(c * blk, (c + 1) * blk)
        qt_ref[0:hd, rows] = q_ref[rows, :].astype(F32).T.astype(BF16)
        vt_ref[0:hd, rows] = v_ref[rows, :].astype(F32).T.astype(BF16)
    qt_ref[hd + n_blk:, :] = jnp.zeros((qt_ref.shape[0] - hd - n_blk, seq), BF16)
    vt_ref[hd:, :] = jnp.ones((vt_ref.shape[0] - hd, seq), BF16)

    kf = k_ref[...].astype(F32).reshape(n_blk, blk, hd)
    kmean = jnp.sum(kf, axis=1) * (1.0 / blk)
    kmean_parts = _split_bf16(kmean, 3)

    key_id = lax.broadcasted_iota(jnp.int32, (blk, blk), 0)
    qry_id = lax.broadcasted_iota(jnp.int32, (blk, blk), 1)
    causal = key_id <= qry_id
    lane_id = lax.broadcasted_iota(jnp.int32, (blk, hd), 1)

    def mask_own_tile(s):
        own = jnp.where(causal, s[:, :blk], MASK_VALUE)
        return own if s.shape[1] == blk else jnp.concatenate([own, s[:, blk:]], axis=1)

    def finish_tile(c0):
        o = (acc_ref[0:hd, c0:c0 + blk] / acc_ref[hd:hd + 1, c0:c0 + blk]).T
        o_ref[c0:c0 + blk, :] = _rms_norm(o, g_ref[...]).astype(BF16)

    r = _dot(jnp.concatenate([k_ref[0:blk, :]] + kmean_parts, axis=0), qt_ref[0:hd, :])
    for src, dst in zip(cast_in, cast_out):
        dst[...] = src[...].astype(BF16)
    gate = (r[blk:blk + n_blk] + r[blk + n_blk:blk + 2 * n_blk]) + r[blk + 2 * n_blk:blk + 3 * n_blk]
    blk_id = lax.broadcasted_iota(jnp.int32, (n_blk, seq), 0)
    q_tile = lax.broadcasted_iota(jnp.int32, (n_blk, seq), 1) // blk
    rank = jnp.zeros((n_blk, seq), jnp.int32)
    for j in range(n_blk):
        gj = gate[j:j + 1, :]
        beats = (gj > gate) | ((gj == gate) & (j < blk_id))
        rank = rank + jnp.where(beats & (j < q_tile), 1, 0)
    visible = ((blk_id < q_tile) & (rank < MOBA_TOPK)) | (blk_id == q_tile)
    bias = jnp.where(visible, 0.0, MASK_VALUE).astype(F32)
    qt_ref[hd:hd + n_blk, :] = bias.astype(BF16)

    s = mask_own_tile(r[0:blk] + bias[0:1, :])
    m_new = jnp.max(s, axis=0, keepdims=True)
    acc_ref[...] = _dot(vt_ref[:, 0:blk], jnp.exp2(s - m_new).astype(BF16))
    m_ref[...] = m_new
    finish_tile(0)

    def scores(j):
        c0 = j * blk
        k_aug = jnp.concatenate([k_ref[c0:c0 + blk, :], jnp.where(lane_id == j, 1.0, 0.0).astype(BF16)], axis=1)
        s = mask_own_tile(_dot(k_aug, qt_ref[:, c0:]))
        return s, jnp.max(s, axis=0, keepdims=True)

    pending = scores(1)
    for j in range(1, n_blk):
        c0 = j * blk
        s, s_max = pending
        if j + 1 < n_blk:
            pending = scores(j + 1)
        m_old = m_ref[:, c0:]
        m_new = jnp.maximum(m_old, s_max)
        alpha = jnp.exp2(m_old - m_new)
        p = jnp.exp2(s - m_new).astype(BF16)
        acc_ref[:, c0:] = alpha * acc_ref[:, c0:] + _dot(vt_ref[:, c0:c0 + blk], p)
        m_ref[:, c0:] = m_new
        finish_tile(c0)


def _moba_attention(qkv, g_attn, bsz, seq_len, later_weights):
    n_tok = qkv.shape[0]
    nh = N_ATTN_HEADS
    n_blk = seq_len // MOBA_BLOCK
    steps = bsz * nh
    slab = lambda b, h: (b * nh + h, 0)
    assert seq_len % MOBA_BLOCK == 0 and n_blk % BF16_PACK == 0 and HEAD_DIM + n_blk <= MXU_DEPTH
    for w in later_weights:
        assert w.shape[0] % (steps * BF16_PACK) == 0, w.shape
    w_blocks = [(w.shape[0] // steps, w.shape[1]) for w in later_weights]
    outs = pl.pallas_call(
        functools.partial(_moba_kernel, len(later_weights)),
        grid=(bsz, nh),
        in_specs=[
            pl.BlockSpec((seq_len, HEAD_DIM), lambda b, h: (b, h)),
            pl.BlockSpec((seq_len, HEAD_DIM), lambda b, h: (b, nh + h)),
            pl.BlockSpec((seq_len, HEAD_DIM), lambda b, h: (b, 2 * nh + h)),
            pl.BlockSpec((1, HEAD_DIM), lambda b, h: (0, h)),
        ] + [pl.BlockSpec(blk, slab) for blk in w_blocks],
        out_specs=[pl.BlockSpec((seq_len, HEAD_DIM), lambda b, h: (b, h))]
        + [pl.BlockSpec(blk, slab) for blk in w_blocks],
        out_shape=[jax.ShapeDtypeStruct((n_tok, nh * HEAD_DIM), BF16)]
        + [jax.ShapeDtypeStruct(w.shape, BF16) for w in later_weights],
        scratch_shapes=[
            pltpu.VMEM((MXU_DEPTH, seq_len), BF16),
            pltpu.VMEM((HEAD_DIM + BF16_PACK, seq_len), BF16),
            pltpu.VMEM((1, seq_len), F32),
            pltpu.VMEM((HEAD_DIM + BF16_PACK, seq_len), F32),
        ],
        compiler_params=_params("parallel", "parallel"),
        name="moba_attention",
    )(qkv, qkv, qkv, g_attn, *later_weights)
    return outs[0], outs[1:]


def _mix_kernel(tiles_per_seq, a_ref, c_ref, wout_ref, x_ref, g_ref, wq_ref, mem_ref, gm_ref, wkv_ref, wo_ref,
                o_ref, kv_ref):
    @pl.when(pl.program_id(0) % tiles_per_seq == 0)
    def _():
        m = _rms_norm(mem_ref[0], gm_ref[...]).astype(BF16)
        kv_ref[...] = _dot(m, wkv_ref[...]).astype(BF16)

    half = x_ref.shape[0] // 2
    aw = a_ref.shape[1]
    halves = (slice(0, half), slice(half, 2 * half))

    def out_proj(rows):
        y = _dot(a_ref[rows, :], wout_ref[0:aw, :]) + _dot(c_ref[rows, :], wout_ref[aw:, :])
        return x_ref[rows, :] + y

    def queries(x1):
        h = _rms_norm(x1, g_ref[...]).astype(BF16)
        return (_dot(h, wq_ref[...]) * (XATTN_HEAD_DIM ** -0.5)).astype(BF16)

    def attend(q):
        heads = []
        for hh in range(N_XATTN_HEADS):
            lo = hh * XATTN_HEAD_DIM
            k = kv_ref[:, lo:lo + XATTN_HEAD_DIM]
            v = kv_ref[:, XATTN_WIDTH + lo:XATTN_WIDTH + lo + XATTN_HEAD_DIM]
            s = _dot_nt(q[:, lo:lo + XATTN_HEAD_DIM], k)
            p = jnp.exp(s - jnp.max(s, axis=-1, keepdims=True))
            o = _dot(p.astype(BF16), v) / jnp.sum(p, axis=-1, keepdims=True)
            heads.append(o.astype(BF16))
        return jnp.concatenate(heads, axis=1)

    x1 = [out_proj(rows) for rows in halves]
    q = [queries(t) for t in x1]
    for rows, t, qh in zip(halves, x1, q):
        o_ref[rows, :] = t + _dot(attend(qh), wo_ref[...])


def _mix(o_attn, y_conv, w_out_bf, x2d, g, w_xq_bf, mem, g_mem, w_xkv_bf, w_xo_bf, seq_len):
    n_tok, d_model = x2d.shape
    n_mem = mem.shape[1]
    tm = ROW_TILE
    assert n_tok % tm == 0 and seq_len % tm == 0
    tiles_per_seq = seq_len // tm
    const = lambda i: (0, 0)
    row = lambda i: (i, 0)
    resident = pl.Buffered(1)
    return pl.pallas_call(
        functools.partial(_mix_kernel, tiles_per_seq),
        grid=(n_tok // tm,),
        in_specs=[
            pl.BlockSpec((tm, o_attn.shape[1]), row),
            pl.BlockSpec((tm, y_conv.shape[1]), row),
            pl.BlockSpec(w_out_bf.shape, const, pipeline_mode=resident),
            pl.BlockSpec((tm, d_model), row),
            pl.BlockSpec((1, d_model), const),
            pl.BlockSpec(w_xq_bf.shape, const, pipeline_mode=resident),
            pl.BlockSpec((1, n_mem, d_model), lambda i: (i // tiles_per_seq, 0, 0)),
            pl.BlockSpec((1, d_model), const),
            pl.BlockSpec(w_xkv_bf.shape, const, pipeline_mode=resident),
            pl.BlockSpec(w_xo_bf.shape, const, pipeline_mode=resident),
        ],
        out_specs=pl.BlockSpec((tm, d_model), row),
        out_shape=jax.ShapeDtypeStruct((n_tok, d_model), F32),
        scratch_shapes=[pltpu.VMEM((n_mem, w_xkv_bf.shape[1]), BF16)],
        compiler_params=_params("arbitrary"),
        name="mix_out_xattn",
    )(o_attn, y_conv, w_out_bf, x2d, g, w_xq_bf, mem, g_mem, w_xkv_bf, w_xo_bf)


def _ffn_kernel(final_norm, x_ref, g_ref, wg_hbm, wu_hbm, wd_hbm, gf_ref, o_ref, h_ref):
    d_model = x_ref.shape[1]
    d_ff = wg_hbm.shape[1]
    tf = FF_TILE
    x = x_ref[...]
    h_ref[...] = _rms_norm(x, g_ref[...]).astype(BF16)
    o_ref[...] = x

    def ff_tile(wg_ref, wu_ref, wd_ref):
        h = h_ref[...]
        half = tf // 2
        gu = [(_dot(h, wg_ref[:, c:c + half]), _dot(h, wu_ref[:, c:c + half])) for c in (0, half)]
        down = [_dot((gate * jax.nn.sigmoid(gate) * up).astype(BF16), wd_ref[c:c + half, :])
                for c, (gate, up) in zip((0, half), gu)]
        o_ref[...] += down[0] + down[1]

    buffers = pl.Buffered(FFN_WEIGHT_BUFFERS)
    pltpu.emit_pipeline(
        ff_tile,
        grid=(d_ff // tf,),
        in_specs=[
            pl.BlockSpec((d_model, tf), lambda j: (0, j), pipeline_mode=buffers),
            pl.BlockSpec((d_model, tf), lambda j: (0, j), pipeline_mode=buffers),
            pl.BlockSpec((tf, d_model), lambda j: (j, 0), pipeline_mode=buffers),
        ],
    )(wg_hbm, wu_hbm, wd_hbm)

    if final_norm:
        o_ref[...] = _rms_norm(o_ref[...], gf_ref[...])


def _ffn(x2d, g, w_gate_bf, w_up_bf, w_down_bf, g_final, final_norm):
    n_tok, d_model = x2d.shape
    d_ff = w_gate_bf.shape[1]
    tm, tf = FFN_ROW_TILE, FF_TILE
    assert n_tok % tm == 0 and d_ff % tf == 0
    hbm = pl.BlockSpec(memory_space=pl.ANY)
    return pl.pallas_call(
        functools.partial(_ffn_kernel, final_norm),
        grid=(n_tok // tm,),
        in_specs=[
            pl.BlockSpec((tm, d_model), lambda i: (i, 0)),
            pl.BlockSpec((1, d_model), lambda i: (0, 0)),
            hbm, hbm, hbm,
            pl.BlockSpec((1, d_model), lambda i: (0, 0)),
        ],
        out_specs=pl.BlockSpec((tm, d_model), lambda i: (i, 0)),
        out_shape=jax.ShapeDtypeStruct((n_tok, d_model), F32),
        scratch_shapes=[pltpu.VMEM((tm, d_model), BF16)],
        compiler_params=_params("arbitrary", vmem_limit=FFN_VMEM_LIMIT),
        name="ffn",
    )(x2d, g, w_gate_bf, w_up_bf, w_down_bf, g_final)


def kernel(x, mem, positions, g_mix, w_in, w_conv, b_conv, g_attn_out, g_conv_out, w_out, g_xattn, g_mem,
           w_xq, w_xkv, w_xo, g_ffn, w_gate, w_up, w_down, g_final):
    bsz, seq_len, d_model = x.shape
    depth = g_mix.shape[0]
    row = lambda a: a.reshape(1, -1)
    pos, inv_full, sign = _rope_inputs(positions)
    x2d = x.reshape(bsz * seq_len, d_model)
    for l in range(depth):
        qkv, y_conv = _in_proj(x2d, row(g_mix[l]), w_in[l].astype(BF16), pos, inv_full, sign, w_conv[l],
                               row(b_conv[l]), row(g_conv_out[l]), seq_len)
        o_attn, (w_out_bf, w_xq_bf, w_xkv_bf, w_xo_bf, w_gate_bf, w_up_bf, w_down_bf) = _moba_attention(
            qkv, row(g_attn_out[l]), bsz, seq_len,
            [w_out[l], w_xq[l], w_xkv[l], w_xo[l], w_gate[l], w_up[l], w_down[l]])
        x2d = _mix(o_attn, y_conv, w_out_bf, x2d, row(g_xattn[l]), w_xq_bf, mem, row(g_mem[l]), w_xkv_bf, w_xo_bf,
                   seq_len)
        x2d = _ffn(x2d, row(g_ffn[l]), w_gate_bf, w_up_bf, w_down_bf, row(g_final), final_norm=l == depth - 1)
    return x2d.reshape(bsz, seq_len, d_model)
```

```python
import functools

import jax
import jax.numpy as jnp
from jax import lax
from jax.experimental import pallas as pl
from jax.experimental.pallas import tpu as pltpu

HEAD_DIM = 128
N_ATTN_HEADS = 8
CONV_KSIZE = 3
MOBA_BLOCK = 256
MOBA_TOPK = 3
ROPE_THETA = 10000.0
N_XATTN_HEADS = 4
XATTN_HEAD_DIM = 128
XATTN_WIDTH = N_XATTN_HEADS * XATTN_HEAD_DIM
EPS = 1e-6

LOG2_E = 1.4426950408889634
MASK_VALUE = -1e30
SUBLANES = 8
VMEM_LIMIT = 56 * 1024 * 1024

ROW_TILE = 512
FF_TILE = 512
CONV_COL_GROUP = 256
FFN_ROW_TILE = 1024
FFN_VMEM_LIMIT = 60 * 1024 * 1024
BF16_PACK = 16
MXU_DEPTH = 256

BF16 = jnp.bfloat16
F32 = jnp.float32


def _rms_norm(xf, g):
    y = xf * lax.rsqrt(jnp.mean(xf * xf, axis=-1, keepdims=True) + EPS)
    return y * g


def _dot(a, b):
    return jnp.dot(a, b, preferred_element_type=F32)


def _dot_nt(a, b):
    return lax.dot_general(a, b, (((1,), (1,)), ((), ())), preferred_element_type=F32)


def _params(*semantics, vmem_limit=VMEM_LIMIT):
    return pltpu.CompilerParams(dimension_semantics=semantics, vmem_limit_bytes=vmem_limit)


def _rope_inputs(positions):
    half = HEAD_DIM // 2
    inv_freq = ROPE_THETA ** (-jnp.arange(0, HEAD_DIM, 2, dtype=F32) / HEAD_DIM)
    inv_full = jnp.concatenate([inv_freq, inv_freq]).reshape(1, HEAD_DIM)
    sign = jnp.concatenate([-jnp.ones((half,), F32), jnp.ones((half,), F32)]).reshape(1, HEAD_DIM)
    return positions.astype(F32).reshape(positions.size, 1), inv_full, sign


def _in_proj_kernel(tiles_per_seq, x_ref, g_ref, w_ref, pos_ref, inv_ref, sign_ref, wc_ref, bc_ref, gc_ref,
                    qkv_ref, conv_ref, uext_ref, cos_ref, sin_ref):
    tm = x_ref.shape[0]
    aw = qkv_ref.shape[1] // 3
    cw = conv_ref.shape[1]
    cg = CONV_COL_GROUP

    @pl.when(pl.program_id(0) % tiles_per_seq == 0)
    def _():
        uext_ref[0:SUBLANES, :] = jnp.zeros((SUBLANES, cw), F32)

    half = tm // 2
    h_halves = [_rms_norm(x_ref[rows, :], g_ref[...]).astype(BF16) for rows in (slice(0, half), slice(half, tm))]
    h = jnp.concatenate(h_halves, axis=0)

    def project_conv(c0, first=False):
        cols = [slice(3 * aw + part * cw + c0, 3 * aw + part * cw + c0 + cg) for part in range(3)]
        if first:
            parts = [[_dot(hh, w_ref[:, c]) for c in cols] for hh in h_halves]
            return tuple(jnp.concatenate([parts[0][p], parts[1][p]], axis=0) for p in range(3))
        return tuple(_dot(h, w_ref[:, c]) for c in cols)

    def finish_conv(c0, c_gate, b_gate, x_in):
        cols = slice(c0, c0 + cg)
        u = c_gate * x_in
        uext_ref[SUBLANES:, cols] = u
        u1 = uext_ref[SUBLANES - 1:SUBLANES - 1 + tm, cols]
        u2 = uext_ref[SUBLANES - 2:SUBLANES - 2 + tm, cols]
        conv = wc_ref[0:1, cols] * u2 + wc_ref[1:2, cols] * u1 + wc_ref[2:3, cols] * u
        y = b_gate * (conv + bc_ref[:, cols])
        uext_ref[0:SUBLANES, cols] = u[tm - SUBLANES:, :]
        for lo in range(0, cg, HEAD_DIM):
            yg = _rms_norm(y[:, lo:lo + HEAD_DIM], gc_ref[:, c0 + lo:c0 + lo + HEAD_DIM])
            conv_ref[:, c0 + lo:c0 + lo + HEAD_DIM] = yg.astype(BF16)

    def project_attn(part):
        return (_dot(h, w_ref[:, part * aw:(part + 1) * aw]),)

    def finish_attn(part, r):
        scale = HEAD_DIM ** -0.5 * LOG2_E
        for lo in range(0, aw, HEAD_DIM):
            t = r[:, lo:lo + HEAD_DIM]
            if part < 2:
                t = t * cos_ref[...] + pltpu.roll(t, HEAD_DIM // 2, axis=1) * sin_ref[...]
            if part == 0:
                t = t * scale
            qkv_ref[:, part * aw + lo:part * aw + lo + HEAD_DIM] = t.astype(BF16)

    units = [(project_conv, finish_conv, c0) for c0 in range(0, cw, cg)]
    units += [(project_attn, finish_attn, part) for part in (0, 1, 2)]
    pending = project_conv(units[0][2], first=True)
    ang = pos_ref[...] * inv_ref[...]
    cos_ref[...] = jnp.cos(ang)
    sin_ref[...] = jnp.sin(ang) * sign_ref[...]
    for n, (_, finish, arg) in enumerate(units):
        current = pending
        if n + 1 < len(units):
            pending = units[n + 1][0](units[n + 1][2])
        finish(arg, *current)


def _in_proj(x2d, g, w_in_bf, pos, inv_full, sign, w_conv, b_conv, g_conv, seq_len):
    n_tok, d_model = x2d.shape
    cw = w_conv.shape[1]
    aw = (w_in_bf.shape[1] - 3 * cw) // 3
    tm = ROW_TILE
    assert n_tok % tm == 0 and seq_len % tm == 0 and cw % CONV_COL_GROUP == 0 and aw % HEAD_DIM == 0
    const = lambda i: (0, 0)
    row = lambda i: (i, 0)
    return pl.pallas_call(
        functools.partial(_in_proj_kernel, seq_len // tm),
        grid=(n_tok // tm,),
        in_specs=[
            pl.BlockSpec((tm, d_model), row),
            pl.BlockSpec((1, d_model), const),
            pl.BlockSpec(w_in_bf.shape, const, pipeline_mode=pl.Buffered(1)),
            pl.BlockSpec((tm, 1), row),
            pl.BlockSpec((1, HEAD_DIM), const),
            pl.BlockSpec((1, HEAD_DIM), const),
            pl.BlockSpec((CONV_KSIZE, cw), const),
            pl.BlockSpec((1, cw), const),
            pl.BlockSpec((1, cw), const),
        ],
        out_specs=[pl.BlockSpec((tm, 3 * aw), row), pl.BlockSpec((tm, cw), row)],
        out_shape=[jax.ShapeDtypeStruct((n_tok, 3 * aw), BF16), jax.ShapeDtypeStruct((n_tok, cw), BF16)],
        scratch_shapes=[pltpu.VMEM((tm + SUBLANES, cw), F32),
                        pltpu.VMEM((tm, HEAD_DIM), F32), pltpu.VMEM((tm, HEAD_DIM), F32)],
        compiler_params=_params("arbitrary"),
        name="in_proj",
    )(x2d, g, w_in_bf, pos, inv_full, sign, w_conv, b_conv, g_conv)


def _split_bf16(a, parts):
    out = []
    for _ in range(parts):
        piece = a.astype(BF16)
        out.append(piece)
        a = a - piece.astype(F32)
    return out


def _moba_kernel(n_cast, q_ref, k_ref, v_ref, g_ref, *refs):
    cast_in, o_ref, cast_out = refs[:n_cast], refs[n_cast], refs[n_cast + 1:2 * n_cast + 1]
    qt_ref, vt_ref, m_ref, acc_ref = refs[2 * n_cast + 1:]
    blk = MOBA_BLOCK
    hd = HEAD_DIM
    seq = k_ref.shape[0]
    n_blk = seq // blk

    for c in range(n_blk):
        rows = slice(c * blk, (c + 1) * blk)
        qt_ref[0:hd, rows] = q_ref[rows, :].astype(F32).T.astype(BF16)
        vt_ref[0:hd, rows] = v_ref[rows, :].astype(F32).T.astype(BF16)
    qt_ref[hd + n_blk:, :] = jnp.zeros((qt_ref.shape[0] - hd - n_blk, seq), BF16)
    vt_ref[hd:, :] = jnp.ones((vt_ref.shape[0] - hd, seq), BF16)

    kf = k_ref[...].astype(F32).reshape(n_blk, blk, hd)
    kmean = jnp.sum(kf, axis=1) * (1.0 / blk)
    kmean_parts = _split_bf16(kmean, 3)

    key_id = lax.broadcasted_iota(jnp.int32, (blk, blk), 0)
    qry_id = lax.broadcasted_iota(jnp.int32, (blk, blk), 1)
    causal = key_id <= qry_id
    lane_id = lax.broadcasted_iota(jnp.int32, (blk, hd), 1)

    def mask_own_tile(s):
        own = jnp.where(causal, s[:, :blk], MASK_VALUE)
        return own if s.shape[1] == blk else jnp.concatenate([own, s[:, blk:]], axis=1)

    def finish_tile(c0):
        o = (acc_ref[0:hd, c0:c0 + blk] / acc_ref[hd:hd + 1, c0:c0 + blk]).T
        o_ref[c0:c0 + blk, :] = _rms_norm(o, g_ref[...]).astype(BF16)

    r = _dot(jnp.concatenate([k_ref[0:blk, :]] + kmean_parts, axis=0), qt_ref[0:hd, :])
    for src, dst in zip(cast_in, cast_out):
        dst[...] = src[...].astype(BF16)
    gate = (r[blk:blk + n_blk] + r[blk + n_blk:blk + 2 * n_blk]) + r[blk + 2 * n_blk:blk + 3 * n_blk]
    blk_id = lax.broadcasted_iota(jnp.int32, (n_blk, seq), 0)
    q_tile = lax.broadcasted_iota(jnp.int32, (n_blk, seq), 1) // blk
    rank = jnp.zeros((n_blk, seq), jnp.int32)
    for j in range(n_blk):
        gj = gate[j:j + 1, :]
        beats = (gj > gate) | ((gj == gate) & (j < blk_id))
        rank = rank + jnp.where(beats & (j < q_tile), 1, 0)
    visible = ((blk_id < q_tile) & (rank < MOBA_TOPK)) | (blk_id == q_tile)
    bias = jnp.where(visible, 0.0, MASK_VALUE).astype(F32)
    qt_ref[hd:hd + n_blk, :] = bias.astype(BF16)

    s = mask_own_tile(r[0:blk] + bias[0:1, :])
    m_new = jnp.max(s, axis=0, keepdims=True)
    acc_ref[...] = _dot(vt_ref[:, 0:blk], jnp.exp2(s - m_new).astype(BF16))
    m_ref[...] = m_new
    finish_tile(0)

    def scores(j):
        c0 = j * blk
        k_aug = jnp.concatenate([k_ref[c0:c0 + blk, :], jnp.where(lane_id == j, 1.0, 0.0).astype(BF16)], axis=1)
        s = mask_own_tile(_dot(k_aug, qt_ref[:, c0:]))
        return s, jnp.max(s, axis=0, keepdims=True)

    pending = scores(1)
    for j in range(1, n_blk):
        c0 = j * blk
        s, s_max = pending
        if j + 1 < n_blk:
            pending = scores(j + 1)
        m_old = m_ref[:, c0:]
        m_new = jnp.maximum(m_old, s_max)
        alpha = jnp.exp2(m_old - m_new)
        p = jnp.exp2(s - m_new).astype(BF16)
        acc_ref[:, c0:] = alpha * acc_ref[:, c0:] + _dot(vt_ref[:, c0:c0 + blk], p)
        m_ref[:, c0:] = m_new
        finish_tile(c0)


def _moba_attention(qkv, g_attn, bsz, seq_len, later_weights):
    n_tok = qkv.shape[0]
    nh = N_ATTN_HEADS
    n_blk = seq_len // MOBA_BLOCK
    steps = bsz * nh
    slab = lambda b, h: (b * nh + h, 0)
    assert seq_len % MOBA_BLOCK == 0 and n_blk % BF16_PACK == 0 and HEAD_DIM + n_blk <= MXU_DEPTH
    for w in later_weights:
        assert w.shape[0] % (steps * BF16_PACK) == 0, w.shape
    w_blocks = [(w.shape[0] // steps, w.shape[1]) for w in later_weights]
    outs = pl.pallas_call(
        functools.partial(_moba_kernel, len(later_weights)),
        grid=(bsz, nh),
        in_specs=[
            pl.BlockSpec((seq_len, HEAD_DIM), lambda b, h: (b, h)),
            pl.BlockSpec((seq_len, HEAD_DIM), lambda b, h: (b, nh + h)),
            pl.BlockSpec((seq_len, HEAD_DIM), lambda b, h: (b, 2 * nh + h)),
            pl.BlockSpec((1, HEAD_DIM), lambda b, h: (0, h)),
        ] + [pl.BlockSpec(blk, slab) for blk in w_blocks],
        out_specs=[pl.BlockSpec((seq_len, HEAD_DIM), lambda b, h: (b, h))]
        + [pl.BlockSpec(blk, slab) for blk in w_blocks],
        out_shape=[jax.ShapeDtypeStruct((n_tok, nh * HEAD_DIM), BF16)]
        + [jax.ShapeDtypeStruct(w.shape, BF16) for w in later_weights],
        scratch_shapes=[
            pltpu.VMEM((MXU_DEPTH, seq_len), BF16),
            pltpu.VMEM((HEAD_DIM + BF16_PACK, seq_len), BF16),
            pltpu.VMEM((1, seq_len), F32),
            pltpu.VMEM((HEAD_DIM + BF16_PACK, seq_len), F32),
        ],
        compiler_params=_params("parallel", "parallel"),
        name="moba_attention",
    )(qkv, qkv, qkv, g_attn, *later_weights)
    return outs[0], outs[1:]


def _mix_kernel(tiles_per_seq, a_ref, c_ref, wout_ref, x_ref, g_ref, wq_ref, mem_ref, gm_ref, wkv_ref, wo_ref,
                o_ref, kv_ref):
    @pl.when(pl.program_id(0) % tiles_per_seq == 0)
    def _():
        m = _rms_norm(mem_ref[0], gm_ref[...]).astype(BF16)
        kv_ref[...] = _dot(m, wkv_ref[...]).astype(BF16)

    half = x_ref.shape[0] // 2
    aw = a_ref.shape[1]
    halves = (slice(0, half), slice(half, 2 * half))

    def out_proj(rows):
        y = _dot(a_ref[rows, :], wout_ref[0:aw, :]) + _dot(c_ref[rows, :], wout_ref[aw:, :])
        return x_ref[rows, :] + y

    def queries(x1):
        h = _rms_norm(x1, g_ref[...]).astype(BF16)
        return (_dot(h, wq_ref[...]) * (XATTN_HEAD_DIM ** -0.5 * LOG2_E)).astype(BF16)

    def attend(q):
        heads = []
        for hh in range(N_XATTN_HEADS):
            lo = hh * XATTN_HEAD_DIM
            k = kv_ref[:, lo:lo + XATTN_HEAD_DIM]
            v = kv_ref[:, XATTN_WIDTH + lo:XATTN_WIDTH + lo + XATTN_HEAD_DIM]
            s = _dot_nt(q[:, lo:lo + XATTN_HEAD_DIM], k)
            p = jnp.exp2(s - jnp.max(s, axis=-1, keepdims=True))
            o = _dot(p.astype(BF16), v) / jnp.sum(p, axis=-1, keepdims=True)
            heads.append(o.astype(BF16))
        return jnp.concatenate(heads, axis=1)

    x1 = [out_proj(rows) for rows in halves]
    q = [queries(t) for t in x1]
    for rows, t, qh in zip(halves, x1, q):
        o_ref[rows, :] = t + _dot(attend(qh), wo_ref[...])


def _mix(o_attn, y_conv, w_out_bf, x2d, g, w_xq_bf, mem, g_mem, w_xkv_bf, w_xo_bf, seq_len):
    n_tok, d_model = x2d.shape
    n_mem = mem.shape[1]
    tm = ROW_TILE
    assert n_tok % tm == 0 and seq_len % tm == 0
    tiles_per_seq = seq_len // tm
    const = lambda i: (0, 0)
    row = lambda i: (i, 0)
    resident = pl.Buffered(1)
    return pl.pallas_call(
        functools.partial(_mix_kernel, tiles_per_seq),
        grid=(n_tok // tm,),
        in_specs=[
            pl.BlockSpec((tm, o_attn.shape[1]), row),
            pl.BlockSpec((tm, y_conv.shape[1]), row),
            pl.BlockSpec(w_out_bf.shape, const, pipeline_mode=resident),
            pl.BlockSpec((tm, d_model), row),
            pl.BlockSpec((1, d_model), const),
            pl.BlockSpec(w_xq_bf.shape, const, pipeline_mode=resident),
            pl.BlockSpec((1, n_mem, d_model), lambda i: (i // tiles_per_seq, 0, 0)),
            pl.BlockSpec((1, d_model), const),
            pl.BlockSpec(w_xkv_bf.shape, const, pipeline_mode=resident),
            pl.BlockSpec(w_xo_bf.shape, const, pipeline_mode=resident),
        ],
        out_specs=pl.BlockSpec((tm, d_model), row),
        out_shape=jax.ShapeDtypeStruct((n_tok, d_model), F32),
        scratch_shapes=[pltpu.VMEM((n_mem, w_xkv_bf.shape[1]), BF16)],
        compiler_params=_params("arbitrary"),
        name="mix_out_xattn",
    )(o_attn, y_conv, w_out_bf, x2d, g, w_xq_bf, mem, g_mem, w_xkv_bf, w_xo_bf)


def _ffn_kernel(final_norm, x_ref, g_ref, wg_ref, wu_ref, wd_ref, gf_ref, o_ref, h_ref):
    j = pl.program_id(1)

    @pl.when(j == 0)
    def _():
        x = x_ref[...]
        h_ref[...] = _rms_norm(x, g_ref[...]).astype(BF16)
        o_ref[...] = x

    h = h_ref[...]
    half = wg_ref.shape[1] // 2
    gu = [(_dot(h, wg_ref[:, c:c + half]), _dot(h, wu_ref[:, c:c + half])) for c in (0, half)]
    down = [_dot((gate * jax.nn.sigmoid(gate) * up).astype(BF16), wd_ref[c:c + half, :])
            for c, (gate, up) in zip((0, half), gu)]
    o_ref[...] += down[0] + down[1]

    if final_norm:
        @pl.when(j == pl.num_programs(1) - 1)
        def _():
            o_ref[...] = _rms_norm(o_ref[...], gf_ref[...])


def _ffn(x2d, g, w_gate_bf, w_up_bf, w_down_bf, g_final, final_norm):
    n_tok, d_model = x2d.shape
    d_ff = w_gate_bf.shape[1]
    tm, tf = FFN_ROW_TILE, FF_TILE
    assert n_tok % tm == 0 and d_ff % tf == 0
    return pl.pallas_call(
        functools.partial(_ffn_kernel, final_norm),
        grid=(n_tok // tm, d_ff // tf),
        in_specs=[
            pl.BlockSpec((tm, d_model), lambda i, j: (i, 0)),
            pl.BlockSpec((1, d_model), lambda i, j: (0, 0)),
            pl.BlockSpec((d_model, tf), lambda i, j: (0, j)),
            pl.BlockSpec((d_model, tf), lambda i, j: (0, j)),
            pl.BlockSpec((tf, d_model), lambda i, j: (j, 0)),
            pl.BlockSpec((1, d_model), lambda i, j: (0, 0)),
        ],
        out_specs=pl.BlockSpec((tm, d_model), lambda i, j: (i, 0)),
        out_shape=jax.ShapeDtypeStruct((n_tok, d_model), F32),
        scratch_shapes=[pltpu.VMEM((tm, d_model), BF16)],
        compiler_params=_params("parallel", "arbitrary", vmem_limit=FFN_VMEM_LIMIT),
        name="ffn",
    )(x2d, g, w_gate_bf, w_up_bf, w_down_bf, g_final)


def kernel(x, mem, positions, g_mix, w_in, w_conv, b_conv, g_attn_out, g_conv_out, w_out, g_xattn, g_mem,
           w_xq, w_xkv, w_xo, g_ffn, w_gate, w_up, w_down, g_final):
    bsz, seq_len, d_model = x.shape
    depth = g_mix.shape[0]
    row = lambda a: a.reshape(1, -1)
    pos, inv_full, sign = _rope_inputs(positions)
    x2d = x.reshape(bsz * seq_len, d_model)
    for l in range(depth):
        qkv, y_conv = _in_proj(x2d, row(g_mix[l]), w_in[l].astype(BF16), pos, inv_full, sign, w_conv[l],
                               row(b_conv[l]), row(g_conv_out[l]), seq_len)
        o_attn, (w_out_bf, w_xq_bf, w_xkv_bf, w_xo_bf, w_gate_bf, w_up_bf, w_down_bf) = _moba_attention(
            qkv, row(g_attn_out[l]), bsz, seq_len,
            [w_out[l], w_xq[l], w_xkv[l], w_xo[l], w_gate[l], w_up[l], w_down[l]])
        x2d = _mix(o_attn, y_conv, w_out_bf, x2d, row(g_xattn[l]), w_xq_bf, mem, row(g_mem[l]), w_xkv_bf, w_xo_bf,
                   seq_len)
        x2d = _ffn(x2d, row(g_ffn[l]), w_gate_bf, w_up_bf, w_down_bf, row(g_final), final_norm=l == depth - 1)
    return x2d.reshape(bsz, seq_len, d_model)
```
